```python
import functools
import jax, jax.numpy as jnp
from jax import lax
import numpy as np

D_MODEL = 4096
BATCH = 2
SEQ = 4096
DEPTH = 1
DEC_BATCH = 32
DEC_SEQ = 4
PAST_LEN = 8192
PAGE_SIZE = 128

GLA_HEADS = 8
GLA_DK = 128
GLA_DV = 256
GLA_GATE_RANK = 16
GLA_GATE_NORM = 16.0
GLA_CHUNK = 32
NSA_HEADS = 16
NSA_KV_HEADS = 4
NSA_HEAD_DIM = 128
NSA_GROUP = NSA_HEADS // NSA_KV_HEADS
CMP_BLOCK = 32
SEL_BLOCK = 64
N_SEL = 16
WINDOW = 512
Q_BLOCK = 64
ROPE_THETA = 10000.0
N_EXPERTS = 64
TOP_K = 8
N_GROUPS = 8
TOPK_GROUPS = 4
EXPERT_HIDDEN = 1024
SHARED_HIDDEN = 1024
ROUTED_SCALE = 2.5
MOE_BLOCK = 64
DN_ALPHA = (2.0 * DEPTH) ** 0.25
DN_BETA = (8.0 * DEPTH) ** -0.25
LN_EPS = 1e-5
NEG_INF = -1e30

GLA_QK = GLA_HEADS * GLA_DK
GLA_V = GLA_HEADS * GLA_DV
NSA_Q = NSA_HEADS * NSA_HEAD_DIM
NSA_KV = NSA_KV_HEADS * NSA_HEAD_DIM
IN_LAYOUT = (('gla_q', GLA_QK), ('gla_k', GLA_QK), ('gla_v', GLA_V), ('gla_r', GLA_V),
             ('gla_a', GLA_GATE_RANK), ('nsa_q', NSA_Q), ('nsa_kc', NSA_KV), ('nsa_vc', NSA_KV),
             ('nsa_ks', NSA_KV), ('nsa_vs', NSA_KV), ('nsa_kw', NSA_KV), ('nsa_vw', NSA_KV),
             ('nsa_g', NSA_HEADS * 3), ('mg_gla', D_MODEL), ('mg_nsa', D_MODEL))
IN_NAMES = tuple(n for n, _ in IN_LAYOUT)
IN_SPLITS = tuple(int(v) for v in np.cumsum([c for _, c in IN_LAYOUT])[:-1])
N_IN = sum(c for _, c in IN_LAYOUT)

kernel_name = 'hybrid_gla_nsa_moe_deepnorm_adaln_step'


def _ln(x, g=None, b=None):
    xf = x.astype(jnp.float32)
    mu = xf.mean(-1, keepdims=True)
    var = jnp.square(xf - mu).mean(-1, keepdims=True)
    y = (xf - mu) * lax.rsqrt(var + LN_EPS)
    if g is not None:
        y = y * g.astype(jnp.float32) + b.astype(jnp.float32)
    return y.astype(x.dtype)


def _rope(x, pos):
    half = x.shape[-1] // 2
    inv = jnp.power(ROPE_THETA, -jnp.arange(half, dtype=jnp.float32) / half)
    ang = pos.astype(jnp.float32)[:, None] * inv[None, :]
    cos, sin = jnp.cos(ang)[:, None, :], jnp.sin(ang)[:, None, :]
    x1 = x[..., :half].astype(jnp.float32)
    x2 = x[..., half:].astype(jnp.float32)
    return jnp.concatenate([x1 * cos - x2 * sin, x2 * cos + x1 * sin], -1).astype(x.dtype)


def _gla_scan(q, k, v, glog, s0):
    B, S, H, _ = q.shape
    DV = v.shape[-1]
    C = GLA_CHUNK
    pad = (-S) % C
    n = (S + pad) // C

    def prep(t):
        t = jnp.pad(t.astype(jnp.float32), ((0, 0), (0, pad), (0, 0), (0, 0)))
        return t.reshape(B, n, C, H, t.shape[-1]).transpose(0, 3, 1, 2, 4)

    q, k, v, glog = prep(q), prep(k), prep(v), prep(glog)
    bcum = jnp.cumsum(glog, axis=3)
    b_last = bcum[:, :, :, -1:, :]
    qg = q * jnp.exp(bcum)
    kg = k * jnp.exp(-bcum)
    kd = k * jnp.exp(b_last - bcum)
    causal = jnp.tril(jnp.ones((C, C), bool))
    att = jnp.where(causal, jnp.einsum('bhncd,bhnsd->bhncs', qg, kg), 0.0)
    o_intra = jnp.einsum('bhncs,bhnsv->bhncv', att, v)

    def step(s, xs):
        kd_c, v_c, dec_c = xs
        return dec_c[..., None] * s + jnp.einsum('bhcd,bhcv->bhdv', kd_c, v_c), s

    xs = (kd.transpose(2, 0, 1, 3, 4), v.transpose(2, 0, 1, 3, 4),
          jnp.exp(b_last[:, :, :, 0, :]).transpose(2, 0, 1, 3))
    s_fin, s_start = lax.scan(step, s0.astype(jnp.float32), xs)
    o_inter = jnp.einsum('bhncd,nbhdv->bhncv', qg, s_start)
    o = (o_intra + o_inter).transpose(0, 2, 3, 1, 4).reshape(B, n * C, H, DV)[:, :S]
    return o, s_fin


def _gla_mixer(parts, s0, w_a2, b_a2, norm_g):
    B, S = parts['gla_q'].shape[:2]
    dt = parts['gla_q'].dtype
    q = parts['gla_q'].reshape(B, S, GLA_HEADS, GLA_DK) * (GLA_DK ** -0.5)
    k = parts['gla_k'].reshape(B, S, GLA_HEADS, GLA_DK)
    v = parts['gla_v'].reshape(B, S, GLA_HEADS, GLA_DV)
    glog = jax.nn.log_sigmoid((parts['gla_a'] @ w_a2 + b_a2).astype(jnp.float32)) / GLA_GATE_NORM
    o, s_fin = _gla_scan(q, k, v, glog.reshape(B, S, GLA_HEADS, GLA_DK), s0)
    o = o * lax.rsqrt(jnp.mean(o * o, -1, keepdims=True) + LN_EPS) * norm_g.astype(jnp.float32)
    o = o.reshape(B, S, GLA_V).astype(dt) * jax.nn.silu(parts['gla_r'])
    return o, s_fin.astype(dt)


def _compress(kc, vc, pe_k, pe_v, w_k, w_v):
    B, Sk, G, HD = kc.shape
    n = Sk // CMP_BLOCK
    wk = jax.nn.softmax(w_k.astype(jnp.float32))
    wv = jax.nn.softmax(w_v.astype(jnp.float32))
    kb = kc.reshape(B, n, CMP_BLOCK, G, HD).astype(jnp.float32) + pe_k.astype(jnp.float32)[:, None, :]
    vb = vc.reshape(B, n, CMP_BLOCK, G, HD).astype(jnp.float32) + pe_v.astype(jnp.float32)[:, None, :]
    kb = jnp.einsum('bnlgd,l->bngd', kb, wk)
    vb = jnp.einsum('bnlgd,l->bngd', vb, wv)
    kb = _rope(kb, (jnp.arange(n, dtype=jnp.int32) + 1) * CMP_BLOCK - 1)
    return kb, vb


def _nsa_block(q, qpos, kcb, vcb, fetch, kw, vw, kwpos, gates):
    B, Qb = q.shape[:2]
    G, R, HD = NSA_KV_HEADS, NSA_GROUP, NSA_HEAD_DIM
    f32 = jnp.float32
    qg = q.reshape(B, Qb, G, R, HD).astype(f32) * (HD ** -0.5)
    nbc = kcb.shape[1]
    s_c = jnp.einsum('bqgrd,bngd->bgrqn', qg, kcb.astype(f32))
    m_c = ((jnp.arange(nbc) + 1) * CMP_BLOCK - 1)[None, :] <= qpos[:, None]
    p_c = jax.nn.softmax(jnp.where(m_c, s_c, NEG_INF), axis=-1) * m_c
    o_c = jnp.einsum('bgrqn,bngd->bqgrd', p_c, vcb.astype(f32))
    ratio = SEL_BLOCK // CMP_BLOCK
    nbs = nbc // ratio
    imp = p_c.sum(axis=2).reshape(B, G, Qb, nbs, ratio).sum(-1)
    j = jnp.arange(nbs)[None, :]
    cur = (qpos // SEL_BLOCK)[:, None]
    forced = (j == 0) | (j == cur) | (j == cur - 1)
    imp = jnp.where(forced, jnp.inf, jnp.where(j <= cur, imp, -jnp.inf))
    _, idx = lax.top_k(imp, min(N_SEL, nbs))
    k_sel, v_sel = fetch(idx)
    kpos = idx[..., None] * SEL_BLOCK + jnp.arange(SEL_BLOCK)
    m_s = (kpos <= qpos[None, None, :, None, None])[:, :, None]
    s_s = jnp.where(m_s, jnp.einsum('bqgrd,bgqnld->bgrqnl', qg, k_sel.astype(f32)), NEG_INF)
    p_s = jax.nn.softmax(s_s.reshape(s_s.shape[:4] + (-1,)), axis=-1).reshape(s_s.shape)
    o_s = jnp.einsum('bgrqnl,bgqnld->bqgrd', p_s, v_sel.astype(f32))
    m_w = ((kwpos[None, :] <= qpos[:, None]) & (kwpos[None, :] > qpos[:, None] - WINDOW)
           & (kwpos[None, :] >= 0))
    s_w = jnp.einsum('bqgrd,bkgd->bgrqk', qg, kw.astype(f32))
    p_w = jax.nn.softmax(jnp.where(m_w, s_w, NEG_INF), axis=-1)
    o_w = jnp.einsum('bgrqk,bkgd->bqgrd', p_w, vw.astype(f32))
    g = gates.reshape(B, Qb, G, R, 3).astype(f32)
    o = g[..., 0:1] * o_c + g[..., 1:2] * o_s + g[..., 2:3] * o_w
    return o.reshape(B, Qb, G * R * HD).astype(q.dtype)


def _nsa_heads(parts, pos):
    B, S = parts['nsa_q'].shape[:2]
    G, HD = NSA_KV_HEADS, NSA_HEAD_DIM
    hd = lambda name, n: parts[name].reshape(B, S, n, HD)
    q = _rope(hd('nsa_q', NSA_HEADS), pos)
    kc, vc = hd('nsa_kc', G), hd('nsa_vc', G)
    ks, vs = _rope(hd('nsa_ks', G), pos), hd('nsa_vs', G)
    kw, vw = _rope(hd('nsa_kw', G), pos), hd('nsa_vw', G)
    gates = jax.nn.sigmoid(parts['nsa_g'].astype(jnp.float32)).reshape(B, S, NSA_HEADS, 3)
    return q, kc, vc, ks, vs, kw, vw, gates


def _pad_rows(t, total):
    return jnp.pad(t, ((0, 0), (0, total - t.shape[1])) + ((0, 0),) * (t.ndim - 2))


def _nsa_prompt(parts, pe_k, pe_v, w_k, w_v):
    B, S = parts['nsa_q'].shape[:2]
    G, HD = NSA_KV_HEADS, NSA_HEAD_DIM
    pos = jnp.arange(S, dtype=jnp.int32)
    q, kc, vc, ks, vs, kw, vw, gates = _nsa_heads(parts, pos)
    sk = -(-S // SEL_BLOCK) * SEL_BLOCK
    kcb, vcb = _compress(_pad_rows(kc, sk), _pad_rows(vc, sk), pe_k, pe_v, w_k, w_v)
    ks_blk = _pad_rows(ks, sk).reshape(B, sk // SEL_BLOCK, SEL_BLOCK, G, HD)
    vs_blk = _pad_rows(vs, sk).reshape(B, sk // SEL_BLOCK, SEL_BLOCK, G, HD)
    bi = jnp.arange(B)[:, None, None, None]
    gi = jnp.arange(G)[None, :, None, None]

    def fetch(idx):
        return ks_blk[bi, idx, :, gi], vs_blk[bi, idx, :, gi]

    kw_pad = jnp.pad(kw, ((0, 0), (WINDOW, 0), (0, 0), (0, 0)))
    vw_pad = jnp.pad(vw, ((0, 0), (WINDOW, 0), (0, 0), (0, 0)))
    nq = S // Q_BLOCK
    blocks = lambda t: t.reshape((B, nq, Q_BLOCK) + t.shape[2:]).swapaxes(0, 1)

    def body(xs):
        qb, gb, i = xs
        start = i * Q_BLOCK
        qpos = start + jnp.arange(Q_BLOCK, dtype=jnp.int32)
        kwb = lax.dynamic_slice_in_dim(kw_pad, start, WINDOW + Q_BLOCK, axis=1)
        vwb = lax.dynamic_slice_in_dim(vw_pad, start, WINDOW + Q_BLOCK, axis=1)
        kwpos = start - WINDOW + jnp.arange(WINDOW + Q_BLOCK, dtype=jnp.int32)
        return _nsa_block(qb, qpos, kcb, vcb, fetch, kwb, vwb, kwpos, gb)

    o = lax.map(body, (blocks(q), blocks(gates), jnp.arange(nq, dtype=jnp.int32)))
    o = o.swapaxes(0, 1).reshape(B, S, NSA_Q)
    win_buf = min(WINDOW, PAST_LEN)
    win = jnp.stack([kw, vw], 2)
    win = jnp.pad(win, ((0, 0), (max(win_buf - S, 0), 0), (0, 0), (0, 0), (0, 0)))[:, -win_buf:]
    return o, jnp.stack([kc, vc], 2), jnp.stack([ks, vs], 2), win


def _nsa_sample(parts, cache_cmp, cache_slc, cache_win, page_table, pe_k, pe_v, w_k, w_v):
    DB, T = parts['nsa_q'].shape[:2]
    G, HD = NSA_KV_HEADS, NSA_HEAD_DIM
    n_pages = page_table.shape[1]
    past_len = n_pages * PAGE_SIZE
    pos = past_len + jnp.arange(T, dtype=jnp.int32)
    q, kc, vc, ks, vs, kw, vw, gates = _nsa_heads(parts, pos)
    past_cmp = cache_cmp[page_table].reshape(DB, past_len, 2, G, HD)
    total = past_len + T
    sk = -(-total // SEL_BLOCK) * SEL_BLOCK
    kc_all = _pad_rows(jnp.concatenate([past_cmp[:, :, 0], kc], 1), sk)
    vc_all = _pad_rows(jnp.concatenate([past_cmp[:, :, 1], vc], 1), sk)
    kcb, vcb = _compress(kc_all, vc_all, pe_k, pe_v, w_k, w_v)
    bpp = PAGE_SIZE // SEL_BLOCK
    nb_past = past_len // SEL_BLOCK
    n_new = sk // SEL_BLOCK - nb_past
    pool_blk = cache_slc.reshape((-1, SEL_BLOCK, 2, G, HD))
    new_slc = jnp.stack([ks, vs], 2)
    new_blk = _pad_rows(new_slc, n_new * SEL_BLOCK).reshape(DB, n_new, SEL_BLOCK, 2, G, HD)
    bi = jnp.arange(DB)[:, None, None, None]
    gi = jnp.arange(G)[None, :, None, None]

    def fetch(idx):
        jp = jnp.minimum(idx, nb_past - 1)
        phys = page_table[bi, jp // bpp] * bpp + jp % bpp
        past = pool_blk[phys, :, :, gi]
        new = new_blk[bi, jnp.clip(idx - nb_past, 0, n_new - 1), :, :, gi]
        sel = jnp.where((idx < nb_past)[..., None, None, None], past, new)
        return sel[..., 0, :], sel[..., 1, :]

    win_buf = cache_win.shape[1]
    kw_all = jnp.concatenate([cache_win[:, :, 0], kw], 1)
    vw_all = jnp.concatenate([cache_win[:, :, 1], vw], 1)
    kwpos = past_len - win_buf + jnp.arange(win_buf + T, dtype=jnp.int32)
    o = _nsa_block(q, pos, kcb, vcb, fetch, kw_all, vw_all, kwpos, gates)
    new_win = jnp.concatenate([cache_win, jnp.stack([kw, vw], 2)], 1)[:, T:]
    return o, jnp.stack([kc, vc], 2), new_slc, new_win


def _routed_experts(xt, eidx, gate, w1, w3, w2):
    T, D = xt.shape
    tk = T * TOP_K
    flat_e = eidx.reshape(-1)
    order = jnp.argsort(flat_e)
    se = flat_e[order]
    stok = (order // TOP_K).astype(jnp.int32)
    sgate = gate.reshape(-1)[order]
    counts = jnp.bincount(flat_e, length=N_EXPERTS)
    padded = (counts + MOE_BLOCK - 1) // MOE_BLOCK * MOE_BLOCK
    start = jnp.cumsum(counts) - counts
    pend = jnp.cumsum(padded)
    dest = (pend - padded)[se] + jnp.arange(tk) - start[se]
    n_blk = -(-(tk + N_EXPERTS * (MOE_BLOCK - 1)) // MOE_BLOCK)
    n_rows = n_blk * MOE_BLOCK
    row_tok = jnp.full((n_rows,), T, jnp.int32).at[dest].set(stok)
    row_gate = jnp.zeros((n_rows,), jnp.float32).at[dest].set(sgate)
    blk_e = jnp.minimum(jnp.searchsorted(pend, jnp.arange(n_blk) * MOE_BLOCK, side='right'), N_EXPERTS - 1)
    rows = jnp.concatenate([xt, jnp.zeros((1, D), xt.dtype)], 0)[row_tok].reshape(n_blk, MOE_BLOCK, D)

    def expert_block(args):
        xb, e = args
        return (jax.nn.silu(xb @ w1[e]) * (xb @ w3[e])) @ w2[e]

    out = lax.map(expert_block, (rows, blk_e)).reshape(n_rows, D)
    return jax.ops.segment_sum(out * row_gate[:, None].astype(out.dtype), row_tok, num_segments=T + 1)[:T]


def _moe(u, w):
    shape = u.shape
    xt = u.reshape(-1, shape[-1])
    T = xt.shape[0]
    scores = jax.nn.sigmoid((xt @ w['w_router']).astype(jnp.float32))
    biased = scores + w['b_router'].astype(jnp.float32)
    per_group = N_EXPERTS // N_GROUPS
    gscore = lax.top_k(biased.reshape(T, N_GROUPS, per_group), 2)[0].sum(-1)
    _, gidx = lax.top_k(gscore, TOPK_GROUPS)
    gmask = jnp.any(gidx[:, :, None] == jnp.arange(N_GROUPS)[None, None, :], axis=1)
    emask = jnp.repeat(gmask, per_group, axis=1)
    _, eidx = lax.top_k(jnp.where(emask, biased, -jnp.inf), TOP_K)
    gate = jnp.take_along_axis(scores, eidx, axis=1)
    gate = gate / gate.sum(-1, keepdims=True) * ROUTED_SCALE
    routed = _routed_experts(xt, eidx, gate, w['w_exp1'], w['w_exp3'], w['w_exp2'])
    shared = (jax.nn.silu(xt @ w['w_sh1']) * (xt @ w['w_sh3'])) @ w['w_sh2']
    return (routed + shared).reshape(shape)


def _layer(x, c, mixer, w):
    mod = (jax.nn.silu(c) @ w['w_ada'] + w['b_ada'])[:, None, :]
    sh1, sc1, g1, sh2, sc2, g2 = jnp.split(mod, 6, axis=-1)
    u = _ln(x) * (1.0 + sc1) + sh1
    parts = dict(zip(IN_NAMES, jnp.split(u @ w['w_in'], IN_SPLITS, axis=-1)))
    o_gla, o_nsa, state = mixer(parts)
    z = (jax.nn.sigmoid(parts['mg_gla']) * (o_gla @ w['w_br_gla'])
         + jax.nn.sigmoid(parts['mg_nsa']) * (o_nsa @ w['w_br_nsa']))
    x = _ln(DN_ALPHA * x + g1 * (z @ w['w_out']), w['ln1_g'], w['ln1_b'])
    u = _ln(x) * (1.0 + sc2) + sh2
    x = _ln(DN_ALPHA * x + g2 * _moe(u, w), w['ln2_g'], w['ln2_b'])
    return x, state


def _mix_prompt(parts, w):
    B = parts['gla_q'].shape[0]
    s0 = jnp.zeros((B, GLA_HEADS, GLA_DK, GLA_DV), parts['gla_q'].dtype)
    o_gla, s_gla = _gla_mixer(parts, s0, w['w_gla_a2'], w['b_gla_a2'], w['gla_norm_g'])
    o_nsa, cmp_rows, slc_rows, win = _nsa_prompt(parts, w['cmp_pe_k'], w['cmp_pe_v'], w['cmp_w_k'], w['cmp_w_v'])
    return o_gla, o_nsa, (cmp_rows, slc_rows, win, s_gla)


def _mix_sample(parts, w, cache_cmp, cache_slc, cache_win, state_gla, page_table):
    o_gla, s_gla = _gla_mixer(parts, state_gla, w['w_gla_a2'], w['b_gla_a2'], w['gla_norm_g'])
    o_nsa, cmp_rows, slc_rows, win = _nsa_sample(parts, cache_cmp, cache_slc, cache_win, page_table,
                                                 w['cmp_pe_k'], w['cmp_pe_v'], w['cmp_w_k'], w['cmp_w_v'])
    return o_gla, o_nsa, (cmp_rows, slc_rows, win, s_gla)


def setup_inputs(seed: int = 0) -> dict:
    key = jax.random.key(seed)
    keys = iter(jax.random.split(key, 40))
    nrm = lambda shape, scale: jax.random.normal(next(keys), shape, jnp.float32) * scale
    G, HD, D = NSA_KV_HEADS, NSA_HEAD_DIM, D_MODEL
    n_pages = PAST_LEN // PAGE_SIZE
    n_pool = (DEC_BATCH * n_pages * 5 + 3) // 4
    win_buf = min(WINDOW, PAST_LEN)
    page_table = jax.random.permutation(next(keys), n_pool)[:DEC_BATCH * n_pages]
    page_table = page_table.reshape(DEC_BATCH, n_pages).astype(jnp.int32)
    return {
        'x_prompt': nrm((BATCH, SEQ, D), 1.0),
        'x_sample': nrm((DEC_BATCH, DEC_SEQ, D), 1.0),
        'c_prompt': nrm((BATCH, D), 1.0),
        'c_sample': nrm((DEC_BATCH, D), 1.0),
        'cache_cmp': nrm((DEPTH, n_pool, PAGE_SIZE, 2, G, HD), 1.0),
        'cache_slc': nrm((DEPTH, n_pool, PAGE_SIZE, 2, G, HD), 1.0),
        'cache_win': nrm((DEPTH, DEC_BATCH, win_buf, 2, G, HD), 1.0),
        'state_gla': nrm((DEPTH, DEC_BATCH, GLA_HEADS, GLA_DK, GLA_DV), 0.5),
        'page_table': page_table,
        'w_ada': nrm((DEPTH, D, 6 * D), 0.5 * D ** -0.5),
        'b_ada': nrm((DEPTH, 6 * D), 0.01),
        'w_in': nrm((DEPTH, D, N_IN), D ** -0.5),
        'w_gla_a2': nrm((DEPTH, GLA_GATE_RANK, GLA_QK), GLA_GATE_RANK ** -0.5),
        'b_gla_a2': nrm((DEPTH, GLA_QK), 0.01),
        'gla_norm_g': 1.0 + nrm((DEPTH, GLA_DV), 0.01),
        'cmp_pe_k': nrm((DEPTH, CMP_BLOCK, HD), 0.02),
        'cmp_pe_v': nrm((DEPTH, CMP_BLOCK, HD), 0.02),
        'cmp_w_k': nrm((DEPTH, CMP_BLOCK), 0.1),
        'cmp_w_v': nrm((DEPTH, CMP_BLOCK), 0.1),
        'w_br_gla': nrm((DEPTH, GLA_V, D), GLA_V ** -0.5 * DN_BETA),
        'w_br_nsa': nrm((DEPTH, NSA_Q, D), NSA_Q ** -0.5 * DN_BETA),
        'w_out': nrm((DEPTH, D, D), D ** -0.5 * DN_BETA),
        'ln1_g': 1.0 + nrm((DEPTH, D), 0.01),
        'ln1_b': nrm((DEPTH, D), 0.01),
        'w_router': nrm((DEPTH, D, N_EXPERTS), D ** -0.5),
        'b_router': nrm((DEPTH, N_EXPERTS), 0.01),
        'w_exp1': nrm((DEPTH, N_EXPERTS, D, EXPERT_HIDDEN), D ** -0.5),
        'w_exp3': nrm((DEPTH, N_EXPERTS, D, EXPERT_HIDDEN), D ** -0.5),
        'w_exp2': nrm((DEPTH, N_EXPERTS, EXPERT_HIDDEN, D), EXPERT_HIDDEN ** -0.5 * DN_BETA),
        'w_sh1': nrm((DEPTH, D, SHARED_HIDDEN), D ** -0.5),
        'w_sh3': nrm((DEPTH, D, SHARED_HIDDEN), D ** -0.5),
        'w_sh2': nrm((DEPTH, SHARED_HIDDEN, D), SHARED_HIDDEN ** -0.5 * DN_BETA),
        'ln2_g': 1.0 + nrm((DEPTH, D), 0.01),
        'ln2_b': nrm((DEPTH, D), 0.01),
    }


def reference(x_prompt, x_sample, c_prompt, c_sample, cache_cmp, cache_slc, cache_win, state_gla, page_table,
              w_ada, b_ada, w_in, w_gla_a2, b_gla_a2, gla_norm_g, cmp_pe_k, cmp_pe_v, cmp_w_k, cmp_w_v,
              w_br_gla, w_br_nsa, w_out, ln1_g, ln1_b, w_router, b_router, w_exp1, w_exp3, w_exp2,
              w_sh1, w_sh3, w_sh2, ln2_g, ln2_b):
    xp, xs = x_prompt, x_sample
    st_p, st_s = [], []
    for l in range(DEPTH):
        w = dict(w_ada=w_ada[l], b_ada=b_ada[l], w_in=w_in[l], w_gla_a2=w_gla_a2[l], b_gla_a2=b_gla_a2[l],
                 gla_norm_g=gla_norm_g[l], cmp_pe_k=cmp_pe_k[l], cmp_pe_v=cmp_pe_v[l], cmp_w_k=cmp_w_k[l],
                 cmp_w_v=cmp_w_v[l], w_br_gla=w_br_gla[l], w_br_nsa=w_br_nsa[l], w_out=w_out[l],
                 ln1_g=ln1_g[l], ln1_b=ln1_b[l], w_router=w_router[l], b_router=b_router[l],
                 w_exp1=w_exp1[l], w_exp3=w_exp3[l], w_exp2=w_exp2[l], w_sh1=w_sh1[l], w_sh3=w_sh3[l],
                 w_sh2=w_sh2[l], ln2_g=ln2_g[l], ln2_b=ln2_b[l])
        xp, sp = _layer(xp, c_prompt, functools.partial(_mix_prompt, w=w), w)
        xs, ss = _layer(xs, c_sample, functools.partial(
            _mix_sample, w=w, cache_cmp=cache_cmp[l], cache_slc=cache_slc[l], cache_win=cache_win[l],
            state_gla=state_gla[l], page_table=page_table), w)
        st_p.append(sp)
        st_s.append(ss)
    cmp_p, slc_p, win_p, gla_p = [jnp.stack(t, 0) for t in zip(*st_p)]
    cmp_s, slc_s, win_s, gla_s = [jnp.stack(t, 0) for t in zip(*st_s)]
    return (xp, xs, cmp_p, cmp_s, slc_p, slc_s, win_p, win_s, gla_p, gla_s)
```

```python
import functools
import math

import jax
import jax.numpy as jnp
import numpy as np
from jax import lax
from jax.experimental import pallas as pl
from jax.experimental.pallas import tpu as pltpu

F32, BF16, I32, U32 = jnp.float32, jnp.bfloat16, jnp.int32, jnp.uint32
HIGHEST = lax.Precision.HIGHEST

GLA_HEADS, GLA_DK, GLA_DV, GLA_RANK, GLA_GATE_NORM = 8, 128, 256, 16, 16.0
NSA_HEADS, NSA_G, NSA_R, HD = 16, 4, 4, 128
CMP_BLOCK, SEL_BLOCK, N_SEL, WINDOW = 32, 64, 16, 512
ROPE_THETA = 10000.0
TOP_K, N_GROUPS, TOPK_GROUPS, ROUTED_SCALE = 8, 8, 4, 2.5
LN_EPS = 1e-5
NEG = -1e30

LANE = 128
VMEM_LIMIT = 56 * 1024 * 1024
ROW_TILE = 128
MOE_ROWS = 256
GLA_SUB = 32


def _cparams(sem):
    return pltpu.CompilerParams(dimension_semantics=sem, vmem_limit_bytes=VMEM_LIMIT)


def _divisor_tile(n, target, mult):
    best = None
    for d in range(mult, min(n, target) + 1, mult):
        if n % d == 0:
            best = d
    assert best is not None, (n, target, mult)
    return best


def _silu(x):
    return x * jax.nn.sigmoid(x)


def _ln_rows(x):
    mu = jnp.mean(x, axis=-1, keepdims=True)
    xc = x - mu
    var = jnp.mean(xc * xc, axis=-1, keepdims=True)
    return xc * lax.rsqrt(var + LN_EPS)


def _dot(a, b):
    return jnp.dot(a, b, preferred_element_type=F32)


def _dot_nt(a, b, precision=None):
    return lax.dot_general(a, b, (((1,), (1,)), ((), ())), precision=precision,
                           preferred_element_type=F32)


def _dot_tn(a, b, precision=None):
    return lax.dot_general(a, b, (((0,), (0,)), ((), ())), precision=precision,
                           preferred_element_type=F32)


def _col_plan(d_model):
    gqk, gv = GLA_HEADS * GLA_DK, GLA_HEADS * GLA_DV
    nq, nkv = NSA_HEADS * HD, NSA_G * HD
    ref = (('gla_q', gqk), ('gla_k', gqk), ('gla_v', gv), ('gla_r', gv), ('gla_a', GLA_RANK),
           ('nsa_q', nq), ('nsa_kc', nkv), ('nsa_vc', nkv), ('nsa_ks', nkv), ('nsa_vs', nkv),
           ('nsa_kw', nkv), ('nsa_vw', nkv), ('nsa_g', NSA_HEADS * 3), ('mg_gla', d_model),
           ('mg_nsa', d_model))
    src, o = {}, 0
    for n, w in ref:
        src[n] = (o, w)
        o += w
    order = ('gla_q', 'gla_k', 'gla_v', 'gla_r', 'nsa_q', 'nsa_kc', 'nsa_vc', 'nsa_ks', 'nsa_vs',
             'nsa_kw', 'nsa_vw', 'mg_gla', 'mg_nsa', 'gla_a', 'nsa_g')
    dst, o = {}, 0
    for n in order:
        dst[n] = o
        o += src[n][1]
    return src, order, dst, o, o


def _ada_kernel(c_ref, w_ref, b_ref, o_ref):
    a = _silu(c_ref[...]).astype(BF16)
    o_ref[...] = _dot(a, w_ref[...].astype(BF16)) + b_ref[...]


def _ada(c_all, w_ada, b_ada):
    rc, d = c_all.shape
    n = w_ada.shape[1]
    tn = 512
    return pl.pallas_call(
        _ada_kernel,
        out_shape=jax.ShapeDtypeStruct((rc, n), F32),
        grid=(n // tn,),
        in_specs=[pl.BlockSpec((rc, d), lambda j: (0, 0)),
                  pl.BlockSpec((d, tn), lambda j: (0, j)),
                  pl.BlockSpec((1, tn), lambda j: (0, j))],
        out_specs=pl.BlockSpec((rc, tn), lambda j: (0, j)),
        compiler_params=_cparams(("arbitrary",)),
        name="ada_mod",
    )(c_all, w_ada, b_ada.reshape(1, n))


def _mod_spec(k, d, tile_grp):
    return pl.BlockSpec((None, ROW_TILE, d), lambda i: (tile_grp(i), 0, k))


def _ln_mod_kernel(x_ref, sh_ref, sc_ref, u_ref):
    y = _ln_rows(x_ref[...])
    u_ref[...] = (y * (1.0 + sc_ref[...]) + sh_ref[...]).astype(BF16)


def _ln_mod(x_all, mod_tiles, tile_grp):
    m, d = x_all.shape
    return pl.pallas_call(
        _ln_mod_kernel,
        out_shape=jax.ShapeDtypeStruct((m, d), BF16),
        grid=(m // ROW_TILE,),
        in_specs=[pl.BlockSpec((ROW_TILE, d), lambda i: (i, 0)),
                  _mod_spec(0, d, tile_grp), _mod_spec(1, d, tile_grp)],
        out_specs=pl.BlockSpec((ROW_TILE, d), lambda i: (i, 0)),
        compiler_params=_cparams(("arbitrary",)),
        name="ln_mod1",
    )(x_all, mod_tiles, mod_tiles)


def _pack_pairs(u):
    bits = pltpu.bitcast(u, U32)
    out = []
    for s in range(u.shape[1] // 256):
        lo = bits[:, s * 256:s * 256 + 128]
        hi = bits[:, s * 256 + 128:s * 256 + 256]
        out.append((lo >> 16) | (hi & jnp.uint32(0xFFFF0000)))
    return out


def _mid_kernel(x_ref, y_ref, g1_ref, sh_ref, sc_ref, lg_ref, lb_ref, x1_ref, u_ref, up_ref, *, alpha):
    v = alpha * x_ref[...] + g1_ref[...] * y_ref[...]
    x1 = _ln_rows(v) * lg_ref[...] + lb_ref[...]
    x1_ref[...] = x1
    u = _ln_rows(x1) * (1.0 + sc_ref[...]) + sh_ref[...]
    ub = u.astype(BF16)
    u_ref[...] = ub
    words = _pack_pairs(ub.astype(F32))
    s2 = len(words)
    for s, w in enumerate(words):
        up_ref[pl.ds(s, ROW_TILE, stride=s2), :] = w


def _mid(x_all, y, mod_tiles, tile_grp, ln_g, ln_b, alpha):
    m, d = x_all.shape
    s2 = d // 256
    row = pl.BlockSpec((ROW_TILE, d), lambda i: (i, 0))
    vec = pl.BlockSpec((1, d), lambda i: (0, 0))
    return pl.pallas_call(
        functools.partial(_mid_kernel, alpha=alpha),
        out_shape=(jax.ShapeDtypeStruct((m, d), F32), jax.ShapeDtypeStruct((m, d), BF16),
                   jax.ShapeDtypeStruct((m * s2, LANE), U32)),
        grid=(m // ROW_TILE,),
        in_specs=[row, row, _mod_spec(2, d, tile_grp), _mod_spec(3, d, tile_grp),
                  _mod_spec(4, d, tile_grp), vec, vec],
        out_specs=(row, row, pl.BlockSpec((ROW_TILE * s2, LANE), lambda i: (i, 0))),
        compiler_params=_cparams(("arbitrary",)),
        name="ln1_mod2",
    )(x_all, y, mod_tiles, mod_tiles, mod_tiles, ln_g.reshape(1, d), ln_b.reshape(1, d))


def _cast_weights(i, pairs):
    @pl.when(i == 0)
    def _():
        for w_ref, wb_ref in pairs:
            wb_ref[...] = w_ref[...].astype(BF16)


def _inproj_kernel(u_ref, w_ref, cos_ref, sin_ref, o_ref, wb_ref, *, tn, rope_tiles):
    j, i = pl.program_id(0), pl.program_id(1)
    _cast_weights(i, ((w_ref, wb_ref),))
    acc = _dot(u_ref[...], wb_ref[...])
    is_rope = functools.reduce(jnp.logical_or, [(j >= lo) & (j < hi) for lo, hi in rope_tiles])

    @pl.when(is_rope)
    def _():
        cos, sin = cos_ref[...], sin_ref[...]
        for h in range(tn // HD):
            a = acc[:, h * HD:(h + 1) * HD]
            o_ref[:, h * HD:(h + 1) * HD] = a * cos + pltpu.roll(a, HD // 2, axis=1) * sin

    @pl.when(jnp.logical_not(is_rope))
    def _():
        o_ref[...] = acc


def _inproj(u, w_p, cos_t, sin_t, rope_cols, tm, tn):
    m, d = u.shape
    n = w_p.shape[1]
    rope_tiles = tuple((lo // tn, hi // tn) for lo, hi in rope_cols)
    return pl.pallas_call(
        functools.partial(_inproj_kernel, tn=tn, rope_tiles=rope_tiles),
        out_shape=jax.ShapeDtypeStruct((m, n), F32),
        grid=(n // tn, m // tm),
        in_specs=[pl.BlockSpec((tm, d), lambda j, i: (i, 0)),
                  pl.BlockSpec((d, tn), lambda j, i: (0, j)),
                  pl.BlockSpec((tm, HD), lambda j, i: (i, 0)),
                  pl.BlockSpec((tm, HD), lambda j, i: (i, 0))],
        out_specs=pl.BlockSpec((tm, tn), lambda j, i: (i, j)),
        scratch_shapes=[pltpu.VMEM((d, tn), BF16)],
        compiler_params=_cparams(("arbitrary", "arbitrary")),
        name="in_proj",
    )(u, w_p, cos_t, sin_t)


def _merge_kernel(a1_ref, a2_ref, w1_ref, w2_ref, g1_ref, g2_ref, o_ref, wb1_ref, wb2_ref):
    _cast_weights(pl.program_id(1), ((w1_ref, wb1_ref), (w2_ref, wb2_ref)))
    z = (jax.nn.sigmoid(g1_ref[...]) * _dot(a1_ref[...], wb1_ref[...])
         + jax.nn.sigmoid(g2_ref[...]) * _dot(a2_ref[...], wb2_ref[...]))
    o_ref[...] = z.astype(BF16)


def _merge(og, on, w1, w2, parts, off1, off2, tm, tn):
    m, k = og.shape
    n = w1.shape[1]
    return pl.pallas_call(
        _merge_kernel,
        out_shape=jax.ShapeDtypeStruct((m, n), BF16),
        grid=(n // tn, m // tm),
        in_specs=[pl.BlockSpec((tm, k), lambda j, i: (i, 0)),
                  pl.BlockSpec((tm, k), lambda j, i: (i, 0)),
                  pl.BlockSpec((k, tn), lambda j, i: (0, j)),
                  pl.BlockSpec((k, tn), lambda j, i: (0, j)),
                  pl.BlockSpec((tm, tn), lambda j, i: (i, off1 // tn + j)),
                  pl.BlockSpec((tm, tn), lambda j, i: (i, off2 // tn + j))],
        out_specs=pl.BlockSpec((tm, tn), lambda j, i: (i, j)),
        scratch_shapes=[pltpu.VMEM((k, tn), BF16), pltpu.VMEM((k, tn), BF16)],
        compiler_params=_cparams(("arbitrary", "arbitrary")),
        name="branch_merge",
    )(og, on, w1, w2, parts, parts)


def _plain_kernel(a_ref, w_ref, o_ref, wb_ref):
    _cast_weights(pl.program_id(1), ((w_ref, wb_ref),))
    o_ref[...] = _dot(a_ref[...], wb_ref[...])


def _plain(a, w, tm, tn, name):
    m, k = a.shape
    n = w.shape[1]
    return pl.pallas_call(
        _plain_kernel,
        out_shape=jax.ShapeDtypeStruct((m, n), F32),
        grid=(n // tn, m // tm),
        in_specs=[pl.BlockSpec((tm, k), lambda j, i: (i, 0)),
                  pl.BlockSpec((k, tn), lambda j, i: (0, j))],
        out_specs=pl.BlockSpec((tm, tn), lambda j, i: (i, j)),
        scratch_shapes=[pltpu.VMEM((k, tn), BF16)],
        compiler_params=_cparams(("arbitrary", "arbitrary")),
        name=name,
    )(a, w)


def _swiglu_kernel(a_ref, w1_ref, w3_ref, o_ref, wb1_ref, wb3_ref):
    _cast_weights(pl.program_id(1), ((w1_ref, wb1_ref), (w3_ref, wb3_ref)))
    a = a_ref[...]
    o_ref[...] = (_silu(_dot(a, wb1_ref[...])) * _dot(a, wb3_ref[...])).astype(BF16)


def _swiglu(a, w1, w3, tm, tn):
    m, k = a.shape
    n = w1.shape[1]
    return pl.pallas_call(
        _swiglu_kernel,
        out_shape=jax.ShapeDtypeStruct((m, n), BF16),
        grid=(n // tn, m // tm),
        in_specs=[pl.BlockSpec((tm, k), lambda j, i: (i, 0)),
                  pl.BlockSpec((k, tn), lambda j, i: (0, j)),
                  pl.BlockSpec((k, tn), lambda j, i: (0, j))],
        out_specs=pl.BlockSpec((tm, tn), lambda j, i: (i, j)),
        scratch_shapes=[pltpu.VMEM((k, tn), BF16), pltpu.VMEM((k, tn), BF16)],
        compiler_params=_cparams(("arbitrary", "arbitrary")),
        name="shared_swiglu",
    )(a, w1, w3)


def _gla_kernel(q_ref, k_ref, v_ref, r_ref, a_ref, wa_ref, ba_ref, ng_ref, s0_ref,
                og_ref, sf_ref, st_ref, *, tb, chunk, s_valid):
    i = pl.program_id(2)

    @pl.when(i == 0)
    def _():
        st_ref[...] = s0_ref[...]

    x = jnp.dot(a_ref[:, :GLA_RANK], wa_ref[...], precision=HIGHEST,
                preferred_element_type=F32) + ba_ref[...]
    glog = (jnp.minimum(x, 0.0) - jnp.log1p(jnp.exp(-jnp.abs(x)))) * (1.0 / GLA_GATE_NORM)
    row = i * tb + lax.broadcasted_iota(I32, (tb, 1), 0)
    glog = jnp.where(row < s_valid, glog, 0.0)

    sub = min(GLA_SUB, chunk)
    rr = lax.broadcasted_iota(I32, (chunk, chunk), 0)
    cc = lax.broadcasted_iota(I32, (chunk, chunk), 1)
    tri = (rr >= cc).astype(F32)
    ones = jnp.ones((chunk, GLA_DK), F32)
    ng = ng_ref[...]

    for c in range(tb // chunk):
        lo = c * chunk
        g = glog[lo:lo + chunk]
        bc = jnp.dot(tri, g, precision=HIGHEST, preferred_element_type=F32)
        bl = bc[chunk - 1:chunk]
        q = q_ref[lo:lo + chunk, :] * (GLA_DK ** -0.5)
        k = k_ref[lo:lo + chunk, :]
        vb = v_ref[lo:lo + chunk, :].astype(BF16)
        qg = (q * jnp.exp(bc)).astype(BF16)
        kd = (k * jnp.exp(bl - bc)).astype(BF16)
        state = st_ref[...]
        o_inter = _dot(qg, state.astype(BF16))
        outs = []
        for sb in range(chunk // sub):
            r0, r1 = sb * sub, (sb + 1) * sub
            base = bc[r0 - 1:r0] if sb > 0 else jnp.zeros((1, GLA_DK), F32)
            qs = (q[r0:r1] * jnp.exp(bc[r0:r1] - base)).astype(BF16)
            ks = (k[:r1] * jnp.exp(base - bc[:r1])).astype(BF16)
            att = _dot_nt(qs, ks)
            causal = (r0 + lax.broadcasted_iota(I32, (sub, r1), 0)) >= lax.broadcasted_iota(I32, (sub, r1), 1)
            att = jnp.where(causal, att, 0.0)
            outs.append(_dot(att.astype(BF16), vb[:r1]))
        o = jnp.concatenate(outs, axis=0) + o_inter
        dec = jnp.exp(_dot_tn(g, ones, precision=HIGHEST))
        dec = jnp.concatenate([dec] * (GLA_DV // GLA_DK), axis=1)
        st_ref[...] = dec * state + _dot_tn(kd, vb)
        o = o * lax.rsqrt(jnp.mean(o * o, axis=-1, keepdims=True) + LN_EPS) * ng
        og_ref[lo:lo + chunk, :] = (o * _silu(r_ref[lo:lo + chunk, :])).astype(BF16)

    @pl.when(i == pl.num_programs(2) - 1)
    def _():
        sf_ref[...] = st_ref[...]


def _gla(src, nseq, s_pad, s_valid, tb, chunk, dst, s0, w_a2, b_a2, norm_g):
    nb = s_pad // tb
    rows = nseq * s_pad
    qo, ko = dst['gla_q'] // GLA_DK, dst['gla_k'] // GLA_DK
    vo, ro = dst['gla_v'] // GLA_DV, dst['gla_r'] // GLA_DV
    ao = dst['gla_a'] // LANE
    rowblk = lambda b, h, i: b * nb + i
    st_spec = pl.BlockSpec((None, None, GLA_DK, GLA_DV), lambda b, h, i: (b, h, 0, 0))
    return pl.pallas_call(
        functools.partial(_gla_kernel, tb=tb, chunk=chunk, s_valid=s_valid),
        out_shape=(jax.ShapeDtypeStruct((rows, GLA_HEADS * GLA_DV), BF16),
                   jax.ShapeDtypeStruct((nseq, GLA_HEADS, GLA_DK, GLA_DV), F32)),
        grid=(nseq, GLA_HEADS, nb),
        in_specs=[pl.BlockSpec((tb, GLA_DK), lambda b, h, i: (rowblk(b, h, i), qo + h)),
                  pl.BlockSpec((tb, GLA_DK), lambda b, h, i: (rowblk(b, h, i), ko + h)),
                  pl.BlockSpec((tb, GLA_DV), lambda b, h, i: (rowblk(b, h, i), vo + h)),
                  pl.BlockSpec((tb, GLA_DV), lambda b, h, i: (rowblk(b, h, i), ro + h)),
                  pl.BlockSpec((tb, LANE), lambda b, h, i: (rowblk(b, h, i), ao)),
                  pl.BlockSpec((GLA_RANK, GLA_DK), lambda b, h, i: (0, h)),
                  pl.BlockSpec((1, GLA_DK), lambda b, h, i: (0, h)),
                  pl.BlockSpec((1, GLA_DV), lambda b, h, i: (0, 0)),
                  st_spec],
        out_specs=(pl.BlockSpec((tb, GLA_DV), lambda b, h, i: (rowblk(b, h, i), h)), st_spec),
        scratch_shapes=[pltpu.VMEM((GLA_DK, GLA_DV), F32)],
        compiler_params=_cparams(("arbitrary", "arbitrary", "arbitrary")),
        name="gla_scan",
    )(src, src, src, src, src, w_a2, b_a2.reshape(1, -1), norm_g.reshape(1, -1), s0)


def _rope_heads(x, cos, sin):
    return jnp.concatenate(
        [x[:, h * HD:(h + 1) * HD] * cos + pltpu.roll(x[:, h * HD:(h + 1) * HD], HD // 2, axis=1) * sin
         for h in range(x.shape[1] // HD)], axis=1)


def _compress_rows(x, wk_ref, wv_ref, pek_ref, pev_ref, cos, sin):
    rows, width = x.shape
    nb, half = rows // CMP_BLOCK, width // 2

    def softmax_col(w_ref):
        w = w_ref[...]
        e = jnp.exp(w - jnp.max(w, axis=0, keepdims=True))
        return e / jnp.sum(e, axis=0, keepdims=True)

    wk, wv = softmax_col(wk_ref), softmax_col(wv_ref)
    w2 = jnp.concatenate([jnp.broadcast_to(wk, (CMP_BLOCK, half)),
                          jnp.broadcast_to(wv, (CMP_BLOCK, half))], axis=1)
    y = jnp.sum(x.reshape(nb, CMP_BLOCK, width) * w2[None], axis=1)
    pk = jnp.sum(pek_ref[...] * wk, axis=0, keepdims=True)
    pv = jnp.sum(pev_ref[...] * wv, axis=0, keepdims=True)
    kb = y[:, :half] + jnp.concatenate([pk] * NSA_G, axis=1)
    vb = y[:, half:] + jnp.concatenate([pv] * NSA_G, axis=1)
    return jnp.concatenate([_rope_heads(kb, cos, sin), vb], axis=1)


def _compress_prompt_kernel(x_ref, wk_ref, wv_ref, pek_ref, pev_ref, cos_ref, sin_ref, o_ref):
    o_ref[...] = _compress_rows(x_ref[...], wk_ref, wv_ref, pek_ref, pev_ref, cos_ref[...], sin_ref[...])


def _small_specs(nargs):
    col = pl.BlockSpec((CMP_BLOCK, 1), lambda *a: (0, 0))
    pe = pl.BlockSpec((CMP_BLOCK, HD), lambda *a: (0, 0))
    return [col, col, pe, pe]


def _compress_prompt(parts, rows, dst, w_k, w_v, pe_k, pe_v, cos_c, sin_c):
    width = 2 * NSA_G * HD
    tr = _divisor_tile(rows, 1024, 256)
    nbt = tr // CMP_BLOCK
    return pl.pallas_call(
        _compress_prompt_kernel,
        out_shape=jax.ShapeDtypeStruct((rows // CMP_BLOCK, width), F32),
        grid=(rows // tr,),
        in_specs=[pl.BlockSpec((tr, width), lambda i: (i, dst['nsa_kc'] // width))] + _small_specs(1)
        + [pl.BlockSpec((nbt, HD), lambda i: (i, 0)), pl.BlockSpec((nbt, HD), lambda i: (i, 0))],
        out_specs=pl.BlockSpec((nbt, width), lambda i: (i, 0)),
        compiler_params=_cparams(("arbitrary",)),
        name="nsa_compress_prompt",
    )(parts, w_k.reshape(-1, 1), w_v.reshape(-1, 1), pe_k, pe_v, cos_c, sin_c)


def _compress_pages_kernel(pt_ref, *refs, pg, rows_per_page):
    page_refs = refs[:pg]
    wk_ref, wv_ref, pek_ref, pev_ref, cos_ref, sin_ref, o_ref = refs[pg:]
    x = jnp.concatenate([r[...] for r in page_refs], axis=0)
    o_ref[...] = _compress_rows(x, wk_ref, wv_ref, pek_ref, pev_ref, cos_ref[...], sin_ref[...])


def _compress_pages(cache, page_table, w_k, w_v, pe_k, pe_v, cos_c, sin_c):
    n_pool, page, width = cache.shape
    nseq, n_pages = page_table.shape
    pg = _divisor_tile(n_pages, 8, 1)
    bpp = page // CMP_BLOCK
    page_specs = [pl.BlockSpec((None, page, width),
                               (lambda b, i, pt, p=p: (pt[b * n_pages + i * pg + p], 0, 0)))
                  for p in range(pg)]
    small = [pl.BlockSpec((CMP_BLOCK, 1), lambda b, i, pt: (0, 0))] * 2 \
        + [pl.BlockSpec((CMP_BLOCK, HD), lambda b, i, pt: (0, 0))] * 2
    tab = [pl.BlockSpec((pg * bpp, HD), lambda b, i, pt: (i, 0))] * 2
    return pl.pallas_call(
        functools.partial(_compress_pages_kernel, pg=pg, rows_per_page=page),
        out_shape=jax.ShapeDtypeStruct((nseq, n_pages * bpp, width), F32),
        grid_spec=pltpu.PrefetchScalarGridSpec(
            num_scalar_prefetch=1,
            grid=(nseq, n_pages // pg),
            in_specs=page_specs + small + tab,
            out_specs=pl.BlockSpec((None, pg * bpp, width), lambda b, i, pt: (b, i, 0))),
        compiler_params=_cparams(("arbitrary", "arbitrary")),
        name="nsa_compress_pages",
    )(page_table.reshape(-1), *([cache] * pg), w_k.reshape(-1, 1), w_v.reshape(-1, 1), pe_k, pe_v,
      cos_c, sin_c)


def _select_blocks(psum, cur, extra_forced):
    rows, nb = psum.shape
    lane = lax.broadcasted_iota(I32, (rows, nb), 1)
    jl = lane >> 1
    pair = jnp.where((lane & 1) == 0, psum + pltpu.roll(psum, nb - 1, axis=1),
                     psum + pltpu.roll(psum, 1, axis=1))
    forced = (jl == 0) | (jl == cur) | (jl == cur - 1)
    imp = jnp.where(forced, jnp.inf, jnp.where(jl <= cur, pair, -jnp.inf))
    cnt = jnp.where(imp < jnp.inf, float(extra_forced), 0.0)
    for j in range(nb // 2):
        vj = imp[:, 2 * j:2 * j + 1]
        beats = (vj > imp) | ((vj == imp) & (jl > j))
        cnt = cnt + jnp.where(beats, 1.0, 0.0)
    return cnt < float(N_SEL)


def _flash_update(carry, s, mask, vb):
    m, l, acc = carry
    s = jnp.where(mask, s, NEG)
    m_new = jnp.maximum(m, jnp.max(s, axis=-1, keepdims=True))
    p = jnp.where(mask, jnp.exp(s - m_new), 0.0)
    alpha = jnp.exp(m - m_new)
    l = alpha * l + jnp.sum(p, axis=-1, keepdims=True)
    acc = alpha * acc + _dot(p.astype(BF16), vb)
    return m_new, l, acc


def _flash_init(rows):
    return (jnp.full((rows, 1), NEG, F32), jnp.zeros((rows, 1), F32), jnp.zeros((rows, HD), F32))


def _compressed_branch(qb, kcb, vcb, qpos):
    s = _dot_nt(qb, kcb.astype(BF16))
    n = lax.broadcasted_iota(I32, s.shape, 1)
    vis = ((n + 1) * CMP_BLOCK - 1) <= qpos
    s = jnp.where(vis, s, NEG)
    p = jnp.where(vis, jnp.exp(s - jnp.max(s, axis=-1, keepdims=True)), 0.0)
    den = jnp.sum(p, axis=-1, keepdims=True)
    p = p / jnp.where(den > 0.0, den, 1.0)
    return _dot(p.astype(BF16), vcb.astype(BF16)), p


def _expand_blocks(sel_f, first_block, n_keys):
    nb = sel_f.shape[1]
    n = lax.broadcasted_iota(I32, (nb, n_keys), 0)
    kk = lax.broadcasted_iota(I32, (nb, n_keys), 1)
    e = (n == first_block + (kk >> 5)).astype(BF16)
    return _dot(sel_f.astype(BF16), e) > 0.5


def _nsa_prompt_kernel(q_ref, gt_ref, kcb_ref, vcb_ref, ks_ref, vs_ref, kw_ref, vw_ref, o_ref,
                       *, tq, tk, gate_col):
    g, qi = pl.program_id(1), pl.program_id(2)
    q4 = q_ref[...]
    qb = (jnp.concatenate([q4[:, r * HD:(r + 1) * HD] for r in range(NSA_R)], axis=0)
          * (HD ** -0.5)).astype(BF16)
    rows = NSA_R * tq
    q0 = qi * tq
    qpos1 = q0 + lax.broadcasted_iota(I32, (tq, 1), 0)
    qpos = jnp.concatenate([qpos1] * NSA_R, axis=0)

    o_c, p_c = _compressed_branch(qb, kcb_ref[...], vcb_ref[...], qpos)
    psum = functools.reduce(jnp.add, [p_c[r * tq:(r + 1) * tq] for r in range(NSA_R)])
    sel = _select_blocks(psum, qpos1 >> 6, 0).astype(F32)

    def sel_body(kt, carry):
        start = pl.multiple_of(kt * tk, tk)
        kb = ks_ref[pl.ds(start, tk), :].astype(BF16)
        vb = vs_ref[pl.ds(start, tk), :].astype(BF16)
        s = _dot_nt(qb, kb)
        m1 = _expand_blocks(sel, kt * (tk // CMP_BLOCK), tk)
        kpos = start + lax.broadcasted_iota(I32, (1, tk), 1)
        mask = jnp.concatenate([m1] * NSA_R, axis=0) & (kpos <= qpos)
        return _flash_update(carry, s, mask, vb)

    n_kt = (q0 + tq + tk - 1) // tk
    m_s, l_s, acc_s = lax.fori_loop(0, n_kt, sel_body, _flash_init(rows))
    o_s = acc_s / l_s

    carry = _flash_init(rows)
    for wt in range(WINDOW // tq + 1):
        start = q0 - WINDOW + wt * tq
        cstart = pl.multiple_of(jnp.maximum(start, 0), tq)
        kb = kw_ref[pl.ds(cstart, tq), :].astype(BF16)
        vb = vw_ref[pl.ds(cstart, tq), :].astype(BF16)
        s = _dot_nt(qb, kb)
        kpos = start + lax.broadcasted_iota(I32, (1, tq), 1)
        mask = (kpos <= qpos) & (kpos > qpos - WINDOW) & (kpos >= 0)
        carry = _flash_update(carry, s, mask, vb)
    o_w = carry[2] / carry[1]

    gt = jax.nn.sigmoid(gt_ref[...])
    lane = lax.broadcasted_iota(I32, gt.shape, 1)
    outs = []
    for r in range(NSA_R):
        col = gate_col + (g * NSA_R + r) * 3
        gs = [jnp.sum(jnp.where(lane == col + j, gt, 0.0), axis=-1, keepdims=True) for j in range(3)]
        sl = slice(r * tq, (r + 1) * tq)
        outs.append(gs[0] * o_c[sl] + gs[1] * o_s[sl] + gs[2] * o_w[sl])
    o_ref[...] = jnp.concatenate(outs, axis=1).astype(BF16)


def _nsa_prompt(parts, kvb, nseq, s, dst):
    tq, tk = 256, 512
    nq = s // tq
    nb = s // CMP_BLOCK
    assert s % tk == 0 and WINDOW % tq == 0 and nb % LANE == 0
    gw = NSA_R * HD
    kv = lambda name: pl.BlockSpec((s, HD), lambda b, g, i: (b, dst[name] // HD + g))
    return pl.pallas_call(
        functools.partial(_nsa_prompt_kernel, tq=tq, tk=tk, gate_col=GLA_RANK),
        out_shape=jax.ShapeDtypeStruct((nseq * s, NSA_HEADS * HD), BF16),
        grid=(nseq, NSA_G, nq),
        in_specs=[pl.BlockSpec((tq, gw), lambda b, g, i: (b * nq + i, dst['nsa_q'] // gw + g)),
                  pl.BlockSpec((tq, LANE), lambda b, g, i: (b * nq + i, dst['gla_a'] // LANE)),
                  pl.BlockSpec((nb, HD), lambda b, g, i: (b, g)),
                  pl.BlockSpec((nb, HD), lambda b, g, i: (b, NSA_G + g)),
                  kv('nsa_ks'), kv('nsa_vs'), kv('nsa_kw'), kv('nsa_vw')],
        out_specs=pl.BlockSpec((tq, gw), lambda b, g, i: (b * nq + i, g)),
        compiler_params=_cparams(("arbitrary", "arbitrary", "arbitrary")),
        name="nsa_prompt",
    )(parts, parts, kvb, kvb, parts, parts, parts, parts)


def _nsa_sample_kernel(pt_ref, *refs, pg, n_pages, t_new, t_pad, past_len, win_buf):
    page_refs = refs[:pg]
    q_ref, gt_ref, kvb_ref, win_ref, new_ref, o_ref, sel_ref, m_ref, l_ref, acc_ref, oc_ref = refs[pg:]
    i = pl.program_id(1)
    rows = NSA_R * t_pad
    page = page_refs[0].shape[0]
    kvw = NSA_G * HD
    t1 = lax.broadcasted_iota(I32, (t_pad, 1), 0)
    t_row = jnp.concatenate([t1] * NSA_R, axis=0)
    qpos = past_len + t_row

    def queries(g):
        q = q_ref[...]
        return (jnp.concatenate([q[:, (g * NSA_R + r) * HD:(g * NSA_R + r + 1) * HD]
                                 for r in range(NSA_R)], axis=0) * (HD ** -0.5)).astype(BF16)

    @pl.when(i == 0)
    def _():
        kvb = kvb_ref[...]
        for g in range(NSA_G):
            o_c, p_c = _compressed_branch(queries(g), kvb[:, g * HD:(g + 1) * HD],
                                          kvb[:, kvw + g * HD:kvw + (g + 1) * HD], qpos)
            psum = functools.reduce(jnp.add, [p_c[r * t_pad:(r + 1) * t_pad] for r in range(NSA_R)])
            cur = (past_len + t1) >> 6
            sel_ref[g] = _select_blocks(psum, cur, 1).astype(F32)
            oc_ref[g] = o_c
            m_ref[g] = jnp.full((rows, LANE), NEG, F32)
            l_ref[g] = jnp.zeros((rows, LANE), F32)
            acc_ref[g] = jnp.zeros((rows, HD), F32)

    bpp = page // CMP_BLOCK
    for g in range(NSA_G):
        qb = queries(g)
        carry = (m_ref[g][:, :1], l_ref[g][:, :1], acc_ref[g])
        for p in range(pg):
            x = page_refs[p]
            kb = x[:, g * HD:(g + 1) * HD].astype(BF16)
            vb = x[:, kvw + g * HD:kvw + (g + 1) * HD].astype(BF16)
            m1 = _expand_blocks(sel_ref[g], (i * pg + p) * bpp, page)
            mask = jnp.concatenate([m1] * NSA_R, axis=0)
            carry = _flash_update(carry, _dot_nt(qb, kb), mask, vb)
        m_ref[g] = jnp.broadcast_to(carry[0], (rows, LANE))
        l_ref[g] = jnp.broadcast_to(carry[1], (rows, LANE))
        acc_ref[g] = carry[2]

    @pl.when(i == pl.num_programs(1) - 1)
    def _():
        gt = jax.nn.sigmoid(gt_ref[...])
        new = new_ref[...]
        pad = jnp.zeros((LANE - t_pad, HD), F32)
        kk = lax.broadcasted_iota(I32, (1, LANE), 1)
        new_mask = (kk <= t_row) & (kk < t_new)
        wi = lax.broadcasted_iota(I32, (1, win_buf), 1)
        wpos = past_len - win_buf + wi
        win_mask = (wpos <= qpos) & (wpos > qpos - WINDOW) & (wpos >= 0)
        outs = []
        for g in range(NSA_G):
            qb = queries(g)

            def new_rows(j):
                return jnp.concatenate([new[:, j * kvw + g * HD:j * kvw + (g + 1) * HD], pad],
                                       axis=0).astype(BF16)

            carry = (m_ref[g][:, :1], l_ref[g][:, :1], acc_ref[g])
            carry = _flash_update(carry, _dot_nt(qb, new_rows(0)), new_mask, new_rows(1))
            o_s = carry[2] / carry[1]
            win = win_ref[...]
            carry = _flash_init(rows)
            carry = _flash_update(carry, _dot_nt(qb, win[:, g * HD:(g + 1) * HD].astype(BF16)), win_mask,
                                  win[:, kvw + g * HD:kvw + (g + 1) * HD].astype(BF16))
            carry = _flash_update(carry, _dot_nt(qb, new_rows(2)), new_mask, new_rows(3))
            o_w = carry[2] / carry[1]
            o_c = oc_ref[g]
            for r in range(NSA_R):
                col = GLA_RANK + (g * NSA_R + r) * 3
                sl = slice(r * t_pad, (r + 1) * t_pad)
                outs.append(gt[:, col:col + 1] * o_c[sl] + gt[:, col + 1:col + 2] * o_s[sl]
                            + gt[:, col + 2:col + 3] * o_w[sl])
        o_ref[...] = jnp.concatenate(outs, axis=1)


def _nsa_sample(q_s, gt_s, kvb_s, cache_slc, cache_win, new_s, page_table, t_new):
    nseq, t_pad, _ = q_s.shape
    n_pool, page, width = cache_slc.shape
    n_pages = page_table.shape[1]
    past_len = n_pages * page
    win_buf = cache_win.shape[1]
    nbc = kvb_s.shape[1]
    assert nbc % LANE == 0 and t_new <= SEL_BLOCK and past_len % SEL_BLOCK == 0
    pg = _divisor_tile(n_pages, 8, 1)
    rows = NSA_R * t_pad
    page_specs = [pl.BlockSpec((None, page, width),
                               (lambda b, i, pt, p=p: (pt[b * n_pages + i * pg + p], 0, 0)))
                  for p in range(pg)]
    per_seq = lambda shape: pl.BlockSpec((None,) + shape, lambda b, i, pt: (b, 0, 0))
    return pl.pallas_call(
        functools.partial(_nsa_sample_kernel, pg=pg, n_pages=n_pages, t_new=t_new, t_pad=t_pad,
                          past_len=past_len, win_buf=win_buf),
        out_shape=jax.ShapeDtypeStruct((nseq, t_pad, NSA_HEADS * HD), F32),
        grid_spec=pltpu.PrefetchScalarGridSpec(
            num_scalar_prefetch=1,
            grid=(nseq, n_pages // pg),
            in_specs=page_specs + [per_seq((t_pad, NSA_HEADS * HD)), per_seq((t_pad, LANE)),
                                   per_seq((nbc, width)), per_seq((win_buf, width)),
                                   per_seq((t_pad, 2 * width))],
            out_specs=per_seq((t_pad, NSA_HEADS * HD)),
            scratch_shapes=[pltpu.VMEM((NSA_G, t_pad, nbc), F32),
                            pltpu.VMEM((NSA_G, rows, LANE), F32),
                            pltpu.VMEM((NSA_G, rows, LANE), F32),
                            pltpu.VMEM((NSA_G, rows, HD), F32),
                            pltpu.VMEM((NSA_G, rows, HD), F32)]),
        compiler_params=_cparams(("arbitrary", "arbitrary")),
        name="nsa_sample",
    )(page_table.reshape(-1), *([cache_slc] * pg), q_s, gt_s, kvb_s, cache_win, new_s)


def _router_kernel(u_ref, w_ref, b_ref, gate_ref, pos_ref, cnt_ref, carry_ref, *, n_exp):
    i = pl.program_id(0)
    tt = u_ref.shape[0]

    @pl.when(i == 0)
    def _():
        carry_ref[...] = jnp.zeros_like(carry_ref)

    logits = _dot(u_ref[...], w_ref[...].astype(BF16))
    sc = jax.nn.sigmoid(logits.T[:n_exp])
    biased = sc + b_ref[...]
    per = n_exp // N_GROUPS
    sub = lax.broadcasted_iota(I32, (per, tt), 0)
    gs_rows = []
    for gq in range(N_GROUPS):
        x8 = biased[gq * per:(gq + 1) * per]
        m1 = jnp.max(x8, axis=0, keepdims=True)
        first = jnp.min(jnp.where(x8 == m1, sub, per), axis=0, keepdims=True)
        m2 = jnp.max(jnp.where(sub == first, -jnp.inf, x8), axis=0, keepdims=True)
        gs_rows.append(m1 + m2)
    gs = jnp.concatenate(gs_rows, axis=0)
    gi = lax.broadcasted_iota(I32, gs.shape, 0)
    gcnt = jnp.zeros(gs.shape, F32)
    for j in range(N_GROUPS):
        vj = gs[j:j + 1]
        gcnt = gcnt + jnp.where((vj > gs) | ((vj == gs) & (gi > j)), 1.0, 0.0)
    gsel = jnp.where(gcnt < float(TOPK_GROUPS), 1.0, 0.0)
    emask = jnp.concatenate([jnp.broadcast_to(gsel[gq:gq + 1], (per, tt)) for gq in range(N_GROUPS)],
                            axis=0) > 0.5
    masked = jnp.where(emask, biased, -jnp.inf)
    ei = lax.broadcasted_iota(I32, masked.shape, 0)
    ecnt = jnp.zeros(masked.shape, F32)
    for j in range(n_exp):
        vj = masked[j:j + 1]
        ecnt = ecnt + jnp.where((vj > masked) | ((vj == masked) & (ei > j)), 1.0, 0.0)
    sel = ecnt < float(TOP_K)
    g = jnp.where(sel, sc, 0.0)
    gate_ref[...] = g / jnp.sum(g, axis=0, keepdims=True) * ROUTED_SCALE
    self_ = sel.astype(BF16)
    tr = lax.broadcasted_iota(I32, (tt, tt), 0)
    tc = lax.broadcasted_iota(I32, (tt, tt), 1)
    prefix = _dot(self_, (tr < tc).astype(BF16))
    carry = carry_ref[...]
    pos_ref[...] = jnp.where(sel, (prefix + carry).astype(I32), -1)
    carry = carry + jnp.sum(sel.astype(F32), axis=1, keepdims=True)
    carry_ref[...] = carry
    cnt_ref[...] = carry.astype(I32)


def _router(u, w_router, b_router):
    m, d = u.shape
    n_exp = w_router.shape[1]
    assert n_exp <= LANE and n_exp % (8 * N_GROUPS) == 0
    tt = LANE
    w_pad = jnp.pad(w_router, ((0, 0), (0, LANE - n_exp)))
    return pl.pallas_call(
        functools.partial(_router_kernel, n_exp=n_exp),
        out_shape=(jax.ShapeDtypeStruct((n_exp, m), F32), jax.ShapeDtypeStruct((n_exp, m), I32),
                   jax.ShapeDtypeStruct((n_exp, LANE), I32)),
        grid=(m // tt,),
        in_specs=[pl.BlockSpec((tt, d), lambda i: (i, 0)),
                  pl.BlockSpec((d, LANE), lambda i: (0, 0)),
                  pl.BlockSpec((n_exp, 1), lambda i: (0, 0))],
        out_specs=(pl.BlockSpec((n_exp, tt), lambda i: (0, i)),
                   pl.BlockSpec((n_exp, tt), lambda i: (0, i)),
                   pl.BlockSpec((n_exp, LANE), lambda i: (0, 0))),
        scratch_shapes=[pltpu.VMEM((n_exp, LANE), F32)],
        compiler_params=_cparams(("arbitrary",)),
        name="moe_router",
    )(u, w_pad, b_router.reshape(n_exp, 1))


def _slots_kernel(gate_ref, pos_ref, start_ref, dest_ref, gk_ref):
    pos = pos_ref[...]
    sel = pos >= 0
    n_exp, tt = pos.shape
    dest = start_ref[...] + pos
    er = lax.broadcasted_iota(I32, (n_exp, n_exp), 0)
    ec = lax.broadcasted_iota(I32, (n_exp, n_exp), 1)
    rank = _dot((ec < er).astype(BF16), sel.astype(BF16))
    gate = gate_ref[...]
    for k in range(TOP_K):
        pick = sel & (rank == float(k))
        dest_ref[k:k + 1, :] = jnp.sum(jnp.where(pick, dest, 0), axis=0, keepdims=True)
        gk_ref[k:k + 1, :] = jnp.sum(jnp.where(pick, gate, 0.0), axis=0, keepdims=True)


def _slots(gate_t, pos_t, start):
    n_exp, m = pos_t.shape
    tt = LANE
    return pl.pallas_call(
        _slots_kernel,
        out_shape=(jax.ShapeDtypeStruct((TOP_K, m), I32), jax.ShapeDtypeStruct((TOP_K, m), F32)),
        grid=(m // tt,),
        in_specs=[pl.BlockSpec((n_exp, tt), lambda i: (0, i)),
                  pl.BlockSpec((n_exp, tt), lambda i: (0, i)),
                  pl.BlockSpec((n_exp, 1), lambda i: (0, 0))],
        out_specs=(pl.BlockSpec((TOP_K, tt), lambda i: (0, i)),
                   pl.BlockSpec((TOP_K, tt), lambda i: (0, i))),
        compiler_params=_cparams(("arbitrary",)),
        name="moe_slots",
    )(gate_t, pos_t, start.reshape(n_exp, 1))


def _row_copy(src_hbm, dst_hbm, t, r, sem):
    return pltpu.make_async_copy(src_hbm.at[t], dst_hbm.at[r], sem)


def _dispatch_kernel(dest_ref, padlo_ref, up_hbm, xs_hbm, zero_ref, sem, zsem, *, tt, n_exp, rb):
    i = pl.program_id(0)

    @pl.when(i == 0)
    def _():
        zero_ref[...] = jnp.zeros_like(zero_ref)

        def zstart(e, c):
            pltpu.make_async_copy(zero_ref, xs_hbm.at[pl.ds(padlo_ref[e], rb)], zsem).start()
            return c

        def zwait(e, c):
            pltpu.make_async_copy(zero_ref, xs_hbm.at[pl.ds(padlo_ref[e], rb)], zsem).wait()
            return c

        lax.fori_loop(0, n_exp, zstart, 0)
        lax.fori_loop(0, n_exp, zwait, 0)

    def start(t, c):
        for k in range(TOP_K):
            _row_copy(up_hbm, xs_hbm, i * tt + t, dest_ref[k, t], sem).start()
        return c

    def wait(t, c):
        for k in range(TOP_K):
            _row_copy(up_hbm, xs_hbm, i * tt + t, dest_ref[k, t], sem).wait()
        return c

    lax.fori_loop(0, tt, start, 0)
    lax.fori_loop(0, tt, wait, 0)


def _dispatch(up3, dest3, pad_lo, n_rows, rb):
    m, s2, _ = up3.shape
    nt, _, tt = dest3.shape
    n_exp = pad_lo.shape[0]
    return pl.pallas_call(
        functools.partial(_dispatch_kernel, tt=tt, n_exp=n_exp, rb=rb),
        out_shape=jax.ShapeDtypeStruct((n_rows + rb, s2, LANE), U32),
        grid=(nt,),
        in_specs=[pl.BlockSpec((None, TOP_K, tt), lambda i: (i, 0, 0), memory_space=pltpu.SMEM),
                  pl.BlockSpec(memory_space=pltpu.SMEM),
                  pl.BlockSpec(memory_space=pl.ANY)],
        out_specs=pl.BlockSpec(memory_space=pl.ANY),
        scratch_shapes=[pltpu.VMEM((rb, s2, LANE), U32), pltpu.SemaphoreType.DMA(()),
                        pltpu.SemaphoreType.DMA(())],
        compiler_params=_cparams(("arbitrary",)),
        name="moe_dispatch",
    )(dest3, pad_lo, up3)


def _unpack_rows(x_ref, rb, s2):
    cols = []
    for s in range(s2):
        w = x_ref[pl.ds(s, rb, stride=s2), :]
        cols.append(pltpu.bitcast(w << 16, F32))
        cols.append(pltpu.bitcast(w & jnp.uint32(0xFFFF0000), F32))
    return jnp.concatenate(cols, axis=1).astype(BF16)


def _blk(i, meta_ref):
    return jnp.minimum(i, meta_ref[0] - 1)


def _expert_up_kernel(meta_ref, be_ref, x_ref, w1_ref, w3_ref, h_ref, wb1_ref, wb3_ref, *, rb, s2):
    i = pl.program_id(1)
    blk = _blk(i, meta_ref)
    changed = (i == 0) | (be_ref[blk] != be_ref[jnp.maximum(blk - 1, 0)])

    @pl.when(changed)
    def _():
        wb1_ref[...] = w1_ref[...].astype(BF16)
        wb3_ref[...] = w3_ref[...].astype(BF16)

    @pl.when(i < meta_ref[0])
    def _():
        x = _unpack_rows(x_ref, rb, s2)
        h_ref[...] = (_silu(_dot(x, wb1_ref[...])) * _dot(x, wb3_ref[...])).astype(BF16)


def _expert_up(meta, blk_e, xs2, w1, w3, n_blk, rb, s2):
    n_exp, d, hid = w1.shape
    th = 256
    wspec = pl.BlockSpec((None, d, th), lambda hh, i, meta, be: (be[_blk(i, meta)], 0, hh))
    return pl.pallas_call(
        functools.partial(_expert_up_kernel, rb=rb, s2=s2),
        out_shape=jax.ShapeDtypeStruct((n_blk * rb, hid), BF16),
        grid_spec=pltpu.PrefetchScalarGridSpec(
            num_scalar_prefetch=2,
            grid=(hid // th, n_blk),
            in_specs=[pl.BlockSpec((rb * s2, LANE), lambda hh, i, meta, be: (_blk(i, meta), 0)),
                      wspec, wspec],
            out_specs=pl.BlockSpec((rb, th), lambda hh, i, meta, be: (_blk(i, meta), hh)),
            scratch_shapes=[pltpu.VMEM((d, th), BF16), pltpu.VMEM((d, th), BF16)]),
        compiler_params=_cparams(("arbitrary", "arbitrary")),
        name="moe_expert_up",
    )(meta, blk_e, xs2, w1, w3)


def _expert_down_kernel(meta_ref, be_ref, h_ref, w2_ref, y_ref, *, rb, tn):
    i = pl.program_id(0)

    @pl.when(i < meta_ref[0])
    def _():
        h = h_ref[...]
        d = w2_ref.shape[1]
        s_all = d // LANE
        for c in range(d // tn):
            y = _dot(h, w2_ref[:, c * tn:(c + 1) * tn].astype(BF16))
            for s in range(tn // LANE):
                y_ref[pl.ds(c * (tn // LANE) + s, rb, stride=s_all), :] = y[:, s * LANE:(s + 1) * LANE]


def _expert_down(meta, blk_e, h, w2, n_blk, rb):
    n_exp, hid, d = w2.shape
    s_all = d // LANE
    return pl.pallas_call(
        functools.partial(_expert_down_kernel, rb=rb, tn=min(d, 512)),
        out_shape=jax.ShapeDtypeStruct((n_blk * rb * s_all, LANE), F32),
        grid_spec=pltpu.PrefetchScalarGridSpec(
            num_scalar_prefetch=2,
            grid=(n_blk,),
            in_specs=[pl.BlockSpec((rb, hid), lambda i, meta, be: (_blk(i, meta), 0)),
                      pl.BlockSpec((None, hid, d), lambda i, meta, be: (be[_blk(i, meta)], 0, 0))],
            out_specs=pl.BlockSpec((rb * s_all, LANE), lambda i, meta, be: (_blk(i, meta), 0))),
        compiler_params=_cparams(("arbitrary",)),
        name="moe_expert_down",
    )(meta, blk_e, h, w2)


def _slot_copy(y_hbm, buf_ref, dest_ref, k, t, s_all, sem):
    dst = buf_ref.at[k, pl.ds(pl.multiple_of(t * s_all, s_all), s_all)]
    return pltpu.make_async_copy(y_hbm.at[dest_ref[k, t]], dst, sem)


def _final_kernel(dest_ref, y_hbm, gk_ref, x1_ref, ysh_ref, g2_ref, lg_ref, lb_ref, o_ref,
                  buf_ref, sem, *, tf, alpha):
    i = pl.program_id(0)
    s_all = x1_ref.shape[1] // LANE

    def start(t, c):
        for k in range(TOP_K):
            _slot_copy(y_hbm, buf_ref, dest_ref, k, t, s_all, sem).start()
        return c

    def wait(t, c):
        for k in range(TOP_K):
            _slot_copy(y_hbm, buf_ref, dest_ref, k, t, s_all, sem).wait()
        return c

    lax.fori_loop(0, tf, start, 0)
    lax.fori_loop(0, tf, wait, 0)
    gk = gk_ref[...]
    cols = []
    for s in range(s_all):
        acc = None
        for k in range(TOP_K):
            term = gk[:, k:k + 1] * buf_ref[k, pl.ds(s, tf, stride=s_all), :]
            acc = term if acc is None else acc + term
        cols.append(acc)
    moe = jnp.concatenate(cols, axis=1) + ysh_ref[...]
    v = alpha * x1_ref[...] + g2_ref[...] * moe
    o_ref[...] = _ln_rows(v) * lg_ref[...] + lb_ref[...]


def _final(dest3, y3, gk, x1, ysh, mod_tiles, tile_grp, ln_g, ln_b, alpha):
    m, d = x1.shape
    tf = dest3.shape[2]
    assert tf == ROW_TILE
    s_all = d // LANE
    row = pl.BlockSpec((tf, d), lambda i: (i, 0))
    vec = pl.BlockSpec((1, d), lambda i: (0, 0))
    return pl.pallas_call(
        functools.partial(_final_kernel, tf=tf, alpha=alpha),
        out_shape=jax.ShapeDtypeStruct((m, d), F32),
        grid=(m // tf,),
        in_specs=[pl.BlockSpec((None, TOP_K, tf), lambda i: (i, 0, 0), memory_space=pltpu.SMEM),
                  pl.BlockSpec(memory_space=pl.ANY),
                  pl.BlockSpec((tf, TOP_K), lambda i: (i, 0)),
                  row, row, _mod_spec(5, d, tile_grp), vec, vec],
        out_specs=row,
        scratch_shapes=[pltpu.VMEM((TOP_K, tf * s_all, LANE), F32), pltpu.SemaphoreType.DMA(())],
        compiler_params=_cparams(("arbitrary",)),
        name="moe_combine_ln2",
    )(dest3, y3, gk, x1, ysh, mod_tiles, ln_g.reshape(1, d), ln_b.reshape(1, d))


def _rope_tables(pos):
    half = HD // 2
    inv = jnp.power(ROPE_THETA, -jnp.arange(half, dtype=F32) / half)
    ang = pos.astype(F32)[:, None] * inv[None, :]
    cos, sin = jnp.cos(ang), jnp.sin(ang)
    return jnp.concatenate([cos, cos], axis=1), jnp.concatenate([-sin, sin], axis=1)


def kernel(x_prompt, x_sample, c_prompt, c_sample, cache_cmp, cache_slc, cache_win, state_gla, page_table, w_ada, b_ada, w_in, w_gla_a2, b_gla_a2, gla_norm_g, cmp_pe_k, cmp_pe_v, cmp_w_k, cmp_w_v, w_br_gla, w_br_nsa, w_out, ln1_g, ln1_b, w_router, b_router, w_exp1, w_exp3, w_exp2, w_sh1, w_sh3, w_sh2, ln2_g, ln2_b):
    depth = w_in.shape[0]
    assert depth == 1
    alpha = (2.0 * depth) ** 0.25
    nb_, s_, d = x_prompt.shape
    db_, t_, _ = x_sample.shape
    mp, ns = nb_ * s_, db_ * t_
    m = mp + ns
    assert ns % ROW_TILE == 0 and s_ % ROW_TILE == 0 and d % 256 == 0
    n_pool, page = cache_cmp.shape[1], cache_cmp.shape[2]
    n_pages = page_table.shape[1]
    past_len = n_pages * page
    kvw2 = 2 * NSA_G * HD
    src, order, dst, n_used, _ = _col_plan(d)
    tn = 512
    small_off = dst['gla_a']
    dst = dict(dst)
    dst['nsa_g'] = small_off + GLA_RANK
    n_p = -(-(small_off + LANE) // tn) * tn

    rc = -(-(nb_ + db_) // 8) * 8
    c_all = jnp.pad(jnp.concatenate([c_prompt, c_sample], axis=0), ((0, rc - nb_ - db_), (0, 0)))
    mod = _ada(c_all, w_ada[0], b_ada[0])
    mod_tiles = jnp.concatenate(
        [jnp.broadcast_to(mod[:nb_, None, :], (nb_, ROW_TILE, 6 * d)),
         jnp.repeat(mod[nb_:nb_ + db_], t_, axis=0).reshape(ns // ROW_TILE, ROW_TILE, 6 * d)], axis=0)
    tiles_per_seq = s_ // ROW_TILE
    n_ptiles = mp // ROW_TILE
    tile_grp = lambda i: jnp.where(i < n_ptiles, i // tiles_per_seq, nb_ + i - n_ptiles)

    x_all = jnp.concatenate([x_prompt.reshape(mp, d), x_sample.reshape(ns, d)], axis=0)
    u1 = _ln_mod(x_all, mod_tiles, tile_grp)

    pieces = [w_in[0][:, src[n][0]:src[n][0] + src[n][1]] for n in order]
    w_p = jnp.concatenate(pieces + [jnp.zeros((d, n_p - n_used), F32)], axis=1)
    pos_all = jnp.concatenate([jnp.tile(jnp.arange(s_, dtype=I32), nb_),
                               jnp.tile(past_len + jnp.arange(t_, dtype=I32), db_)])
    cos_t, sin_t = _rope_tables(pos_all)
    tm = _divisor_tile(m, 1056, 16)
    rope_cols = ((dst['nsa_q'], dst['nsa_q'] + NSA_HEADS * HD),
                 (dst['nsa_ks'], dst['nsa_ks'] + NSA_G * HD),
                 (dst['nsa_kw'], dst['nsa_kw'] + NSA_G * HD))
    parts = _inproj(u1, w_p, cos_t, sin_t, rope_cols, tm, tn)

    zeros_state = jnp.zeros((nb_, GLA_HEADS, GLA_DK, GLA_DV), F32)
    og_p, gla_p = _gla(parts, nb_, s_, s_, 512, 128, dst, zeros_state, w_gla_a2[0], b_gla_a2[0],
                       gla_norm_g[0])
    t_pad = 16
    gla_cols = dst['nsa_q']
    smp = parts[mp:].reshape(db_, t_, n_p)
    smp_pad = jnp.pad(smp, ((0, 0), (0, t_pad - t_), (0, 0))).reshape(db_ * t_pad, n_p)
    og_s, gla_s = _gla(smp_pad, db_, t_pad, t_, t_pad, t_pad, dst, state_gla[0], w_gla_a2[0],
                       b_gla_a2[0], gla_norm_g[0])
    og_s = og_s.reshape(db_, t_pad, -1)[:, :t_].reshape(ns, -1)
    og = jnp.concatenate([og_p, og_s], axis=0)

    cmp_pos_p = jnp.tile((jnp.arange(s_ // CMP_BLOCK, dtype=I32) + 1) * CMP_BLOCK - 1, nb_)
    cos_c, sin_c = _rope_tables(cmp_pos_p)
    kvb_p = _compress_prompt(parts, mp, dst, cmp_w_k[0], cmp_w_v[0], cmp_pe_k[0], cmp_pe_v[0], cos_c, sin_c)
    on_p = _nsa_prompt(parts, kvb_p, nb_, s_, dst)

    cmp_pos_s = (jnp.arange(past_len // CMP_BLOCK, dtype=I32) + 1) * CMP_BLOCK - 1
    cos_cs, sin_cs = _rope_tables(cmp_pos_s)
    kvb_s = _compress_pages(cache_cmp[0].reshape(n_pool, page, kvw2), page_table, cmp_w_k[0], cmp_w_v[0],
                            cmp_pe_k[0], cmp_pe_v[0], cos_cs, sin_cs)
    tq_pad = 8
    pad_t = lambda a: jnp.pad(a, ((0, 0), (0, tq_pad - t_), (0, 0)))
    q_s = pad_t(smp[:, :, dst['nsa_q']:dst['nsa_q'] + NSA_HEADS * HD])
    gt_s = pad_t(smp[:, :, small_off:small_off + LANE])
    new_s = pad_t(smp[:, :, dst['nsa_ks']:dst['nsa_ks'] + 2 * kvw2])
    on_s = _nsa_sample(q_s, gt_s, kvb_s, cache_slc[0].reshape(n_pool, page, kvw2),
                       cache_win[0].reshape(db_, -1, kvw2), new_s, page_table, t_)
    on = jnp.concatenate([on_p, on_s[:, :t_].reshape(ns, -1).astype(BF16)], axis=0)

    z = _merge(og, on, w_br_gla[0], w_br_nsa[0], parts, dst['mg_gla'], dst['mg_nsa'], tm, tn)
    y = _plain(z, w_out[0], tm, tn, "out_proj")
    x1, u2, up = _mid(x_all, y, mod_tiles, tile_grp, ln1_g[0], ln1_b[0], alpha)

    n_exp = w_router.shape[2]
    rb = MOE_ROWS
    s2 = d // 256
    gate_t, pos_t, cnt = _router(u2, w_router[0], b_router[0])
    counts = cnt[:, 0]
    padded = (counts + rb - 1) // rb * rb
    pend = jnp.cumsum(padded)
    start = pend - padded
    n_blk = -(-(m * TOP_K + n_exp * (rb - 1)) // rb)
    blk_e = jnp.minimum(jnp.searchsorted(pend, jnp.arange(n_blk, dtype=I32) * rb, side='right'),
                        n_exp - 1).astype(I32)
    meta = (pend[-1:] // rb).astype(I32)
    dest, gk = _slots(gate_t, pos_t, start.astype(I32))
    nt = m // ROW_TILE
    dest3 = dest.reshape(TOP_K, nt, ROW_TILE).transpose(1, 0, 2)
    xs3 = _dispatch(up.reshape(m, s2, LANE), dest3, (start + counts).astype(I32), n_blk * rb, rb)
    h = _expert_up(meta, blk_e, xs3.reshape(-1, LANE), w_exp1[0], w_exp3[0], n_blk, rb, s2)
    y2 = _expert_down(meta, blk_e, h, w_exp2[0], n_blk, rb)
    hs = _swiglu(u2, w_sh1[0], w_sh3[0], tm, tn // 2)
    ysh = _plain(hs, w_sh2[0], tm, tn, "shared_down")
    out = _final(dest3, y2.reshape(n_blk * rb, d // LANE, LANE), gk.T, x1, ysh, mod_tiles, tile_grp,
                 ln2_g[0], ln2_b[0], alpha)

    def rows(lo, n_rows, name, lead):
        return parts[lo:lo + n_rows, dst[name]:dst[name] + kvw2].reshape(lead + (2, NSA_G, HD))

    win_buf = cache_win.shape[2]
    win_p = rows(0, mp, 'nsa_kw', (nb_, s_))[:, s_ - win_buf:]
    win_s = jnp.concatenate([cache_win[0], rows(mp, ns, 'nsa_kw', (db_, t_))], axis=1)[:, t_:]
    return (out[:mp].reshape(nb_, s_, d), out[mp:].reshape(db_, t_, d),
            rows(0, mp, 'nsa_kc', (nb_, s_))[None], rows(mp, ns, 'nsa_kc', (db_, t_))[None],
            rows(0, mp, 'nsa_ks', (nb_, s_))[None], rows(mp, ns, 'nsa_ks', (db_, t_))[None],
            win_p[None], win_s[None], gla_p[None], gla_s[None])
```

```python
import functools
import math

import jax
import jax.numpy as jnp
import numpy as np
from jax import lax
from jax.experimental import pallas as pl
from jax.experimental.pallas import tpu as pltpu

F32, BF16, I32, U32 = jnp.float32, jnp.bfloat16, jnp.int32, jnp.uint32
HIGHEST = lax.Precision.HIGHEST

GLA_HEADS, GLA_DK, GLA_DV, GLA_RANK, GLA_GATE_NORM = 8, 128, 256, 16, 16.0
NSA_HEADS, NSA_G, NSA_R, HD = 16, 4, 4, 128
KV_CH = 2 * NSA_G
CMP_BLOCK, SEL_BLOCK, N_SEL, WINDOW = 32, 64, 16, 512
ROPE_THETA = 10000.0
TOP_K, N_GROUPS, TOPK_GROUPS, ROUTED_SCALE = 8, 8, 4, 2.5
LN_EPS = 1e-5
NEG = -1e30

LANE = 128
VMEM_LIMIT = 56 * 1024 * 1024
ROW_TILE = 128
MOE_ROWS = 256
GLA_SUB = 32


def _cparams(sem):
    return pltpu.CompilerParams(dimension_semantics=sem, vmem_limit_bytes=VMEM_LIMIT)


def _divisor_tile(n, target, mult):
    best = None
    for d in range(mult, min(n, target) + 1, mult):
        if n % d == 0:
            best = d
    assert best is not None, (n, target, mult)
    return best


def _silu(x):
    return x * jax.nn.sigmoid(x)


def _ln_rows(x):
    mu = jnp.mean(x, axis=-1, keepdims=True)
    xc = x - mu
    var = jnp.mean(xc * xc, axis=-1, keepdims=True)
    return xc * lax.rsqrt(var + LN_EPS)


def _dot(a, b):
    return jnp.dot(a, b, preferred_element_type=F32)


def _dot_nt(a, b, precision=None):
    return lax.dot_general(a, b, (((1,), (1,)), ((), ())), precision=precision,
                           preferred_element_type=F32)


def _dot_tn(a, b, precision=None):
    return lax.dot_general(a, b, (((0,), (0,)), ((), ())), precision=precision,
                           preferred_element_type=F32)


def _col_plan(d_model):
    gqk, gv = GLA_HEADS * GLA_DK, GLA_HEADS * GLA_DV
    nq, nkv = NSA_HEADS * HD, NSA_G * HD
    ref = (('gla_q', gqk), ('gla_k', gqk), ('gla_v', gv), ('gla_r', gv), ('gla_a', GLA_RANK),
           ('nsa_q', nq), ('nsa_kc', nkv), ('nsa_vc', nkv), ('nsa_ks', nkv), ('nsa_vs', nkv),
           ('nsa_kw', nkv), ('nsa_vw', nkv), ('nsa_g', NSA_HEADS * 3), ('mg_gla', d_model),
           ('mg_nsa', d_model))
    src, o = {}, 0
    for n, w in ref:
        src[n] = (o, w)
        o += w
    order = ('gla_q', 'gla_k', 'gla_v', 'gla_r', 'nsa_q', 'nsa_kc', 'nsa_vc', 'nsa_ks', 'nsa_vs',
             'nsa_kw', 'nsa_vw', 'mg_gla', 'mg_nsa', 'gla_a', 'nsa_g')
    dst, o = {}, 0
    for n in order:
        dst[n] = o
        o += src[n][1]
    return src, order, dst, o, o


def _ada_kernel(c_ref, w_ref, b_ref, o_ref):
    a = _silu(c_ref[...]).astype(BF16)
    o_ref[...] = _dot(a, w_ref[...].astype(BF16)) + b_ref[...]


def _ada(c_all, w_ada, b_ada):
    rc, d = c_all.shape
    n = w_ada.shape[1]
    tn = 512
    return pl.pallas_call(
        _ada_kernel,
        out_shape=jax.ShapeDtypeStruct((rc, n), F32),
        grid=(n // tn,),
        in_specs=[pl.BlockSpec((rc, d), lambda j: (0, 0)),
                  pl.BlockSpec((d, tn), lambda j: (0, j)),
                  pl.BlockSpec((1, tn), lambda j: (0, j))],
        out_specs=pl.BlockSpec((rc, tn), lambda j: (0, j)),
        compiler_params=_cparams(("arbitrary",)),
        name="ada_mod",
    )(c_all, w_ada, b_ada.reshape(1, n))


def _mod_index(rows, mp, seq, nseq):
    n_pt, per_seq, per_grp = mp // rows, seq // rows, ROW_TILE // rows

    def index(i):
        j = jnp.maximum(i - n_pt, 0)
        return (jnp.where(i < n_pt, i // per_seq, nseq + j // per_grp),
                jnp.where(i < n_pt, 0, j % per_grp))

    return index


def _mod_spec(k, d, rows, mod_index):
    return pl.BlockSpec((None, rows, d), lambda i: (*mod_index(i), k))


def _ln_mod_kernel(x_ref, sh_ref, sc_ref, u_ref):
    y = _ln_rows(x_ref[...])
    u_ref[...] = (y * (1.0 + sc_ref[...]) + sh_ref[...]).astype(BF16)


def _ln_mod(x_all, mod_tiles, mod_index):
    m, d = x_all.shape
    return pl.pallas_call(
        _ln_mod_kernel,
        out_shape=jax.ShapeDtypeStruct((m, d), BF16),
        grid=(m // ROW_TILE,),
        in_specs=[pl.BlockSpec((ROW_TILE, d), lambda i: (i, 0)),
                  _mod_spec(0, d, ROW_TILE, mod_index), _mod_spec(1, d, ROW_TILE, mod_index)],
        out_specs=pl.BlockSpec((ROW_TILE, d), lambda i: (i, 0)),
        compiler_params=_cparams(("arbitrary",)),
        name="ln_mod1",
    )(x_all, mod_tiles, mod_tiles)


def _pack_pairs(u):
    bits = pltpu.bitcast(u, U32)
    out = []
    for s in range(u.shape[1] // 256):
        lo = bits[:, s * 256:s * 256 + 128]
        hi = bits[:, s * 256 + 128:s * 256 + 256]
        out.append((lo >> 16) | (hi & jnp.uint32(0xFFFF0000)))
    return out


def _mid_kernel(x_ref, y_ref, g1_ref, sh_ref, sc_ref, lg_ref, lb_ref, x1_ref, u_ref, up_ref, *, alpha):
    v = alpha * x_ref[...] + g1_ref[...] * y_ref[...]
    x1 = _ln_rows(v) * lg_ref[...] + lb_ref[...]
    x1_ref[...] = x1
    u = _ln_rows(x1) * (1.0 + sc_ref[...]) + sh_ref[...]
    ub = u.astype(BF16)
    u_ref[...] = ub
    words = _pack_pairs(ub.astype(F32))
    s2 = len(words)
    for s, w in enumerate(words):
        up_ref[pl.ds(s, ROW_TILE, stride=s2), :] = w


def _mid(x_all, y, mod_tiles, mod_index, ln_g, ln_b, alpha):
    m, d = x_all.shape
    s2 = d // 256
    row = pl.BlockSpec((ROW_TILE, d), lambda i: (i, 0))
    vec = pl.BlockSpec((1, d), lambda i: (0, 0))
    return pl.pallas_call(
        functools.partial(_mid_kernel, alpha=alpha),
        out_shape=(jax.ShapeDtypeStruct((m, d), F32), jax.ShapeDtypeStruct((m, d), BF16),
                   jax.ShapeDtypeStruct((m * s2, LANE), U32)),
        grid=(m // ROW_TILE,),
        in_specs=[row, row, _mod_spec(2, d, ROW_TILE, mod_index), _mod_spec(3, d, ROW_TILE, mod_index),
                  _mod_spec(4, d, ROW_TILE, mod_index), vec, vec],
        out_specs=(row, row, pl.BlockSpec((ROW_TILE * s2, LANE), lambda i: (i, 0))),
        compiler_params=_cparams(("arbitrary",)),
        name="ln1_mod2",
    )(x_all, y, mod_tiles, mod_tiles, mod_tiles, ln_g.reshape(1, d), ln_b.reshape(1, d))


def _cast_weights(i, pairs):
    @pl.when(i == 0)
    def _():
        for w_ref, wb_ref in pairs:
            wb_ref[...] = w_ref[...].astype(BF16)


def _inproj_kernel(u_ref, w_ref, cos_ref, sin_ref, o_ref, wb_ref, *, tn, rope_tiles):
    j, i = pl.program_id(0), pl.program_id(1)
    _cast_weights(i, ((w_ref, wb_ref),))
    acc = _dot(u_ref[...], wb_ref[...])
    is_rope = functools.reduce(jnp.logical_or, [(j >= lo) & (j < hi) for lo, hi in rope_tiles])

    @pl.when(is_rope)
    def _():
        cos, sin = cos_ref[...], sin_ref[...]
        for h in range(tn // HD):
            a = acc[:, h * HD:(h + 1) * HD]
            o_ref[:, h * HD:(h + 1) * HD] = a * cos + pltpu.roll(a, HD // 2, axis=1) * sin

    @pl.when(jnp.logical_not(is_rope))
    def _():
        o_ref[...] = acc


def _inproj(u, w_p, cos_t, sin_t, rope_cols, tm, tn):
    m, d = u.shape
    n = w_p.shape[1]
    rope_tiles = tuple((lo // tn, hi // tn) for lo, hi in rope_cols)
    return pl.pallas_call(
        functools.partial(_inproj_kernel, tn=tn, rope_tiles=rope_tiles),
        out_shape=jax.ShapeDtypeStruct((m, n), F32),
        grid=(n // tn, m // tm),
        in_specs=[pl.BlockSpec((tm, d), lambda j, i: (i, 0)),
                  pl.BlockSpec((d, tn), lambda j, i: (0, j)),
                  pl.BlockSpec((tm, HD), lambda j, i: (i, 0)),
                  pl.BlockSpec((tm, HD), lambda j, i: (i, 0))],
        out_specs=pl.BlockSpec((tm, tn), lambda j, i: (i, j)),
        scratch_shapes=[pltpu.VMEM((d, tn), BF16)],
        compiler_params=_cparams(("arbitrary", "arbitrary")),
        name="in_proj",
    )(u, w_p, cos_t, sin_t)


def _merge_kernel(a1_ref, a2_ref, w1_ref, w2_ref, g1_ref, g2_ref, o_ref, wb1_ref, wb2_ref):
    _cast_weights(pl.program_id(1), ((w1_ref, wb1_ref), (w2_ref, wb2_ref)))
    z = (jax.nn.sigmoid(g1_ref[...]) * _dot(a1_ref[...], wb1_ref[...])
         + jax.nn.sigmoid(g2_ref[...]) * _dot(a2_ref[...], wb2_ref[...]))
    o_ref[...] = z.astype(BF16)


def _merge(og, on, w1, w2, parts, off1, off2, tm, tn):
    m, k = og.shape
    n = w1.shape[1]
    return pl.pallas_call(
        _merge_kernel,
        out_shape=jax.ShapeDtypeStruct((m, n), BF16),
        grid=(n // tn, m // tm),
        in_specs=[pl.BlockSpec((tm, k), lambda j, i: (i, 0)),
                  pl.BlockSpec((tm, k), lambda j, i: (i, 0)),
                  pl.BlockSpec((k, tn), lambda j, i: (0, j)),
                  pl.BlockSpec((k, tn), lambda j, i: (0, j)),
                  pl.BlockSpec((tm, tn), lambda j, i: (i, off1 // tn + j)),
                  pl.BlockSpec((tm, tn), lambda j, i: (i, off2 // tn + j))],
        out_specs=pl.BlockSpec((tm, tn), lambda j, i: (i, j)),
        scratch_shapes=[pltpu.VMEM((k, tn), BF16), pltpu.VMEM((k, tn), BF16)],
        compiler_params=_cparams(("arbitrary", "arbitrary")),
        name="branch_merge",
    )(og, on, w1, w2, parts, parts)


def _plain_kernel(a_ref, w_ref, o_ref, wb_ref):
    _cast_weights(pl.program_id(1), ((w_ref, wb_ref),))
    o_ref[...] = _dot(a_ref[...], wb_ref[...])


def _plain(a, w, tm, tn, name):
    m, k = a.shape
    n = w.shape[1]
    return pl.pallas_call(
        _plain_kernel,
        out_shape=jax.ShapeDtypeStruct((m, n), F32),
        grid=(n // tn, m // tm),
        in_specs=[pl.BlockSpec((tm, k), lambda j, i: (i, 0)),
                  pl.BlockSpec((k, tn), lambda j, i: (0, j))],
        out_specs=pl.BlockSpec((tm, tn), lambda j, i: (i, j)),
        scratch_shapes=[pltpu.VMEM((k, tn), BF16)],
        compiler_params=_cparams(("arbitrary", "arbitrary")),
        name=name,
    )(a, w)


def _swiglu_kernel(a_ref, w1_ref, w3_ref, o_ref, wb1_ref, wb3_ref):
    _cast_weights(pl.program_id(1), ((w1_ref, wb1_ref), (w3_ref, wb3_ref)))
    a = a_ref[...]
    o_ref[...] = (_silu(_dot(a, wb1_ref[...])) * _dot(a, wb3_ref[...])).astype(BF16)


def _swiglu(a, w1, w3, tm, tn):
    m, k = a.shape
    n = w1.shape[1]
    return pl.pallas_call(
        _swiglu_kernel,
        out_shape=jax.ShapeDtypeStruct((m, n), BF16),
        grid=(n // tn, m // tm),
        in_specs=[pl.BlockSpec((tm, k), lambda j, i: (i, 0)),
                  pl.BlockSpec((k, tn), lambda j, i: (0, j)),
                  pl.BlockSpec((k, tn), lambda j, i: (0, j))],
        out_specs=pl.BlockSpec((tm, tn), lambda j, i: (i, j)),
        scratch_shapes=[pltpu.VMEM((k, tn), BF16), pltpu.VMEM((k, tn), BF16)],
        compiler_params=_cparams(("arbitrary", "arbitrary")),
        name="shared_swiglu",
    )(a, w1, w3)


def _gla_kernel(q_ref, k_ref, v_ref, r_ref, a_ref, wa_ref, ba_ref, ng_ref, s0_ref,
                og_ref, sf_ref, st_ref, *, tb, chunk, s_valid):
    i = pl.program_id(2)

    @pl.when(i == 0)
    def _():
        st_ref[...] = s0_ref[...]

    x = jnp.dot(a_ref[:, :GLA_RANK], wa_ref[...], precision=HIGHEST,
                preferred_element_type=F32) + ba_ref[...]
    glog = (jnp.minimum(x, 0.0) - jnp.log1p(jnp.exp(-jnp.abs(x)))) * (1.0 / GLA_GATE_NORM)
    row = i * tb + lax.broadcasted_iota(I32, (tb, 1), 0)
    glog = jnp.where(row < s_valid, glog, 0.0)

    sub = min(GLA_SUB, chunk)
    rr = lax.broadcasted_iota(I32, (chunk, chunk), 0)
    cc = lax.broadcasted_iota(I32, (chunk, chunk), 1)
    tri = (rr >= cc).astype(F32)
    ones = jnp.ones((chunk, GLA_DK), F32)
    ng = ng_ref[...]

    for c in range(tb // chunk):
        lo = c * chunk
        g = glog[lo:lo + chunk]
        bc = jnp.dot(tri, g, precision=HIGHEST, preferred_element_type=F32)
        bl = bc[chunk - 1:chunk]
        q = q_ref[lo:lo + chunk, :] * (GLA_DK ** -0.5)
        k = k_ref[lo:lo + chunk, :]
        vb = v_ref[lo:lo + chunk, :].astype(BF16)
        qg = (q * jnp.exp(bc)).astype(BF16)
        kd = (k * jnp.exp(bl - bc)).astype(BF16)
        state = st_ref[...]
        o_inter = _dot(qg, state.astype(BF16))
        outs = []
        for sb in range(chunk // sub):
            r0, r1 = sb * sub, (sb + 1) * sub
            base = bc[r0 - 1:r0] if sb > 0 else jnp.zeros((1, GLA_DK), F32)
            qs = (q[r0:r1] * jnp.exp(bc[r0:r1] - base)).astype(BF16)
            ks = (k[:r1] * jnp.exp(base - bc[:r1])).astype(BF16)
            att = _dot_nt(qs, ks)
            causal = (r0 + lax.broadcasted_iota(I32, (sub, r1), 0)) >= lax.broadcasted_iota(I32, (sub, r1), 1)
            att = jnp.where(causal, att, 0.0)
            outs.append(_dot(att.astype(BF16), vb[:r1]))
        o = jnp.concatenate(outs, axis=0) + o_inter
        dec = jnp.exp(_dot_tn(g, ones, precision=HIGHEST))
        dec = jnp.concatenate([dec] * (GLA_DV // GLA_DK), axis=1)
        st_ref[...] = dec * state + _dot_tn(kd, vb)
        o = o * lax.rsqrt(jnp.mean(o * o, axis=-1, keepdims=True) + LN_EPS) * ng
        og_ref[lo:lo + chunk, :] = (o * _silu(r_ref[lo:lo + chunk, :])).astype(BF16)

    @pl.when(i == pl.num_programs(2) - 1)
    def _():
        sf_ref[...] = st_ref[...]


def _gla(src, nseq, s_pad, s_valid, tb, chunk, dst, s0, w_a2, b_a2, norm_g):
    nb = s_pad // tb
    rows = nseq * s_pad
    qo, ko = dst['gla_q'] // GLA_DK, dst['gla_k'] // GLA_DK
    vo, ro = dst['gla_v'] // GLA_DV, dst['gla_r'] // GLA_DV
    ao = dst['gla_a'] // LANE
    rowblk = lambda b, h, i: b * nb + i
    st_spec = pl.BlockSpec((None, None, GLA_DK, GLA_DV), lambda b, h, i: (b, h, 0, 0))
    return pl.pallas_call(
        functools.partial(_gla_kernel, tb=tb, chunk=chunk, s_valid=s_valid),
        out_shape=(jax.ShapeDtypeStruct((rows, GLA_HEADS * GLA_DV), BF16),
                   jax.ShapeDtypeStruct((nseq, GLA_HEADS, GLA_DK, GLA_DV), F32)),
        grid=(nseq, GLA_HEADS, nb),
        in_specs=[pl.BlockSpec((tb, GLA_DK), lambda b, h, i: (rowblk(b, h, i), qo + h)),
                  pl.BlockSpec((tb, GLA_DK), lambda b, h, i: (rowblk(b, h, i), ko + h)),
                  pl.BlockSpec((tb, GLA_DV), lambda b, h, i: (rowblk(b, h, i), vo + h)),
                  pl.BlockSpec((tb, GLA_DV), lambda b, h, i: (rowblk(b, h, i), ro + h)),
                  pl.BlockSpec((tb, LANE), lambda b, h, i: (rowblk(b, h, i), ao)),
                  pl.BlockSpec((GLA_RANK, GLA_DK), lambda b, h, i: (0, h)),
                  pl.BlockSpec((1, GLA_DK), lambda b, h, i: (0, h)),
                  pl.BlockSpec((1, GLA_DV), lambda b, h, i: (0, 0)),
                  st_spec],
        out_specs=(pl.BlockSpec((tb, GLA_DV), lambda b, h, i: (rowblk(b, h, i), h)), st_spec),
        scratch_shapes=[pltpu.VMEM((GLA_DK, GLA_DV), F32)],
        compiler_params=_cparams(("arbitrary", "arbitrary", "arbitrary")),
        name="gla_scan",
    )(src, src, src, src, src, w_a2, b_a2.reshape(1, -1), norm_g.reshape(1, -1), s0)


def _rope_heads(x, cos, sin):
    return jnp.concatenate(
        [x[:, h * HD:(h + 1) * HD] * cos + pltpu.roll(x[:, h * HD:(h + 1) * HD], HD // 2, axis=1) * sin
         for h in range(x.shape[1] // HD)], axis=1)


def _compress_rows(x, wk_ref, wv_ref, pek_ref, pev_ref, cos, sin):
    rows, width = x.shape
    nb, half = rows // CMP_BLOCK, width // 2

    def softmax_col(w_ref):
        w = w_ref[...]
        e = jnp.exp(w - jnp.max(w, axis=0, keepdims=True))
        return e / jnp.sum(e, axis=0, keepdims=True)

    wk, wv = softmax_col(wk_ref), softmax_col(wv_ref)
    w2 = jnp.concatenate([jnp.broadcast_to(wk, (CMP_BLOCK, half)),
                          jnp.broadcast_to(wv, (CMP_BLOCK, half))], axis=1)
    y = jnp.sum(x.reshape(nb, CMP_BLOCK, width) * w2[None], axis=1)
    pk = jnp.sum(pek_ref[...] * wk, axis=0, keepdims=True)
    pv = jnp.sum(pev_ref[...] * wv, axis=0, keepdims=True)
    kb = y[:, :half] + jnp.concatenate([pk] * NSA_G, axis=1)
    vb = y[:, half:] + jnp.concatenate([pv] * NSA_G, axis=1)
    return jnp.concatenate([_rope_heads(kb, cos, sin), vb], axis=1)


def _compress_prompt_kernel(x_ref, wk_ref, wv_ref, pek_ref, pev_ref, cos_ref, sin_ref, o_ref):
    o_ref[...] = _compress_rows(x_ref[...], wk_ref, wv_ref, pek_ref, pev_ref, cos_ref[...], sin_ref[...])


def _small_specs(nargs):
    col = pl.BlockSpec((CMP_BLOCK, 1), lambda *a: (0, 0))
    pe = pl.BlockSpec((CMP_BLOCK, HD), lambda *a: (0, 0))
    return [col, col, pe, pe]


def _compress_prompt(parts, rows, dst, w_k, w_v, pe_k, pe_v, cos_c, sin_c):
    width = 2 * NSA_G * HD
    tr = _divisor_tile(rows, 1024, 256)
    nbt = tr // CMP_BLOCK
    return pl.pallas_call(
        _compress_prompt_kernel,
        out_shape=jax.ShapeDtypeStruct((rows // CMP_BLOCK, width), F32),
        grid=(rows // tr,),
        in_specs=[pl.BlockSpec((tr, width), lambda i: (i, dst['nsa_kc'] // width))] + _small_specs(1)
        + [pl.BlockSpec((nbt, HD), lambda i: (i, 0)), pl.BlockSpec((nbt, HD), lambda i: (i, 0))],
        out_specs=pl.BlockSpec((nbt, width), lambda i: (i, 0)),
        compiler_params=_cparams(("arbitrary",)),
        name="nsa_compress_prompt",
    )(parts, w_k.reshape(-1, 1), w_v.reshape(-1, 1), pe_k, pe_v, cos_c, sin_c)


def _compress_pages_kernel(pt_ref, *refs, pg):
    page_refs = refs[:pg]
    wk_ref, wv_ref, pek_ref, pev_ref, cos_ref, sin_ref, o_ref = refs[pg:]

    def softmax_col(w_ref):
        w = w_ref[...]
        e = jnp.exp(w - jnp.max(w, axis=0, keepdims=True))
        return e / jnp.sum(e, axis=0, keepdims=True)

    wk, wv = softmax_col(wk_ref), softmax_col(wv_ref)
    blk_rows = CMP_BLOCK * KV_CH
    onehot = ((lax.broadcasted_iota(I32, (blk_rows, CMP_BLOCK), 0) >> 3)
              == lax.broadcasted_iota(I32, (blk_rows, CMP_BLOCK), 1)).astype(F32)
    spread = lambda w: jnp.dot(onehot, jnp.broadcast_to(w, (CMP_BLOCK, HD)), precision=HIGHEST,
                               preferred_element_type=F32)
    is_key = (lax.broadcasted_iota(I32, (blk_rows, HD), 0) & (KV_CH - 1)) < NSA_G
    wexp = jnp.where(is_key, spread(wk), spread(wv))
    pk = jnp.sum(pek_ref[...] * wk, axis=0, keepdims=True)
    pv = jnp.sum(pev_ref[...] * wv, axis=0, keepdims=True)
    pe8 = jnp.where(lax.broadcasted_iota(I32, (KV_CH, HD), 0) < NSA_G, pk, pv)
    outs = []
    for p in range(pg):
        x = page_refs[p][...]
        for n in range(x.shape[0] // blk_rows):
            xw = x[n * blk_rows:(n + 1) * blk_rows] * wexp
            acc = pe8
            for l in range(CMP_BLOCK):
                acc = acc + xw[l * KV_CH:(l + 1) * KV_CH]
            outs.append(acc)
    y = jnp.concatenate(outs, axis=0)
    o_ref[...] = y * cos_ref[...] + pltpu.roll(y, HD // 2, axis=1) * sin_ref[...]


def _compress_pages(cache2, page, page_table, w_k, w_v, pe_k, pe_v, cos_e, sin_e):
    nseq, n_pages = page_table.shape
    pg = _divisor_tile(n_pages, 8, 1)
    out_rows = page // CMP_BLOCK * KV_CH
    page_specs = [pl.BlockSpec((page * KV_CH, HD),
                               (lambda b, i, pt, p=p: (pt[b * n_pages + i * pg + p], 0)))
                  for p in range(pg)]
    small = [pl.BlockSpec((CMP_BLOCK, 1), lambda b, i, pt: (0, 0))] * 2 \
        + [pl.BlockSpec((CMP_BLOCK, HD), lambda b, i, pt: (0, 0))] * 2
    tab = [pl.BlockSpec((pg * out_rows, HD), lambda b, i, pt: (i, 0))] * 2
    return pl.pallas_call(
        functools.partial(_compress_pages_kernel, pg=pg),
        out_shape=jax.ShapeDtypeStruct((nseq, n_pages * out_rows, HD), F32),
        grid_spec=pltpu.PrefetchScalarGridSpec(
            num_scalar_prefetch=1,
            grid=(nseq, n_pages // pg),
            in_specs=page_specs + small + tab,
            out_specs=pl.BlockSpec((None, pg * out_rows, HD), lambda b, i, pt: (b, i, 0))),
        compiler_params=_cparams(("arbitrary", "arbitrary")),
        name="nsa_compress_pages",
    )(page_table.reshape(-1), *([cache2] * pg), w_k.reshape(-1, 1), w_v.reshape(-1, 1), pe_k, pe_v,
      cos_e, sin_e)


def _select_blocks(psum, cur, extra_forced):
    rows, nb = psum.shape
    lane = lax.broadcasted_iota(I32, (rows, nb), 1)
    jl = lane >> 1
    pair = jnp.where((lane & 1) == 0, psum + pltpu.roll(psum, nb - 1, axis=1),
                     psum + pltpu.roll(psum, 1, axis=1))
    forced = (jl == 0) | (jl == cur) | (jl == cur - 1)
    imp = jnp.where(forced, jnp.inf, jnp.where(jl <= cur, pair, -jnp.inf))
    cnt = jnp.where(imp < jnp.inf, float(extra_forced), 0.0)
    for j in range(nb // 2):
        vj = imp[:, 2 * j:2 * j + 1]
        beats = (vj > imp) | ((vj == imp) & (jl > j))
        cnt = cnt + jnp.where(beats, 1.0, 0.0)
    return cnt < float(N_SEL)


def _flash_update(carry, s, mask, vb):
    m, l, acc = carry
    s = jnp.where(mask, s, NEG)
    m_new = jnp.maximum(m, jnp.max(s, axis=-1, keepdims=True))
    p = jnp.where(mask, jnp.exp(s - m_new), 0.0)
    alpha = jnp.exp(m - m_new)
    l = alpha * l + jnp.sum(p, axis=-1, keepdims=True)
    acc = alpha * acc + _dot(p.astype(BF16), vb)
    return m_new, l, acc


def _flash_init(rows):
    return (jnp.full((rows, 1), NEG, F32), jnp.zeros((rows, 1), F32), jnp.zeros((rows, HD), F32))


def _compressed_branch(qb, kcb, vcb, qpos):
    s = _dot_nt(qb, kcb.astype(BF16))
    n = lax.broadcasted_iota(I32, s.shape, 1)
    vis = ((n + 1) * CMP_BLOCK - 1) <= qpos
    s = jnp.where(vis, s, NEG)
    p = jnp.where(vis, jnp.exp(s - jnp.max(s, axis=-1, keepdims=True)), 0.0)
    den = jnp.sum(p, axis=-1, keepdims=True)
    p = p / jnp.where(den > 0.0, den, 1.0)
    return _dot(p.astype(BF16), vcb.astype(BF16)), p


def _expand_blocks(sel_f, first_block, n_keys):
    nb = sel_f.shape[1]
    n = lax.broadcasted_iota(I32, (nb, n_keys), 0)
    kk = lax.broadcasted_iota(I32, (nb, n_keys), 1)
    e = (n == first_block + (kk >> 5)).astype(BF16)
    return _dot(sel_f.astype(BF16), e) > 0.5


def _nsa_prompt_kernel(q_ref, gt_ref, kcb_ref, vcb_ref, ks_ref, vs_ref, kw_ref, vw_ref, o_ref,
                       *, tq, tk, gate_col):
    g, qi = pl.program_id(1), pl.program_id(2)
    q4 = q_ref[...]
    qb = (jnp.concatenate([q4[:, r * HD:(r + 1) * HD] for r in range(NSA_R)], axis=0)
          * (HD ** -0.5)).astype(BF16)
    rows = NSA_R * tq
    q0 = qi * tq
    qpos1 = q0 + lax.broadcasted_iota(I32, (tq, 1), 0)
    qpos = jnp.concatenate([qpos1] * NSA_R, axis=0)

    o_c, p_c = _compressed_branch(qb, kcb_ref[...], vcb_ref[...], qpos)
    psum = functools.reduce(jnp.add, [p_c[r * tq:(r + 1) * tq] for r in range(NSA_R)])
    sel = _select_blocks(psum, qpos1 >> 6, 0).astype(F32)

    def sel_body(kt, carry):
        start = pl.multiple_of(kt * tk, tk)
        kb = ks_ref[pl.ds(start, tk), :].astype(BF16)
        vb = vs_ref[pl.ds(start, tk), :].astype(BF16)
        s = _dot_nt(qb, kb)
        m1 = _expand_blocks(sel, kt * (tk // CMP_BLOCK), tk)
        kpos = start + lax.broadcasted_iota(I32, (1, tk), 1)
        mask = jnp.concatenate([m1] * NSA_R, axis=0) & (kpos <= qpos)
        return _flash_update(carry, s, mask, vb)

    n_kt = (q0 + tq + tk - 1) // tk
    m_s, l_s, acc_s = lax.fori_loop(0, n_kt, sel_body, _flash_init(rows))
    o_s = acc_s / l_s

    carry = _flash_init(rows)
    for wt in range(WINDOW // tq + 1):
        start = q0 - WINDOW + wt * tq
        cstart = pl.multiple_of(jnp.maximum(start, 0), tq)
        kb = kw_ref[pl.ds(cstart, tq), :].astype(BF16)
        vb = vw_ref[pl.ds(cstart, tq), :].astype(BF16)
        s = _dot_nt(qb, kb)
        kpos = start + lax.broadcasted_iota(I32, (1, tq), 1)
        mask = (kpos <= qpos) & (kpos > qpos - WINDOW) & (kpos >= 0)
        carry = _flash_update(carry, s, mask, vb)
    o_w = carry[2] / carry[1]

    gt = jax.nn.sigmoid(gt_ref[...])
    lane = lax.broadcasted_iota(I32, gt.shape, 1)
    outs = []
    for r in range(NSA_R):
        col = gate_col + (g * NSA_R + r) * 3
        gs = [jnp.sum(jnp.where(lane == col + j, gt, 0.0), axis=-1, keepdims=True) for j in range(3)]
        sl = slice(r * tq, (r + 1) * tq)
        outs.append(gs[0] * o_c[sl] + gs[1] * o_s[sl] + gs[2] * o_w[sl])
    o_ref[...] = jnp.concatenate(outs, axis=1).astype(BF16)


def _nsa_prompt(parts, kvb, nseq, s, dst):
    tq, tk = 256, 512
    nq = s // tq
    nb = s // CMP_BLOCK
    assert s % tk == 0 and WINDOW % tq == 0 and nb % LANE == 0
    gw = NSA_R * HD
    kv = lambda name: pl.BlockSpec((s, HD), lambda b, g, i: (b, dst[name] // HD + g))
    return pl.pallas_call(
        functools.partial(_nsa_prompt_kernel, tq=tq, tk=tk, gate_col=GLA_RANK),
        out_shape=jax.ShapeDtypeStruct((nseq * s, NSA_HEADS * HD), BF16),
        grid=(nseq, NSA_G, nq),
        in_specs=[pl.BlockSpec((tq, gw), lambda b, g, i: (b * nq + i, dst['nsa_q'] // gw + g)),
                  pl.BlockSpec((tq, LANE), lambda b, g, i: (b * nq + i, dst['gla_a'] // LANE)),
                  pl.BlockSpec((nb, HD), lambda b, g, i: (b, g)),
                  pl.BlockSpec((nb, HD), lambda b, g, i: (b, NSA_G + g)),
                  kv('nsa_ks'), kv('nsa_vs'), kv('nsa_kw'), kv('nsa_vw')],
        out_specs=pl.BlockSpec((tq, gw), lambda b, g, i: (b * nq + i, g)),
        compiler_params=_cparams(("arbitrary", "arbitrary", "arbitrary")),
        name="nsa_prompt",
    )(parts, parts, kvb, kvb, parts, parts, parts, parts)


def _nsa_sample_kernel(pt_ref, *refs, pg, n_pages, t_new, t_pad, past_len, win_buf):
    page_refs = refs[:pg]
    q_ref, gt_ref, kvb_ref, win_ref, new_ref, o_ref, sel_ref, m_ref, l_ref, acc_ref, oc_ref = refs[pg:]
    i = pl.program_id(1)
    rows = NSA_R * t_pad
    page = page_refs[0].shape[0] // KV_CH
    kvw = NSA_G * HD
    nbc = kvb_ref.shape[0] // KV_CH
    chan = lambda ref, c, n: ref[pl.ds(c, n, stride=KV_CH), :]
    t1 = lax.broadcasted_iota(I32, (t_pad, 1), 0)
    t_row = jnp.concatenate([t1] * NSA_R, axis=0)
    qpos = past_len + t_row

    def queries(g):
        q = q_ref[...]
        return (jnp.concatenate([q[:, (g * NSA_R + r) * HD:(g * NSA_R + r + 1) * HD]
                                 for r in range(NSA_R)], axis=0) * (HD ** -0.5)).astype(BF16)

    @pl.when(i == 0)
    def _():
        for g in range(NSA_G):
            o_c, p_c = _compressed_branch(queries(g), chan(kvb_ref, g, nbc), chan(kvb_ref, NSA_G + g, nbc),
                                          qpos)
            psum = functools.reduce(jnp.add, [p_c[r * t_pad:(r + 1) * t_pad] for r in range(NSA_R)])
            cur = (past_len + t1) >> 6
            sel_ref[g] = _select_blocks(psum, cur, 1).astype(F32)
            oc_ref[g] = o_c
            m_ref[g] = jnp.full((rows, LANE), NEG, F32)
            l_ref[g] = jnp.zeros((rows, LANE), F32)
            acc_ref[g] = jnp.zeros((rows, HD), F32)

    bpp = page // CMP_BLOCK
    for g in range(NSA_G):
        qb = queries(g)
        carry = (m_ref[g][:, :1], l_ref[g][:, :1], acc_ref[g])
        kb = jnp.concatenate([chan(x, g, page).astype(BF16) for x in page_refs], axis=0)
        vb = jnp.concatenate([chan(x, NSA_G + g, page).astype(BF16) for x in page_refs], axis=0)
        m1 = _expand_blocks(sel_ref[g], i * pg * bpp, pg * page)
        mask = jnp.concatenate([m1] * NSA_R, axis=0)
        carry = _flash_update(carry, _dot_nt(qb, kb), mask, vb)
        m_ref[g] = jnp.broadcast_to(carry[0], (rows, LANE))
        l_ref[g] = jnp.broadcast_to(carry[1], (rows, LANE))
        acc_ref[g] = carry[2]

    @pl.when(i == pl.num_programs(1) - 1)
    def _():
        gt = jax.nn.sigmoid(gt_ref[...])
        new = new_ref[...]
        pad = jnp.zeros((LANE - t_pad, HD), F32)
        kk = lax.broadcasted_iota(I32, (1, LANE), 1)
        new_mask = (kk <= t_row) & (kk < t_new)
        wi = lax.broadcasted_iota(I32, (1, win_buf), 1)
        wpos = past_len - win_buf + wi
        win_mask = (wpos <= qpos) & (wpos > qpos - WINDOW) & (wpos >= 0)
        outs = []
        for g in range(NSA_G):
            qb = queries(g)

            def new_rows(j):
                return jnp.concatenate([new[:, j * kvw + g * HD:j * kvw + (g + 1) * HD], pad],
                                       axis=0).astype(BF16)

            carry = (m_ref[g][:, :1], l_ref[g][:, :1], acc_ref[g])
            carry = _flash_update(carry, _dot_nt(qb, new_rows(0)), new_mask, new_rows(1))
            o_s = carry[2] / carry[1]
            carry = _flash_init(rows)
            carry = _flash_update(carry, _dot_nt(qb, chan(win_ref, g, win_buf).astype(BF16)), win_mask,
                                  chan(win_ref, NSA_G + g, win_buf).astype(BF16))
            carry = _flash_update(carry, _dot_nt(qb, new_rows(2)), new_mask, new_rows(3))
            o_w = carry[2] / carry[1]
            o_c = oc_ref[g]
            for r in range(NSA_R):
                col = GLA_RANK + (g * NSA_R + r) * 3
                sl = slice(r * t_pad, (r + 1) * t_pad)
                outs.append(gt[:, col:col + 1] * o_c[sl] + gt[:, col + 1:col + 2] * o_s[sl]
                            + gt[:, col + 2:col + 3] * o_w[sl])
        o_ref[...] = jnp.concatenate(outs, axis=1)


def _nsa_sample(q_s, gt_s, kvb_s, slc2, page, win2, win_buf, new_s, page_table, t_new):
    nseq, t_pad, _ = q_s.shape
    width = KV_CH * HD
    n_pages = page_table.shape[1]
    past_len = n_pages * page
    nbc = kvb_s.shape[1] // KV_CH
    assert nbc % LANE == 0 and t_new <= SEL_BLOCK and past_len % SEL_BLOCK == 0
    pg = _divisor_tile(n_pages, 16, 1)
    rows = NSA_R * t_pad
    page_specs = [pl.BlockSpec((page * KV_CH, HD),
                               (lambda b, i, pt, p=p: (pt[b * n_pages + i * pg + p], 0)))
                  for p in range(pg)]
    per_seq = lambda shape: pl.BlockSpec((None,) + shape, lambda b, i, pt: (b, 0, 0))
    return pl.pallas_call(
        functools.partial(_nsa_sample_kernel, pg=pg, n_pages=n_pages, t_new=t_new, t_pad=t_pad,
                          past_len=past_len, win_buf=win_buf),
        out_shape=jax.ShapeDtypeStruct((nseq, t_pad, NSA_HEADS * HD), F32),
        grid_spec=pltpu.PrefetchScalarGridSpec(
            num_scalar_prefetch=1,
            grid=(nseq, n_pages // pg),
            in_specs=page_specs + [per_seq((t_pad, NSA_HEADS * HD)), per_seq((t_pad, LANE)),
                                   per_seq((nbc * KV_CH, HD)),
                                   pl.BlockSpec((win_buf * KV_CH, HD), lambda b, i, pt: (b, 0)),
                                   per_seq((t_pad, 2 * width))],
            out_specs=per_seq((t_pad, NSA_HEADS * HD)),
            scratch_shapes=[pltpu.VMEM((NSA_G, t_pad, nbc), F32),
                            pltpu.VMEM((NSA_G, rows, LANE), F32),
                            pltpu.VMEM((NSA_G, rows, LANE), F32),
                            pltpu.VMEM((NSA_G, rows, HD), F32),
                            pltpu.VMEM((NSA_G, rows, HD), F32)]),
        compiler_params=_cparams(("arbitrary", "arbitrary")),
        name="nsa_sample",
    )(page_table.reshape(-1), *([slc2] * pg), q_s, gt_s, kvb_s, win2, new_s)


def _router_kernel(u_ref, w_ref, b_ref, gate_ref, pos_ref, cnt_ref, carry_ref, *, n_exp):
    i = pl.program_id(0)
    tt = u_ref.shape[0]

    @pl.when(i == 0)
    def _():
        carry_ref[...] = jnp.zeros_like(carry_ref)

    logits = _dot(u_ref[...], w_ref[...].astype(BF16))
    sc = jax.nn.sigmoid(logits.T[:n_exp])
    biased = sc + b_ref[...]
    per = n_exp // N_GROUPS
    sub = lax.broadcasted_iota(I32, (per, tt), 0)
    gs_rows = []
    for gq in range(N_GROUPS):
        x8 = biased[gq * per:(gq + 1) * per]
        m1 = jnp.max(x8, axis=0, keepdims=True)
        first = jnp.min(jnp.where(x8 == m1, sub, per), axis=0, keepdims=True)
        m2 = jnp.max(jnp.where(sub == first, -jnp.inf, x8), axis=0, keepdims=True)
        gs_rows.append(m1 + m2)
    gs = jnp.concatenate(gs_rows, axis=0)
    gi = lax.broadcasted_iota(I32, gs.shape, 0)
    gcnt = jnp.zeros(gs.shape, F32)
    for j in range(N_GROUPS):
        vj = gs[j:j + 1]
        gcnt = gcnt + jnp.where((vj > gs) | ((vj == gs) & (gi > j)), 1.0, 0.0)
    gsel = jnp.where(gcnt < float(TOPK_GROUPS), 1.0, 0.0)
    emask = jnp.concatenate([jnp.broadcast_to(gsel[gq:gq + 1], (per, tt)) for gq in range(N_GROUPS)],
                            axis=0) > 0.5
    masked = jnp.where(emask, biased, -jnp.inf)
    ei = lax.broadcasted_iota(I32, masked.shape, 0)
    ecnt = jnp.zeros(masked.shape, F32)
    for j in range(n_exp):
        vj = masked[j:j + 1]
        ecnt = ecnt + jnp.where((vj > masked) | ((vj == masked) & (ei > j)), 1.0, 0.0)
    sel = ecnt < float(TOP_K)
    g = jnp.where(sel, sc, 0.0)
    gate_ref[...] = g / jnp.sum(g, axis=0, keepdims=True) * ROUTED_SCALE
    self_ = sel.astype(BF16)
    tr = lax.broadcasted_iota(I32, (tt, tt), 0)
    tc = lax.broadcasted_iota(I32, (tt, tt), 1)
    prefix = _dot(self_, (tr < tc).astype(BF16))
    carry = carry_ref[...]
    pos_ref[...] = jnp.where(sel, (prefix + carry).astype(I32), -1)
    carry = carry + jnp.sum(sel.astype(F32), axis=1, keepdims=True)
    carry_ref[...] = carry
    cnt_ref[...] = carry.astype(I32)


def _router(u, w_router, b_router):
    m, d = u.shape
    n_exp = w_router.shape[1]
    assert n_exp <= LANE and n_exp % (8 * N_GROUPS) == 0
    tt = LANE
    w_pad = jnp.pad(w_router, ((0, 0), (0, LANE - n_exp)))
    return pl.pallas_call(
        functools.partial(_router_kernel, n_exp=n_exp),
        out_shape=(jax.ShapeDtypeStruct((n_exp, m), F32), jax.ShapeDtypeStruct((n_exp, m), I32),
                   jax.ShapeDtypeStruct((n_exp, LANE), I32)),
        grid=(m // tt,),
        in_specs=[pl.BlockSpec((tt, d), lambda i: (i, 0)),
                  pl.BlockSpec((d, LANE), lambda i: (0, 0)),
                  pl.BlockSpec((n_exp, 1), lambda i: (0, 0))],
        out_specs=(pl.BlockSpec((n_exp, tt), lambda i: (0, i)),
                   pl.BlockSpec((n_exp, tt), lambda i: (0, i)),
                   pl.BlockSpec((n_exp, LANE), lambda i: (0, 0))),
        scratch_shapes=[pltpu.VMEM((n_exp, LANE), F32)],
        compiler_params=_cparams(("arbitrary",)),
        name="moe_router",
    )(u, w_pad, b_router.reshape(n_exp, 1))


def _slots_kernel(gate_ref, pos_ref, start_ref, dest_ref, gk_ref):
    pos = pos_ref[...]
    sel = pos >= 0
    n_exp, tt = pos.shape
    dest = start_ref[...] + pos
    er = lax.broadcasted_iota(I32, (n_exp, n_exp), 0)
    ec = lax.broadcasted_iota(I32, (n_exp, n_exp), 1)
    rank = _dot((ec < er).astype(BF16), sel.astype(BF16))
    gate = gate_ref[...]
    for k in range(TOP_K):
        pick = sel & (rank == float(k))
        dest_ref[k:k + 1, :] = jnp.sum(jnp.where(pick, dest, 0), axis=0, keepdims=True)
        gk_ref[k:k + 1, :] = jnp.sum(jnp.where(pick, gate, 0.0), axis=0, keepdims=True)


def _slots(gate_t, pos_t, start):
    n_exp, m = pos_t.shape
    tt = LANE
    return pl.pallas_call(
        _slots_kernel,
        out_shape=(jax.ShapeDtypeStruct((TOP_K, m), I32), jax.ShapeDtypeStruct((TOP_K, m), F32)),
        grid=(m // tt,),
        in_specs=[pl.BlockSpec((n_exp, tt), lambda i: (0, i)),
                  pl.BlockSpec((n_exp, tt), lambda i: (0, i)),
                  pl.BlockSpec((n_exp, 1), lambda i: (0, 0))],
        out_specs=(pl.BlockSpec((TOP_K, tt), lambda i: (0, i)),
                   pl.BlockSpec((TOP_K, tt), lambda i: (0, i))),
        compiler_params=_cparams(("arbitrary",)),
        name="moe_slots",
    )(gate_t, pos_t, start.reshape(n_exp, 1))


def _dispatch_kernel(meta_ref, tok_ref, up_hbm, xs_ref, sem, *, rps):
    i = pl.program_id(0)
    used = i * rps < meta_ref[0] * MOE_ROWS

    @pl.when(used)
    def _():
        for a in range(rps // LANE):
            def start(b, c, a=a):
                pltpu.make_async_copy(up_hbm.at[tok_ref[a, b]], xs_ref.at[a * LANE + b], sem).start()
                return c

            lax.fori_loop(0, LANE, start, 0, unroll=8)
        pltpu.make_async_copy(up_hbm.at[pl.ds(0, rps)], xs_ref, sem).wait()

    @pl.when(jnp.logical_not(used))
    def _():
        xs_ref[...] = jnp.zeros_like(xs_ref)


def _dispatch(meta, row_tok3, up3):
    m, s2, _ = up3.shape
    n_steps, sub, _ = row_tok3.shape
    rps = sub * LANE
    assert m >= rps
    return pl.pallas_call(
        functools.partial(_dispatch_kernel, rps=rps),
        out_shape=jax.ShapeDtypeStruct((n_steps * rps, s2, LANE), U32),
        grid_spec=pltpu.PrefetchScalarGridSpec(
            num_scalar_prefetch=1,
            grid=(n_steps,),
            in_specs=[pl.BlockSpec((None, sub, LANE), lambda i, meta: (i, 0, 0), memory_space=pltpu.SMEM),
                      pl.BlockSpec(memory_space=pl.ANY)],
            out_specs=pl.BlockSpec((rps, s2, LANE), lambda i, meta: (i, 0, 0)),
            scratch_shapes=[pltpu.SemaphoreType.DMA(())]),
        compiler_params=_cparams(("arbitrary",)),
        name="moe_dispatch",
    )(meta, row_tok3, up3)


def _unpack_rows(x_ref, rb, s2):
    cols = []
    for s in range(s2):
        w = x_ref[pl.ds(s, rb, stride=s2), :]
        cols.append(pltpu.bitcast(w << 16, F32))
        cols.append(pltpu.bitcast(w & jnp.uint32(0xFFFF0000), F32))
    return jnp.concatenate(cols, axis=1).astype(BF16)


def _blk(i, meta_ref):
    return jnp.minimum(i, meta_ref[0] - 1)


def _expert_up_kernel(meta_ref, be_ref, x_ref, w1_ref, w3_ref, h_ref, wb1_ref, wb3_ref, *, rb, s2):
    i = pl.program_id(1)
    blk = _blk(i, meta_ref)
    changed = (i == 0) | (be_ref[blk] != be_ref[jnp.maximum(blk - 1, 0)])

    @pl.when(changed)
    def _():
        wb1_ref[...] = w1_ref[...].astype(BF16)
        wb3_ref[...] = w3_ref[...].astype(BF16)

    @pl.when(i < meta_ref[0])
    def _():
        x = _unpack_rows(x_ref, rb, s2)
        h_ref[...] = (_silu(_dot(x, wb1_ref[...])) * _dot(x, wb3_ref[...])).astype(BF16)

    @pl.when(i >= meta_ref[0])
    def _():
        h_ref[...] = jnp.zeros_like(h_ref)


def _expert_up(meta, blk_e, xs2, w1, w3, n_blk, rb, s2):
    n_exp, d, hid = w1.shape
    th = min(hid, 512)
    wspec = pl.BlockSpec((None, d, th), lambda hh, i, meta, be: (be[_blk(i, meta)], 0, hh))
    return pl.pallas_call(
        functools.partial(_expert_up_kernel, rb=rb, s2=s2),
        out_shape=jax.ShapeDtypeStruct((n_blk * rb, hid), BF16),
        grid_spec=pltpu.PrefetchScalarGridSpec(
            num_scalar_prefetch=2,
            grid=(hid // th, n_blk),
            in_specs=[pl.BlockSpec((rb * s2, LANE), lambda hh, i, meta, be: (_blk(i, meta), 0)),
                      wspec, wspec],
            out_specs=pl.BlockSpec((rb, th), lambda hh, i, meta, be: (i, hh)),
            scratch_shapes=[pltpu.VMEM((d, th), BF16), pltpu.VMEM((d, th), BF16)]),
        compiler_params=_cparams(("arbitrary", "arbitrary")),
        name="moe_expert_up",
    )(meta, blk_e, xs2, w1, w3)


def _row_pitch(d):
    return d // LANE + 8


def _expert_down_kernel(meta_ref, be_ref, h_ref, w2_ref, y_ref, *, rb, tn, pitch):
    i = pl.program_id(0)

    @pl.when(i < meta_ref[0])
    def _():
        h = h_ref[...]
        d = w2_ref.shape[1]
        for c in range(d // tn):
            y = _dot(h, w2_ref[:, c * tn:(c + 1) * tn].astype(BF16))
            for s in range(tn // LANE):
                y_ref[pl.ds(c * (tn // LANE) + s, rb, stride=pitch), :] = y[:, s * LANE:(s + 1) * LANE]
        for s in range(d // LANE, pitch):
            y_ref[pl.ds(s, rb, stride=pitch), :] = jnp.zeros((rb, LANE), F32)

    @pl.when(i >= meta_ref[0])
    def _():
        y_ref[...] = jnp.zeros_like(y_ref)


def _expert_down(meta, blk_e, h, w2, n_blk, rb):
    n_exp, hid, d = w2.shape
    pitch = _row_pitch(d)
    return pl.pallas_call(
        functools.partial(_expert_down_kernel, rb=rb, tn=min(d, 512), pitch=pitch),
        out_shape=jax.ShapeDtypeStruct((n_blk * rb * pitch, LANE), F32),
        grid_spec=pltpu.PrefetchScalarGridSpec(
            num_scalar_prefetch=2,
            grid=(n_blk,),
            in_specs=[pl.BlockSpec((rb, hid), lambda i, meta, be: (_blk(i, meta), 0)),
                      pl.BlockSpec((None, hid, d), lambda i, meta, be: (be[_blk(i, meta)], 0, 0))],
            out_specs=pl.BlockSpec((rb * pitch, LANE), lambda i, meta, be: (i, 0))),
        compiler_params=_cparams(("arbitrary",)),
        name="moe_expert_down",
    )(meta, blk_e, h, w2)


def _final_kernel(dest_ref, destn_ref, gk_ref, y_hbm, x1_ref, ysh_ref, g2_ref, lg_ref, lb_ref, o_ref,
                  buf_ref, r_ref, sem, *, tf, alpha, pitch):
    i = pl.program_id(0)
    s_all = x1_ref.shape[1] // LANE
    per_blk = ROW_TILE // tf
    slot = lax.rem(i, 2)

    def issue(d_ref, step, sl):
        off = lax.rem(step, per_blk) * tf

        def body(t, c):
            for k in range(TOP_K):
                pltpu.make_async_copy(y_hbm.at[d_ref[k, off + t], pl.ds(0, s_all)], buf_ref.at[sl, k, t],
                                      sem.at[sl]).start()
            return c

        lax.fori_loop(0, tf, body, 0, unroll=2)

    @pl.when(i == 0)
    def _():
        issue(dest_ref, i, slot)

    @pl.when(i + 1 < pl.num_programs(0))
    def _():
        issue(destn_ref, i + 1, 1 - slot)

    for k in range(TOP_K):
        pltpu.make_async_copy(y_hbm.at[pl.ds(0, tf), pl.ds(0, s_all)], buf_ref.at[slot, k],
                              sem.at[slot]).wait()

    off = lax.rem(i, per_blk) * tf

    def token(t, c):
        acc = gk_ref[0, off + t] * buf_ref[slot, 0, t]
        for k in range(1, TOP_K):
            acc = acc + gk_ref[k, off + t] * buf_ref[slot, k, t]
        r_ref[pl.ds(pl.multiple_of(t * pitch, 8), s_all), :] = acc
        return c

    lax.fori_loop(0, tf, token, 0, unroll=2)
    routed = jnp.concatenate([r_ref[pl.ds(s, tf, stride=pitch), :] for s in range(s_all)], axis=1)
    v = alpha * x1_ref[...] + g2_ref[...] * (routed + ysh_ref[...])
    o_ref[...] = _ln_rows(v) * lg_ref[...] + lb_ref[...]


def _final(dest3, gk3, y3, x1, ysh, mod_tiles, mod_index, tf, ln_g, ln_b, alpha):
    m, d = x1.shape
    n_rows, pitch, _ = y3.shape
    s_all = d // LANE
    per_blk = ROW_TILE // tf
    n_steps = m // tf
    assert dest3.shape[2] == ROW_TILE and n_rows >= tf
    row = pl.BlockSpec((tf, d), lambda i: (i, 0))
    vec = pl.BlockSpec((1, d), lambda i: (0, 0))
    smem_blk = lambda f: pl.BlockSpec((None, TOP_K, ROW_TILE), lambda i: (f(i) // per_blk, 0, 0),
                                      memory_space=pltpu.SMEM)
    return pl.pallas_call(
        functools.partial(_final_kernel, tf=tf, alpha=alpha, pitch=pitch),
        out_shape=jax.ShapeDtypeStruct((m, d), F32),
        grid=(n_steps,),
        in_specs=[smem_blk(lambda i: i), smem_blk(lambda i: jnp.minimum(i + 1, n_steps - 1)),
                  smem_blk(lambda i: i),
                  pl.BlockSpec(memory_space=pl.ANY),
                  row, row, _mod_spec(5, d, tf, mod_index), vec, vec],
        out_specs=row,
        scratch_shapes=[pltpu.VMEM((2, TOP_K, tf, s_all, LANE), F32),
                        pltpu.VMEM((tf * pitch, LANE), F32),
                        pltpu.SemaphoreType.DMA((2,))],
        compiler_params=_cparams(("arbitrary",)),
        name="moe_combine_ln2",
    )(dest3, dest3, gk3, y3, x1, ysh, mod_tiles, ln_g.reshape(1, d), ln_b.reshape(1, d))


def _rope_tables(pos):
    half = HD // 2
    inv = jnp.power(ROPE_THETA, -jnp.arange(half, dtype=F32) / half)
    ang = pos.astype(F32)[:, None] * inv[None, :]
    cos, sin = jnp.cos(ang), jnp.sin(ang)
    return jnp.concatenate([cos, cos], axis=1), jnp.concatenate([-sin, sin], axis=1)


def kernel(x_prompt, x_sample, c_prompt, c_sample, cache_cmp, cache_slc, cache_win, state_gla, page_table, w_ada, b_ada, w_in, w_gla_a2, b_gla_a2, gla_norm_g, cmp_pe_k, cmp_pe_v, cmp_w_k, cmp_w_v, w_br_gla, w_br_nsa, w_out, ln1_g, ln1_b, w_router, b_router, w_exp1, w_exp3, w_exp2, w_sh1, w_sh3, w_sh2, ln2_g, ln2_b):
    depth = w_in.shape[0]
    assert depth == 1
    alpha = (2.0 * depth) ** 0.25
    nb_, s_, d = x_prompt.shape
    db_, t_, _ = x_sample.shape
    mp, ns = nb_ * s_, db_ * t_
    m = mp + ns
    assert ns % ROW_TILE == 0 and s_ % ROW_TILE == 0 and d % 256 == 0
    n_pool, page = cache_cmp.shape[1], cache_cmp.shape[2]
    n_pages = page_table.shape[1]
    past_len = n_pages * page
    kvw2 = 2 * NSA_G * HD
    src, order, dst, n_used, _ = _col_plan(d)
    tn = 512
    small_off = dst['gla_a']
    dst = dict(dst)
    dst['nsa_g'] = small_off + GLA_RANK
    n_p = -(-(small_off + LANE) // tn) * tn

    rc = -(-(nb_ + db_) // 8) * 8
    c_all = jnp.pad(jnp.concatenate([c_prompt, c_sample], axis=0), ((0, rc - nb_ - db_), (0, 0)))
    mod = _ada(c_all, w_ada[0], b_ada[0])
    mod_tiles = jnp.concatenate(
        [jnp.broadcast_to(mod[:nb_, None, :], (nb_, ROW_TILE, 6 * d)),
         jnp.repeat(mod[nb_:nb_ + db_], t_, axis=0).reshape(ns // ROW_TILE, ROW_TILE, 6 * d)], axis=0)
    mod_index = _mod_index(ROW_TILE, mp, s_, nb_)

    x_all = jnp.concatenate([x_prompt.reshape(mp, d), x_sample.reshape(ns, d)], axis=0)
    u1 = _ln_mod(x_all, mod_tiles, mod_index)

    pieces = [w_in[0][:, src[n][0]:src[n][0] + src[n][1]] for n in order]
    w_p = jnp.concatenate(pieces + [jnp.zeros((d, n_p - n_used), F32)], axis=1)
    pos_all = jnp.concatenate([jnp.tile(jnp.arange(s_, dtype=I32), nb_),
                               jnp.tile(past_len + jnp.arange(t_, dtype=I32), db_)])
    cos_t, sin_t = _rope_tables(pos_all)
    tm = _divisor_tile(m, 1056, 16)
    rope_cols = ((dst['nsa_q'], dst['nsa_q'] + NSA_HEADS * HD),
                 (dst['nsa_ks'], dst['nsa_ks'] + NSA_G * HD),
                 (dst['nsa_kw'], dst['nsa_kw'] + NSA_G * HD))
    parts = _inproj(u1, w_p, cos_t, sin_t, rope_cols, tm, tn)

    zeros_state = jnp.zeros((nb_, GLA_HEADS, GLA_DK, GLA_DV), F32)
    og_p, gla_p = _gla(parts, nb_, s_, s_, 512, 128, dst, zeros_state, w_gla_a2[0], b_gla_a2[0],
                       gla_norm_g[0])
    t_pad = 16
    gla_cols = dst['nsa_q']
    smp = parts[mp:].reshape(db_, t_, n_p)
    smp_pad = jnp.pad(smp, ((0, 0), (0, t_pad - t_), (0, 0))).reshape(db_ * t_pad, n_p)
    og_s, gla_s = _gla(smp_pad, db_, t_pad, t_, t_pad, t_pad, dst, state_gla[0], w_gla_a2[0],
                       b_gla_a2[0], gla_norm_g[0])
    og_s = og_s.reshape(db_, t_pad, -1)[:, :t_].reshape(ns, -1)
    og = jnp.concatenate([og_p, og_s], axis=0)

    cmp_pos_p = jnp.tile((jnp.arange(s_ // CMP_BLOCK, dtype=I32) + 1) * CMP_BLOCK - 1, nb_)
    cos_c, sin_c = _rope_tables(cmp_pos_p)
    kvb_p = _compress_prompt(parts, mp, dst, cmp_w_k[0], cmp_w_v[0], cmp_pe_k[0], cmp_pe_v[0], cos_c, sin_c)
    on_p = _nsa_prompt(parts, kvb_p, nb_, s_, dst)

    cmp_pos_s = (jnp.arange(past_len // CMP_BLOCK, dtype=I32) + 1) * CMP_BLOCK - 1
    cos_cs, sin_cs = _rope_tables(cmp_pos_s)
    is_key = (jnp.arange(KV_CH) < NSA_G)[None, :, None]
    cos_e = jnp.where(is_key, cos_cs[:, None, :], 1.0).reshape(-1, HD)
    sin_e = jnp.where(is_key, sin_cs[:, None, :], 0.0).reshape(-1, HD)
    kvb_s = _compress_pages(cache_cmp.reshape(-1, HD), page, page_table, cmp_w_k[0], cmp_w_v[0],
                            cmp_pe_k[0], cmp_pe_v[0], cos_e, sin_e)
    tq_pad = 8
    pad_t = lambda a: jnp.pad(a, ((0, 0), (0, tq_pad - t_), (0, 0)))
    q_s = pad_t(smp[:, :, dst['nsa_q']:dst['nsa_q'] + NSA_HEADS * HD])
    gt_s = pad_t(smp[:, :, small_off:small_off + LANE])
    new_s = pad_t(smp[:, :, dst['nsa_ks']:dst['nsa_ks'] + 2 * kvw2])
    on_s = _nsa_sample(q_s, gt_s, kvb_s, cache_slc.reshape(-1, HD), page, cache_win.reshape(-1, HD),
                       cache_win.shape[2], new_s, page_table, t_)
    on = jnp.concatenate([on_p, on_s[:, :t_].reshape(ns, -1).astype(BF16)], axis=0)

    z = _merge(og, on, w_br_gla[0], w_br_nsa[0], parts, dst['mg_gla'], dst['mg_nsa'], tm, tn)
    y = _plain(z, w_out[0], tm, tn, "out_proj")
    x1, u2, up = _mid(x_all, y, mod_tiles, mod_index, ln1_g[0], ln1_b[0], alpha)

    n_exp = w_router.shape[2]
    rb = MOE_ROWS
    s2 = d // 256
    gate_t, pos_t, cnt = _router(u2, w_router[0], b_router[0])
    counts = cnt[:, 0]
    padded = (counts + rb - 1) // rb * rb
    pend = jnp.cumsum(padded)
    start = pend - padded
    n_blk = -(-(m * TOP_K + n_exp * (rb - 1)) // rb)
    rps = 2 * rb
    n_blk = -(-(m * TOP_K + n_exp * (rb - 1)) // rps) * (rps // rb)
    blk_first = jnp.arange(n_blk, dtype=I32) * rb
    blk_e = jnp.minimum(jnp.sum(pend[None, :] <= blk_first[:, None], axis=1), n_exp - 1).astype(I32)
    meta = (pend[-1:] // rb).astype(I32)
    dest, gk = _slots(gate_t, pos_t, start.astype(I32))
    nt = m // ROW_TILE
    as_blocks = lambda a: a.reshape(TOP_K, nt, ROW_TILE).transpose(1, 0, 2)
    dest3, gk3 = as_blocks(dest), as_blocks(gk)
    row_tok = jnp.zeros((n_blk * rb,), I32).at[dest.reshape(-1)].set(
        jnp.tile(jnp.arange(m, dtype=I32), TOP_K))
    xs3 = _dispatch(meta, row_tok.reshape(-1, rps // LANE, LANE), up.reshape(m, s2, LANE))
    h = _expert_up(meta, blk_e, xs3.reshape(-1, LANE), w_exp1[0], w_exp3[0], n_blk, rb, s2)
    y2 = _expert_down(meta, blk_e, h, w_exp2[0], n_blk, rb)
    hs = _swiglu(u2, w_sh1[0], w_sh3[0], tm, tn // 2)
    ysh = _plain(hs, w_sh2[0], tm, tn, "shared_down")
    tf = 64
    out = _final(dest3, gk3, y2.reshape(n_blk * rb, -1, LANE), x1, ysh, mod_tiles,
                 _mod_index(tf, mp, s_, nb_), tf, ln2_g[0], ln2_b[0], alpha)

    def rows(lo, n_rows, name, lead):
        return parts[lo:lo + n_rows, dst[name]:dst[name] + kvw2].reshape(lead + (2, NSA_G, HD))

    win_buf = cache_win.shape[2]
    win_p = rows(0, mp, 'nsa_kw', (nb_, s_))[:, s_ - win_buf:]
    win_s = jnp.concatenate([cache_win[0], rows(mp, ns, 'nsa_kw', (db_, t_))], axis=1)[:, t_:]
    return (out[:mp].reshape(nb_, s_, d), out[mp:].reshape(db_, t_, d),
            rows(0, mp, 'nsa_kc', (nb_, s_))[None], rows(mp, ns, 'nsa_kc', (db_, t_))[None],
            rows(0, mp, 'nsa_ks', (nb_, s_))[None], rows(mp, ns, 'nsa_ks', (db_, t_))[None],
            win_p[None], win_s[None], gla_p[None], gla_s[None])
```

```python
import functools
import math

import jax
import jax.numpy as jnp
import numpy as np
from jax import lax
from jax.experimental import pallas as pl
from jax.experimental.pallas import tpu as pltpu

F32, BF16, I32, U32 = jnp.float32, jnp.bfloat16, jnp.int32, jnp.uint32
HIGHEST = lax.Precision.HIGHEST

GLA_HEADS, GLA_DK, GLA_DV, GLA_RANK, GLA_GATE_NORM = 8, 128, 256, 16, 16.0
NSA_HEADS, NSA_G, NSA_R, HD = 16, 4, 4, 128
KV_CH = 2 * NSA_G
CMP_BLOCK, SEL_BLOCK, N_SEL, WINDOW = 32, 64, 16, 512
ROPE_THETA = 10000.0
TOP_K, N_GROUPS, TOPK_GROUPS, ROUTED_SCALE = 8, 8, 4, 2.5
LN_EPS = 1e-5
NEG = -1e30

LANE = 128
VMEM_LIMIT = 56 * 1024 * 1024
ROW_TILE = 128
MOE_ROWS = 512
GLA_SUB = 32


def _cparams(sem):
    return pltpu.CompilerParams(dimension_semantics=sem, vmem_limit_bytes=VMEM_LIMIT)


def _divisor_tile(n, target, mult):
    best = None
    for d in range(mult, min(n, target) + 1, mult):
        if n % d == 0:
            best = d
    assert best is not None, (n, target, mult)
    return best


def _silu(x):
    return x * jax.nn.sigmoid(x)


def _ln_rows(x):
    mu = jnp.mean(x, axis=-1, keepdims=True)
    xc = x - mu
    var = jnp.mean(xc * xc, axis=-1, keepdims=True)
    return xc * lax.rsqrt(var + LN_EPS)


def _dot(a, b):
    return jnp.dot(a, b, preferred_element_type=F32)


def _dot_nt(a, b, precision=None):
    return lax.dot_general(a, b, (((1,), (1,)), ((), ())), precision=precision,
                           preferred_element_type=F32)


def _dot_tn(a, b, precision=None):
    return lax.dot_general(a, b, (((0,), (0,)), ((), ())), precision=precision,
                           preferred_element_type=F32)


def _col_plan(d_model):
    gqk, gv = GLA_HEADS * GLA_DK, GLA_HEADS * GLA_DV
    nq, nkv = NSA_HEADS * HD, NSA_G * HD
    ref = (('gla_q', gqk), ('gla_k', gqk), ('gla_v', gv), ('gla_r', gv), ('gla_a', GLA_RANK),
           ('nsa_q', nq), ('nsa_kc', nkv), ('nsa_vc', nkv), ('nsa_ks', nkv), ('nsa_vs', nkv),
           ('nsa_kw', nkv), ('nsa_vw', nkv), ('nsa_g', NSA_HEADS * 3), ('mg_gla', d_model),
           ('mg_nsa', d_model))
    src, o = {}, 0
    for n, w in ref:
        src[n] = (o, w)
        o += w
    order = ('gla_q', 'gla_k', 'gla_v', 'gla_r', 'nsa_q', 'nsa_kc', 'nsa_vc', 'nsa_ks', 'nsa_vs',
             'nsa_kw', 'nsa_vw', 'mg_gla', 'mg_nsa', 'gla_a', 'nsa_g')
    dst, o = {}, 0
    for n in order:
        dst[n] = o
        o += src[n][1]
    return src, order, dst, o, o


def _ada_kernel(c_ref, w_ref, b_ref, o_ref):
    a = _silu(c_ref[...]).astype(BF16)
    o_ref[...] = _dot(a, w_ref[...].astype(BF16)) + b_ref[...]


def _ada(c_all, w_ada, b_ada):
    rc, d = c_all.shape
    n = w_ada.shape[1]
    tn = 512
    return pl.pallas_call(
        _ada_kernel,
        out_shape=jax.ShapeDtypeStruct((rc, n), F32),
        grid=(n // tn,),
        in_specs=[pl.BlockSpec((rc, d), lambda j: (0, 0)),
                  pl.BlockSpec((d, tn), lambda j: (0, j)),
                  pl.BlockSpec((1, tn), lambda j: (0, j))],
        out_specs=pl.BlockSpec((rc, tn), lambda j: (0, j)),
        compiler_params=_cparams(("arbitrary",)),
        name="ada_mod",
    )(c_all, w_ada, b_ada.reshape(1, n))


def _mod_index(rows, mp, seq, nseq):
    n_pt, per_seq, per_grp = mp // rows, seq // rows, ROW_TILE // rows

    def index(i):
        j = jnp.maximum(i - n_pt, 0)
        return (jnp.where(i < n_pt, i // per_seq, nseq + j // per_grp),
                jnp.where(i < n_pt, 0, j % per_grp))

    return index


def _mod_spec(k, d, rows, mod_index):
    return pl.BlockSpec((None, rows, d), lambda i: (*mod_index(i), k))


def _ln_mod_kernel(x_ref, sh_ref, sc_ref, u_ref):
    y = _ln_rows(x_ref[...])
    u_ref[...] = (y * (1.0 + sc_ref[...]) + sh_ref[...]).astype(BF16)


def _ln_mod(x_all, mod_tiles, mod_index):
    m, d = x_all.shape
    return pl.pallas_call(
        _ln_mod_kernel,
        out_shape=jax.ShapeDtypeStruct((m, d), BF16),
        grid=(m // ROW_TILE,),
        in_specs=[pl.BlockSpec((ROW_TILE, d), lambda i: (i, 0)),
                  _mod_spec(0, d, ROW_TILE, mod_index), _mod_spec(1, d, ROW_TILE, mod_index)],
        out_specs=pl.BlockSpec((ROW_TILE, d), lambda i: (i, 0)),
        compiler_params=_cparams(("arbitrary",)),
        name="ln_mod1",
    )(x_all, mod_tiles, mod_tiles)


def _pack_pairs(u):
    bits = pltpu.bitcast(u, U32)
    out = []
    for s in range(u.shape[1] // 256):
        lo = bits[:, s * 256:s * 256 + 128]
        hi = bits[:, s * 256 + 128:s * 256 + 256]
        out.append((lo >> 16) | (hi & jnp.uint32(0xFFFF0000)))
    return out


def _mid_kernel(x_ref, y_ref, g1_ref, sh_ref, sc_ref, lg_ref, lb_ref, x1_ref, u_ref, up_ref, *, alpha):
    v = alpha * x_ref[...] + g1_ref[...] * y_ref[...]
    x1 = _ln_rows(v) * lg_ref[...] + lb_ref[...]
    x1_ref[...] = x1
    u = _ln_rows(x1) * (1.0 + sc_ref[...]) + sh_ref[...]
    ub = u.astype(BF16)
    u_ref[...] = ub
    words = _pack_pairs(ub.astype(F32))
    s2 = len(words)
    for s, w in enumerate(words):
        up_ref[pl.ds(s, ROW_TILE, stride=s2), :] = w


def _mid(x_all, y, mod_tiles, mod_index, ln_g, ln_b, alpha):
    m, d = x_all.shape
    s2 = d // 256
    row = pl.BlockSpec((ROW_TILE, d), lambda i: (i, 0))
    vec = pl.BlockSpec((1, d), lambda i: (0, 0))
    return pl.pallas_call(
        functools.partial(_mid_kernel, alpha=alpha),
        out_shape=(jax.ShapeDtypeStruct((m, d), F32), jax.ShapeDtypeStruct((m, d), BF16),
                   jax.ShapeDtypeStruct((m * s2, LANE), U32)),
        grid=(m // ROW_TILE,),
        in_specs=[row, row, _mod_spec(2, d, ROW_TILE, mod_index), _mod_spec(3, d, ROW_TILE, mod_index),
                  _mod_spec(4, d, ROW_TILE, mod_index), vec, vec],
        out_specs=(row, row, pl.BlockSpec((ROW_TILE * s2, LANE), lambda i: (i, 0))),
        compiler_params=_cparams(("arbitrary",)),
        name="ln1_mod2",
    )(x_all, y, mod_tiles, mod_tiles, mod_tiles, ln_g.reshape(1, d), ln_b.reshape(1, d))


def _cast_weights(i, pairs):
    @pl.when(i == 0)
    def _():
        for w_ref, wb_ref in pairs:
            wb_ref[...] = w_ref[...].astype(BF16)


def _inproj_kernel(u_ref, w_ref, cos_ref, sin_ref, o_ref, wb_ref, *, tn, rope_patterns):
    j, i = pl.program_id(0), pl.program_id(1)
    _cast_weights(i, ((w_ref, wb_ref),))
    acc = _dot(u_ref[...], wb_ref[...])
    plain = True
    for flags, tiles in rope_patterns:
        here = functools.reduce(jnp.logical_or, [j == t for t in tiles])
        plain = jnp.logical_and(plain, jnp.logical_not(here))

        @pl.when(here)
        def _(flags=flags):
            cos, sin = cos_ref[...], sin_ref[...]
            for h, rotate in enumerate(flags):
                a = acc[:, h * HD:(h + 1) * HD]
                o_ref[:, h * HD:(h + 1) * HD] = (a * cos + pltpu.roll(a, HD // 2, axis=1) * sin) if rotate else a

    @pl.when(plain)
    def _():
        o_ref[...] = acc


def _inproj(u, w_p, cos_t, sin_t, rope_cols, tm, tn):
    m, d = u.shape
    n = w_p.shape[1]
    by_flags = {}
    for t in range(n // tn):
        flags = tuple(any(lo <= t * tn + h * HD < hi for lo, hi in rope_cols) for h in range(tn // HD))
        if any(flags):
            by_flags.setdefault(flags, []).append(t)
    rope_patterns = tuple((f, tuple(ts)) for f, ts in by_flags.items())
    return pl.pallas_call(
        functools.partial(_inproj_kernel, tn=tn, rope_patterns=rope_patterns),
        out_shape=jax.ShapeDtypeStruct((m, n), F32),
        grid=(n // tn, m // tm),
        in_specs=[pl.BlockSpec((tm, d), lambda j, i: (i, 0)),
                  pl.BlockSpec((d, tn), lambda j, i: (0, j), pipeline_mode=pl.Buffered(1)),
                  pl.BlockSpec((tm, HD), lambda j, i: (i, 0)),
                  pl.BlockSpec((tm, HD), lambda j, i: (i, 0))],
        out_specs=pl.BlockSpec((tm, tn), lambda j, i: (i, j)),
        scratch_shapes=[pltpu.VMEM((d, tn), BF16)],
        compiler_params=_cparams(("arbitrary", "arbitrary")),
        name="in_proj",
    )(u, w_p, cos_t, sin_t)


def _merge_kernel(a1_ref, a2_ref, w1_ref, w2_ref, g1_ref, g2_ref, o_ref, wb1_ref, wb2_ref):
    _cast_weights(pl.program_id(1), ((w1_ref, wb1_ref), (w2_ref, wb2_ref)))
    z = (jax.nn.sigmoid(g1_ref[...]) * _dot(a1_ref[...], wb1_ref[...])
         + jax.nn.sigmoid(g2_ref[...]) * _dot(a2_ref[...], wb2_ref[...]))
    o_ref[...] = z.astype(BF16)


def _merge(og, on, w1, w2, parts, off1, off2, tm, tn):
    m, k = og.shape
    n = w1.shape[1]
    return pl.pallas_call(
        _merge_kernel,
        out_shape=jax.ShapeDtypeStruct((m, n), BF16),
        grid=(n // tn, m // tm),
        in_specs=[pl.BlockSpec((tm, k), lambda j, i: (i, 0)),
                  pl.BlockSpec((tm, k), lambda j, i: (i, 0)),
                  pl.BlockSpec((k, tn), lambda j, i: (0, j)),
                  pl.BlockSpec((k, tn), lambda j, i: (0, j)),
                  pl.BlockSpec((tm, tn), lambda j, i: (i, off1 // tn + j)),
                  pl.BlockSpec((tm, tn), lambda j, i: (i, off2 // tn + j))],
        out_specs=pl.BlockSpec((tm, tn), lambda j, i: (i, j)),
        scratch_shapes=[pltpu.VMEM((k, tn), BF16), pltpu.VMEM((k, tn), BF16)],
        compiler_params=_cparams(("arbitrary", "arbitrary")),
        name="branch_merge",
    )(og, on, w1, w2, parts, parts)


def _plain_kernel(a_ref, w_ref, o_ref, wb_ref):
    _cast_weights(pl.program_id(1), ((w_ref, wb_ref),))
    o_ref[...] = _dot(a_ref[...], wb_ref[...])


def _plain(a, w, tm, tn, name):
    m, k = a.shape
    n = w.shape[1]
    return pl.pallas_call(
        _plain_kernel,
        out_shape=jax.ShapeDtypeStruct((m, n), F32),
        grid=(n // tn, m // tm),
        in_specs=[pl.BlockSpec((tm, k), lambda j, i: (i, 0)),
                  pl.BlockSpec((k, tn), lambda j, i: (0, j))],
        out_specs=pl.BlockSpec((tm, tn), lambda j, i: (i, j)),
        scratch_shapes=[pltpu.VMEM((k, tn), BF16)],
        compiler_params=_cparams(("arbitrary", "arbitrary")),
        name=name,
    )(a, w)


def _swiglu_kernel(a_ref, w1_ref, w3_ref, o_ref, wb1_ref, wb3_ref):
    _cast_weights(pl.program_id(1), ((w1_ref, wb1_ref), (w3_ref, wb3_ref)))
    a = a_ref[...]
    o_ref[...] = (_silu(_dot(a, wb1_ref[...])) * _dot(a, wb3_ref[...])).astype(BF16)


def _swiglu(a, w1, w3, tm, tn):
    m, k = a.shape
    n = w1.shape[1]
    return pl.pallas_call(
        _swiglu_kernel,
        out_shape=jax.ShapeDtypeStruct((m, n), BF16),
        grid=(n // tn, m // tm),
        in_specs=[pl.BlockSpec((tm, k), lambda j, i: (i, 0)),
                  pl.BlockSpec((k, tn), lambda j, i: (0, j)),
                  pl.BlockSpec((k, tn), lambda j, i: (0, j))],
        out_specs=pl.BlockSpec((tm, tn), lambda j, i: (i, j)),
        scratch_shapes=[pltpu.VMEM((k, tn), BF16), pltpu.VMEM((k, tn), BF16)],
        compiler_params=_cparams(("arbitrary", "arbitrary")),
        name="shared_swiglu",
    )(a, w1, w3)


def _gla_kernel(q_ref, k_ref, v_ref, r_ref, a_ref, wa_ref, ba_ref, ng_ref, s0_ref,
                og_ref, sf_ref, st_ref, *, tb, chunk, s_valid):
    i = pl.program_id(2)

    @pl.when(i == 0)
    def _():
        st_ref[...] = s0_ref[...]

    x = jnp.dot(a_ref[:, :GLA_RANK], wa_ref[...], precision=HIGHEST,
                preferred_element_type=F32) + ba_ref[...]
    glog = (jnp.minimum(x, 0.0) - jnp.log1p(jnp.exp(-jnp.abs(x)))) * (1.0 / GLA_GATE_NORM)
    row = i * tb + lax.broadcasted_iota(I32, (tb, 1), 0)
    glog = jnp.where(row < s_valid, glog, 0.0)

    sub = min(GLA_SUB, chunk)
    rr = lax.broadcasted_iota(I32, (chunk, chunk), 0)
    cc = lax.broadcasted_iota(I32, (chunk, chunk), 1)
    tri = (rr >= cc).astype(F32)
    ones = jnp.ones((chunk, GLA_DK), F32)
    ng = ng_ref[...]

    for c in range(tb // chunk):
        lo = c * chunk
        g = glog[lo:lo + chunk]
        bc = jnp.dot(tri, g, precision=HIGHEST, preferred_element_type=F32)
        bl = bc[chunk - 1:chunk]
        q = q_ref[lo:lo + chunk, :] * (GLA_DK ** -0.5)
        k = k_ref[lo:lo + chunk, :]
        vb = v_ref[lo:lo + chunk, :].astype(BF16)
        qg = (q * jnp.exp(bc)).astype(BF16)
        kd = (k * jnp.exp(bl - bc)).astype(BF16)
        state = st_ref[...]
        o_inter = _dot(qg, state.astype(BF16))
        outs = []
        for sb in range(chunk // sub):
            r0, r1 = sb * sub, (sb + 1) * sub
            base = bc[r0 - 1:r0] if sb > 0 else jnp.zeros((1, GLA_DK), F32)
            qs = (q[r0:r1] * jnp.exp(bc[r0:r1] - base)).astype(BF16)
            ks = (k[:r1] * jnp.exp(base - bc[:r1])).astype(BF16)
            att = _dot_nt(qs, ks)
            causal = (r0 + lax.broadcasted_iota(I32, (sub, r1), 0)) >= lax.broadcasted_iota(I32, (sub, r1), 1)
            att = jnp.where(causal, att, 0.0)
            outs.append(_dot(att.astype(BF16), vb[:r1]))
        o = jnp.concatenate(outs, axis=0) + o_inter
        dec = jnp.exp(_dot_tn(g, ones, precision=HIGHEST))
        dec = jnp.concatenate([dec] * (GLA_DV // GLA_DK), axis=1)
        st_ref[...] = dec * state + _dot_tn(kd, vb)
        o = o * lax.rsqrt(jnp.mean(o * o, axis=-1, keepdims=True) + LN_EPS) * ng
        og_ref[lo:lo + chunk, :] = (o * _silu(r_ref[lo:lo + chunk, :])).astype(BF16)

    @pl.when(i == pl.num_programs(2) - 1)
    def _():
        sf_ref[...] = st_ref[...]


def _gla(src, nseq, s_pad, s_valid, tb, chunk, dst, s0, w_a2, b_a2, norm_g):
    nb = s_pad // tb
    rows = nseq * s_pad
    qo, ko = dst['gla_q'] // GLA_DK, dst['gla_k'] // GLA_DK
    vo, ro = dst['gla_v'] // GLA_DV, dst['gla_r'] // GLA_DV
    ao = dst['gla_a'] // LANE
    rowblk = lambda b, h, i: b * nb + i
    st_spec = pl.BlockSpec((None, None, GLA_DK, GLA_DV), lambda b, h, i: (b, h, 0, 0))
    return pl.pallas_call(
        functools.partial(_gla_kernel, tb=tb, chunk=chunk, s_valid=s_valid),
        out_shape=(jax.ShapeDtypeStruct((rows, GLA_HEADS * GLA_DV), BF16),
                   jax.ShapeDtypeStruct((nseq, GLA_HEADS, GLA_DK, GLA_DV), F32)),
        grid=(nseq, GLA_HEADS, nb),
        in_specs=[pl.BlockSpec((tb, GLA_DK), lambda b, h, i: (rowblk(b, h, i), qo + h)),
                  pl.BlockSpec((tb, GLA_DK), lambda b, h, i: (rowblk(b, h, i), ko + h)),
                  pl.BlockSpec((tb, GLA_DV), lambda b, h, i: (rowblk(b, h, i), vo + h)),
                  pl.BlockSpec((tb, GLA_DV), lambda b, h, i: (rowblk(b, h, i), ro + h)),
                  pl.BlockSpec((tb, LANE), lambda b, h, i: (rowblk(b, h, i), ao)),
                  pl.BlockSpec((GLA_RANK, GLA_DK), lambda b, h, i: (0, h)),
                  pl.BlockSpec((1, GLA_DK), lambda b, h, i: (0, h)),
                  pl.BlockSpec((1, GLA_DV), lambda b, h, i: (0, 0)),
                  st_spec],
        out_specs=(pl.BlockSpec((tb, GLA_DV), lambda b, h, i: (rowblk(b, h, i), h)), st_spec),
        scratch_shapes=[pltpu.VMEM((GLA_DK, GLA_DV), F32)],
        compiler_params=_cparams(("arbitrary", "arbitrary", "arbitrary")),
        name="gla_scan",
    )(src, src, src, src, src, w_a2, b_a2.reshape(1, -1), norm_g.reshape(1, -1), s0)


def _rope_heads(x, cos, sin):
    return jnp.concatenate(
        [x[:, h * HD:(h + 1) * HD] * cos + pltpu.roll(x[:, h * HD:(h + 1) * HD], HD // 2, axis=1) * sin
         for h in range(x.shape[1] // HD)], axis=1)


def _compress_rows(x, wk_ref, wv_ref, pek_ref, pev_ref, cos, sin):
    rows, width = x.shape
    nb, half = rows // CMP_BLOCK, width // 2

    def softmax_col(w_ref):
        w = w_ref[...]
        e = jnp.exp(w - jnp.max(w, axis=0, keepdims=True))
        return e / jnp.sum(e, axis=0, keepdims=True)

    wk, wv = softmax_col(wk_ref), softmax_col(wv_ref)
    w2 = jnp.concatenate([jnp.broadcast_to(wk, (CMP_BLOCK, half)),
                          jnp.broadcast_to(wv, (CMP_BLOCK, half))], axis=1)
    y = jnp.sum(x.reshape(nb, CMP_BLOCK, width) * w2[None], axis=1)
    pk = jnp.sum(pek_ref[...] * wk, axis=0, keepdims=True)
    pv = jnp.sum(pev_ref[...] * wv, axis=0, keepdims=True)
    kb = y[:, :half] + jnp.concatenate([pk] * NSA_G, axis=1)
    vb = y[:, half:] + jnp.concatenate([pv] * NSA_G, axis=1)
    return jnp.concatenate([_rope_heads(kb, cos, sin), vb], axis=1)


def _compress_prompt_kernel(x_ref, wk_ref, wv_ref, pek_ref, pev_ref, cos_ref, sin_ref, o_ref):
    o_ref[...] = _compress_rows(x_ref[...], wk_ref, wv_ref, pek_ref, pev_ref, cos_ref[...], sin_ref[...])


def _small_specs(nargs):
    col = pl.BlockSpec((CMP_BLOCK, 1), lambda *a: (0, 0))
    pe = pl.BlockSpec((CMP_BLOCK, HD), lambda *a: (0, 0))
    return [col, col, pe, pe]


def _compress_prompt(parts, rows, dst, w_k, w_v, pe_k, pe_v, cos_c, sin_c):
    width = 2 * NSA_G * HD
    tr = _divisor_tile(rows, 1024, 256)
    nbt = tr // CMP_BLOCK
    return pl.pallas_call(
        _compress_prompt_kernel,
        out_shape=jax.ShapeDtypeStruct((rows // CMP_BLOCK, width), F32),
        grid=(rows // tr,),
        in_specs=[pl.BlockSpec((tr, width), lambda i: (i, dst['nsa_kc'] // width))] + _small_specs(1)
        + [pl.BlockSpec((nbt, HD), lambda i: (i, 0)), pl.BlockSpec((nbt, HD), lambda i: (i, 0))],
        out_specs=pl.BlockSpec((nbt, width), lambda i: (i, 0)),
        compiler_params=_cparams(("arbitrary",)),
        name="nsa_compress_prompt",
    )(parts, w_k.reshape(-1, 1), w_v.reshape(-1, 1), pe_k, pe_v, cos_c, sin_c)


def _compress_pages_kernel(pt_ref, *refs, pg):
    page_refs = refs[:pg]
    wk_ref, wv_ref, pek_ref, pev_ref, cos_ref, sin_ref, o_ref = refs[pg:]

    def softmax_col(w_ref):
        w = w_ref[...]
        e = jnp.exp(w - jnp.max(w, axis=0, keepdims=True))
        return e / jnp.sum(e, axis=0, keepdims=True)

    wk, wv = softmax_col(wk_ref), softmax_col(wv_ref)
    blk_rows = CMP_BLOCK * KV_CH
    onehot = ((lax.broadcasted_iota(I32, (blk_rows, CMP_BLOCK), 0) >> 3)
              == lax.broadcasted_iota(I32, (blk_rows, CMP_BLOCK), 1)).astype(F32)
    spread = lambda w: jnp.dot(onehot, jnp.broadcast_to(w, (CMP_BLOCK, HD)), precision=HIGHEST,
                               preferred_element_type=F32)
    is_key = (lax.broadcasted_iota(I32, (blk_rows, HD), 0) & (KV_CH - 1)) < NSA_G
    wexp = jnp.where(is_key, spread(wk), spread(wv))
    pk = jnp.sum(pek_ref[...] * wk, axis=0, keepdims=True)
    pv = jnp.sum(pev_ref[...] * wv, axis=0, keepdims=True)
    pe8 = jnp.where(lax.broadcasted_iota(I32, (KV_CH, HD), 0) < NSA_G, pk, pv)
    outs = []
    for p in range(pg):
        x = page_refs[p][...]
        for n in range(x.shape[0] // blk_rows):
            xw = x[n * blk_rows:(n + 1) * blk_rows] * wexp
            acc = pe8
            for l in range(CMP_BLOCK):
                acc = acc + xw[l * KV_CH:(l + 1) * KV_CH]
            outs.append(acc)
    y = jnp.concatenate(outs, axis=0)
    o_ref[...] = y * cos_ref[...] + pltpu.roll(y, HD // 2, axis=1) * sin_ref[...]


def _compress_pages(cache2, page, page_table, w_k, w_v, pe_k, pe_v, cos_e, sin_e):
    nseq, n_pages = page_table.shape
    pg = _divisor_tile(n_pages, 8, 1)
    out_rows = page // CMP_BLOCK * KV_CH
    page_specs = [pl.BlockSpec((page * KV_CH, HD),
                               (lambda b, i, pt, p=p: (pt[b * n_pages + i * pg + p], 0)))
                  for p in range(pg)]
    small = [pl.BlockSpec((CMP_BLOCK, 1), lambda b, i, pt: (0, 0))] * 2 \
        + [pl.BlockSpec((CMP_BLOCK, HD), lambda b, i, pt: (0, 0))] * 2
    tab = [pl.BlockSpec((pg * out_rows, HD), lambda b, i, pt: (i, 0))] * 2
    return pl.pallas_call(
        functools.partial(_compress_pages_kernel, pg=pg),
        out_shape=jax.ShapeDtypeStruct((nseq, n_pages * out_rows, HD), F32),
        grid_spec=pltpu.PrefetchScalarGridSpec(
            num_scalar_prefetch=1,
            grid=(nseq, n_pages // pg),
            in_specs=page_specs + small + tab,
            out_specs=pl.BlockSpec((None, pg * out_rows, HD), lambda b, i, pt: (b, i, 0))),
        compiler_params=_cparams(("arbitrary", "arbitrary")),
        name="nsa_compress_pages",
    )(page_table.reshape(-1), *([cache2] * pg), w_k.reshape(-1, 1), w_v.reshape(-1, 1), pe_k, pe_v,
      cos_e, sin_e)


def _select_blocks(psum, cur, extra_forced, cnt_ref=None, cur_max=None):
    rows, nb = psum.shape
    lane = lax.broadcasted_iota(I32, (rows, nb), 1)
    jl = lane >> 1
    pair = jnp.where((lane & 1) == 0, psum + pltpu.roll(psum, nb - 1, axis=1),
                     psum + pltpu.roll(psum, 1, axis=1))
    forced = (jl == 0) | (jl == cur) | (jl == cur - 1)
    imp = jnp.where(forced, jnp.inf, jnp.where(jl <= cur, pair, -jnp.inf))
    base = jnp.where(imp < jnp.inf, float(extra_forced), 0.0)

    def count(cnt, j0, j1):
        for j in range(j0, j1):
            vj = imp[:, 2 * j:2 * j + 1]
            beats = (vj > imp) | ((vj == imp) & (jl > j))
            cnt = cnt + jnp.where(beats, 1.0, 0.0)
        return cnt

    if cnt_ref is None:
        return count(base, 0, nb // 2) < float(N_SEL)
    cnt_ref[...] = base
    group = 8
    for j0 in range(0, nb // 2, group):
        @pl.when(j0 <= cur_max)
        def _():
            cnt_ref[...] = count(cnt_ref[...], j0, j0 + group)
    return cnt_ref[...] < float(N_SEL)


def _flash_update(carry, s, mask, vb):
    m, l, acc = carry
    s = jnp.where(mask, s, NEG)
    m_new = jnp.maximum(m, jnp.max(s, axis=-1, keepdims=True))
    p = jnp.where(mask, jnp.exp(s - m_new), 0.0)
    alpha = jnp.exp(m - m_new)
    l = alpha * l + jnp.sum(p, axis=-1, keepdims=True)
    acc = alpha * acc + _dot(p.astype(BF16), vb)
    return m_new, l, acc


def _flash_step(carry, s, vb):
    m, l, acc = carry
    m_new = jnp.maximum(m, jnp.max(s, axis=-1, keepdims=True))
    p = jnp.exp(s - m_new)
    alpha = jnp.exp(m - m_new)
    l = alpha * l + jnp.sum(p, axis=-1, keepdims=True)
    acc = alpha * acc + _dot(p.astype(BF16), vb)
    return m_new, l, acc


def _flash_init(rows):
    return (jnp.full((rows, 1), NEG, F32), jnp.zeros((rows, 1), F32), jnp.zeros((rows, HD), F32))


def _compressed_branch(qb, kcb, vcb, qpos):
    s = _dot_nt(qb, kcb.astype(BF16))
    n = lax.broadcasted_iota(I32, s.shape, 1)
    vis = ((n + 1) * CMP_BLOCK - 1) <= qpos
    s = jnp.where(vis, s, NEG)
    p = jnp.where(vis, jnp.exp(s - jnp.max(s, axis=-1, keepdims=True)), 0.0)
    den = jnp.sum(p, axis=-1, keepdims=True)
    p = p / jnp.where(den > 0.0, den, 1.0)
    return _dot(p.astype(BF16), vcb.astype(BF16)), p


def _expand_blocks(sel_f, first_block, n_keys):
    nb = sel_f.shape[1]
    n = lax.broadcasted_iota(I32, (nb, n_keys), 0)
    kk = lax.broadcasted_iota(I32, (nb, n_keys), 1)
    e = (n == first_block + (kk >> 5)).astype(BF16)
    return _dot(sel_f.astype(BF16), e) > 0.5


def _nsa_prompt_kernel(q_ref, gt_ref, kcb_ref, vcb_ref, ks_ref, vs_ref, kw_ref, vw_ref, o_ref, cnt_ref,
                       *, tq, tk, gate_col):
    g, qi = pl.program_id(1), pl.program_id(2)
    q4 = q_ref[...]
    qb = (jnp.concatenate([q4[:, r * HD:(r + 1) * HD] for r in range(NSA_R)], axis=0)
          * (HD ** -0.5)).astype(BF16)
    rows = NSA_R * tq
    q0 = qi * tq
    qpos1 = q0 + lax.broadcasted_iota(I32, (tq, 1), 0)
    qpos = jnp.concatenate([qpos1] * NSA_R, axis=0)

    o_c, p_c = _compressed_branch(qb, kcb_ref[...], vcb_ref[...], qpos)
    psum = functools.reduce(jnp.add, [p_c[r * tq:(r + 1) * tq] for r in range(NSA_R)])
    sel = _select_blocks(psum, qpos1 >> 6, 0, cnt_ref, (q0 + tq - 1) >> 6)
    sel_b = jnp.where(sel, 1.0, 0.0).astype(BF16)
    nbl = sel_b.shape[1]
    e_n = lax.broadcasted_iota(I32, (nbl, tk), 0)
    e_k = lax.broadcasted_iota(I32, (nbl, tk), 1) >> 5

    def sel_tile(kt, carry, causal):
        start = pl.multiple_of(kt * tk, tk)
        kb = ks_ref[pl.ds(start, tk), :].astype(BF16)
        vb = vs_ref[pl.ds(start, tk), :].astype(BF16)
        picked = _dot(sel_b, (e_n == kt * (tk // CMP_BLOCK) + e_k).astype(BF16))
        bias = (picked - 1.0) * (-NEG)
        if causal:
            kpos = start + lax.broadcasted_iota(I32, (1, tk), 1)
            bias = jnp.where(kpos <= qpos1, bias, NEG)
        s = _dot_nt(qb, kb) + jnp.concatenate([bias] * NSA_R, axis=0)
        return _flash_step(carry, s, vb)

    n_kt = (q0 + tq + tk - 1) // tk
    carry = lax.fori_loop(0, n_kt - 1, lambda kt, c: sel_tile(kt, c, False), _flash_init(rows))
    m_s, l_s, acc_s = sel_tile(n_kt - 1, carry, True)
    o_s = acc_s / l_s

    carry = _flash_init(rows)
    for wt in reversed(range(WINDOW // tq + 1)):
        start = q0 - WINDOW + wt * tq
        cstart = pl.multiple_of(jnp.maximum(start, 0), tq)
        kb = kw_ref[pl.ds(cstart, tq), :].astype(BF16)
        vb = vw_ref[pl.ds(cstart, tq), :].astype(BF16)
        kpos = start + lax.broadcasted_iota(I32, (1, tq), 1)
        mask = (kpos <= qpos) & (kpos > qpos - WINDOW) & (kpos >= 0)
        carry = _flash_step(carry, jnp.where(mask, _dot_nt(qb, kb), NEG), vb)
    o_w = carry[2] / carry[1]

    gt = jax.nn.sigmoid(gt_ref[...])
    lane = lax.broadcasted_iota(I32, gt.shape, 1)
    outs = []
    for r in range(NSA_R):
        col = gate_col + (g * NSA_R + r) * 3
        gs = [jnp.sum(jnp.where(lane == col + j, gt, 0.0), axis=-1, keepdims=True) for j in range(3)]
        sl = slice(r * tq, (r + 1) * tq)
        outs.append(gs[0] * o_c[sl] + gs[1] * o_s[sl] + gs[2] * o_w[sl])
    o_ref[...] = jnp.concatenate(outs, axis=1).astype(BF16)


def _nsa_prompt(parts, kvb, nseq, s, dst):
    tq, tk = 256, 512
    nq = s // tq
    nb = s // CMP_BLOCK
    assert s % tk == 0 and WINDOW % tq == 0 and nb % LANE == 0
    gw = NSA_R * HD
    kv = lambda name: pl.BlockSpec((s, HD), lambda b, g, i: (b, dst[name] // HD + g))
    return pl.pallas_call(
        functools.partial(_nsa_prompt_kernel, tq=tq, tk=tk, gate_col=GLA_RANK),
        out_shape=jax.ShapeDtypeStruct((nseq * s, NSA_HEADS * HD), BF16),
        grid=(nseq, NSA_G, nq),
        in_specs=[pl.BlockSpec((tq, gw), lambda b, g, i: (b * nq + i, dst['nsa_q'] // gw + g)),
                  pl.BlockSpec((tq, LANE), lambda b, g, i: (b * nq + i, dst['gla_a'] // LANE)),
                  pl.BlockSpec((nb, HD), lambda b, g, i: (b, g)),
                  pl.BlockSpec((nb, HD), lambda b, g, i: (b, NSA_G + g)),
                  kv('nsa_ks'), kv('nsa_vs'), kv('nsa_kw'), kv('nsa_vw')],
        out_specs=pl.BlockSpec((tq, gw), lambda b, g, i: (b * nq + i, g)),
        scratch_shapes=[pltpu.VMEM((tq, nb), F32)],
        compiler_params=_cparams(("arbitrary", "arbitrary", "arbitrary")),
        name="nsa_prompt",
    )(parts, parts, kvb, kvb, parts, parts, parts, parts)


def _nsa_sample_kernel(pt_ref, *refs, pg, n_pages, t_new, t_pad, past_len, win_buf):
    page_refs = refs[:pg]
    q_ref, gt_ref, kvb_ref, win_ref, new_ref, o_ref, sel_ref, m_ref, l_ref, acc_ref, oc_ref = refs[pg:]
    i = pl.program_id(1)
    rows = NSA_R * t_pad
    page = page_refs[0].shape[0] // KV_CH
    kvw = NSA_G * HD
    nbc = kvb_ref.shape[0] // KV_CH
    chan = lambda ref, c, n: ref[pl.ds(c, n, stride=KV_CH), :]
    t1 = lax.broadcasted_iota(I32, (t_pad, 1), 0)
    t_row = jnp.concatenate([t1] * NSA_R, axis=0)
    qpos = past_len + t_row

    def queries(g):
        q = q_ref[...]
        return (jnp.concatenate([q[:, (g * NSA_R + r) * HD:(g * NSA_R + r + 1) * HD]
                                 for r in range(NSA_R)], axis=0) * (HD ** -0.5)).astype(BF16)

    @pl.when(i == 0)
    def _():
        for g in range(NSA_G):
            o_c, p_c = _compressed_branch(queries(g), chan(kvb_ref, g, nbc), chan(kvb_ref, NSA_G + g, nbc),
                                          qpos)
            psum = functools.reduce(jnp.add, [p_c[r * t_pad:(r + 1) * t_pad] for r in range(NSA_R)])
            cur = (past_len + t1) >> 6
            sel_ref[g] = _select_blocks(psum, cur, 1).astype(F32)
            oc_ref[g] = o_c
            m_ref[g] = jnp.full((rows, LANE), NEG, F32)
            l_ref[g] = jnp.zeros((rows, LANE), F32)
            acc_ref[g] = jnp.zeros((rows, HD), F32)

    bpp = page // CMP_BLOCK
    for g in range(NSA_G):
        qb = queries(g)
        carry = (m_ref[g][:, :1], l_ref[g][:, :1], acc_ref[g])
        kb = jnp.concatenate([chan(x, g, page).astype(BF16) for x in page_refs], axis=0)
        vb = jnp.concatenate([chan(x, NSA_G + g, page).astype(BF16) for x in page_refs], axis=0)
        m1 = _expand_blocks(sel_ref[g], i * pg * bpp, pg * page)
        mask = jnp.concatenate([m1] * NSA_R, axis=0)
        carry = _flash_update(carry, _dot_nt(qb, kb), mask, vb)
        m_ref[g] = jnp.broadcast_to(carry[0], (rows, LANE))
        l_ref[g] = jnp.broadcast_to(carry[1], (rows, LANE))
        acc_ref[g] = carry[2]

    @pl.when(i == pl.num_programs(1) - 1)
    def _():
        gt = jax.nn.sigmoid(gt_ref[...])
        new = new_ref[...]
        pad = jnp.zeros((LANE - t_pad, HD), F32)
        kk = lax.broadcasted_iota(I32, (1, LANE), 1)
        new_mask = (kk <= t_row) & (kk < t_new)
        wi = lax.broadcasted_iota(I32, (1, win_buf), 1)
        wpos = past_len - win_buf + wi
        win_mask = (wpos <= qpos) & (wpos > qpos - WINDOW) & (wpos >= 0)
        outs = []
        for g in range(NSA_G):
            qb = queries(g)

            def new_rows(j):
                return jnp.concatenate([new[:, j * kvw + g * HD:j * kvw + (g + 1) * HD], pad],
                                       axis=0).astype(BF16)

            carry = (m_ref[g][:, :1], l_ref[g][:, :1], acc_ref[g])
            carry = _flash_update(carry, _dot_nt(qb, new_rows(0)), new_mask, new_rows(1))
            o_s = carry[2] / carry[1]
            carry = _flash_init(rows)
            carry = _flash_update(carry, _dot_nt(qb, chan(win_ref, g, win_buf).astype(BF16)), win_mask,
                                  chan(win_ref, NSA_G + g, win_buf).astype(BF16))
            carry = _flash_update(carry, _dot_nt(qb, new_rows(2)), new_mask, new_rows(3))
            o_w = carry[2] / carry[1]
            o_c = oc_ref[g]
            for r in range(NSA_R):
                col = GLA_RANK + (g * NSA_R + r) * 3
                sl = slice(r * t_pad, (r + 1) * t_pad)
                outs.append(gt[:, col:col + 1] * o_c[sl] + gt[:, col + 1:col + 2] * o_s[sl]
                            + gt[:, col + 2:col + 3] * o_w[sl])
        o_ref[...] = jnp.concatenate(outs, axis=1)


def _nsa_sample(q_s, gt_s, kvb_s, slc2, page, win2, win_buf, new_s, page_table, t_new):
    nseq, t_pad, _ = q_s.shape
    width = KV_CH * HD
    n_pages = page_table.shape[1]
    past_len = n_pages * page
    nbc = kvb_s.shape[1] // KV_CH
    assert nbc % LANE == 0 and t_new <= SEL_BLOCK and past_len % SEL_BLOCK == 0
    pg = _divisor_tile(n_pages, 16, 1)
    rows = NSA_R * t_pad
    page_specs = [pl.BlockSpec((page * KV_CH, HD),
                               (lambda b, i, pt, p=p: (pt[b * n_pages + i * pg + p], 0)))
                  for p in range(pg)]
    per_seq = lambda shape: pl.BlockSpec((None,) + shape, lambda b, i, pt: (b, 0, 0))
    return pl.pallas_call(
        functools.partial(_nsa_sample_kernel, pg=pg, n_pages=n_pages, t_new=t_new, t_pad=t_pad,
                          past_len=past_len, win_buf=win_buf),
        out_shape=jax.ShapeDtypeStruct((nseq, t_pad, NSA_HEADS * HD), F32),
        grid_spec=pltpu.PrefetchScalarGridSpec(
            num_scalar_prefetch=1,
            grid=(nseq, n_pages // pg),
            in_specs=page_specs + [per_seq((t_pad, NSA_HEADS * HD)), per_seq((t_pad, LANE)),
                                   per_seq((nbc * KV_CH, HD)),
                                   pl.BlockSpec((win_buf * KV_CH, HD), lambda b, i, pt: (b, 0)),
                                   per_seq((t_pad, 2 * width))],
            out_specs=per_seq((t_pad, NSA_HEADS * HD)),
            scratch_shapes=[pltpu.VMEM((NSA_G, t_pad, nbc), F32),
                            pltpu.VMEM((NSA_G, rows, LANE), F32),
                            pltpu.VMEM((NSA_G, rows, LANE), F32),
                            pltpu.VMEM((NSA_G, rows, HD), F32),
                            pltpu.VMEM((NSA_G, rows, HD), F32)]),
        compiler_params=_cparams(("arbitrary", "arbitrary")),
        name="nsa_sample",
    )(page_table.reshape(-1), *([slc2] * pg), q_s, gt_s, kvb_s, win2, new_s)


def _router_kernel(u_ref, w_ref, b_ref, gate_ref, pos_ref, cnt_ref, carry_ref, *, n_exp):
    i = pl.program_id(0)
    tt = u_ref.shape[0]

    @pl.when(i == 0)
    def _():
        carry_ref[...] = jnp.zeros_like(carry_ref)

    logits = _dot(u_ref[...], w_ref[...].astype(BF16))
    sc = jax.nn.sigmoid(logits.T[:n_exp])
    biased = sc + b_ref[...]
    per = n_exp // N_GROUPS
    sub = lax.broadcasted_iota(I32, (per, tt), 0)
    gs_rows = []
    for gq in range(N_GROUPS):
        x8 = biased[gq * per:(gq + 1) * per]
        m1 = jnp.max(x8, axis=0, keepdims=True)
        first = jnp.min(jnp.where(x8 == m1, sub, per), axis=0, keepdims=True)
        m2 = jnp.max(jnp.where(sub == first, -jnp.inf, x8), axis=0, keepdims=True)
        gs_rows.append(m1 + m2)
    gs = jnp.concatenate(gs_rows, axis=0)
    gi = lax.broadcasted_iota(I32, gs.shape, 0)
    gcnt = jnp.zeros(gs.shape, F32)
    for j in range(N_GROUPS):
        vj = gs[j:j + 1]
        gcnt = gcnt + jnp.where((vj > gs) | ((vj == gs) & (gi > j)), 1.0, 0.0)
    gsel = jnp.where(gcnt < float(TOPK_GROUPS), 1.0, 0.0)
    emask = jnp.concatenate([jnp.broadcast_to(gsel[gq:gq + 1], (per, tt)) for gq in range(N_GROUPS)],
                            axis=0) > 0.5
    masked = jnp.where(emask, biased, -jnp.inf)
    ei = lax.broadcasted_iota(I32, masked.shape, 0)
    ecnt = jnp.zeros(masked.shape, F32)
    for j in range(n_exp):
        vj = masked[j:j + 1]
        ecnt = ecnt + jnp.where((vj > masked) | ((vj == masked) & (ei > j)), 1.0, 0.0)
    sel = ecnt < float(TOP_K)
    g = jnp.where(sel, sc, 0.0)
    gate_ref[...] = g / jnp.sum(g, axis=0, keepdims=True) * ROUTED_SCALE
    self_ = sel.astype(BF16)
    tr = lax.broadcasted_iota(I32, (tt, tt), 0)
    tc = lax.broadcasted_iota(I32, (tt, tt), 1)
    prefix = _dot(self_, (tr < tc).astype(BF16))
    carry = carry_ref[...]
    pos_ref[...] = jnp.where(sel, (prefix + carry).astype(I32), -1)
    carry = carry + jnp.sum(sel.astype(F32), axis=1, keepdims=True)
    carry_ref[...] = carry
    cnt_ref[...] = carry.astype(I32)


def _router(u, w_router, b_router):
    m, d = u.shape
    n_exp = w_router.shape[1]
    assert n_exp <= LANE and n_exp % (8 * N_GROUPS) == 0
    tt = LANE
    w_pad = jnp.pad(w_router, ((0, 0), (0, LANE - n_exp)))
    return pl.pallas_call(
        functools.partial(_router_kernel, n_exp=n_exp),
        out_shape=(jax.ShapeDtypeStruct((n_exp, m), F32), jax.ShapeDtypeStruct((n_exp, m), I32),
                   jax.ShapeDtypeStruct((n_exp, LANE), I32)),
        grid=(m // tt,),
        in_specs=[pl.BlockSpec((tt, d), lambda i: (i, 0)),
                  pl.BlockSpec((d, LANE), lambda i: (0, 0)),
                  pl.BlockSpec((n_exp, 1), lambda i: (0, 0))],
        out_specs=(pl.BlockSpec((n_exp, tt), lambda i: (0, i)),
                   pl.BlockSpec((n_exp, tt), lambda i: (0, i)),
                   pl.BlockSpec((n_exp, LANE), lambda i: (0, 0))),
        scratch_shapes=[pltpu.VMEM((n_exp, LANE), F32)],
        compiler_params=_cparams(("arbitrary",)),
        name="moe_router",
    )(u, w_pad, b_router.reshape(n_exp, 1))


def _slots_kernel(gate_ref, pos_ref, start_ref, dest_ref, gk_ref):
    pos = pos_ref[...]
    sel = pos >= 0
    n_exp, tt = pos.shape
    dest = start_ref[...] + pos
    er = lax.broadcasted_iota(I32, (n_exp, n_exp), 0)
    ec = lax.broadcasted_iota(I32, (n_exp, n_exp), 1)
    rank = _dot((ec < er).astype(BF16), sel.astype(BF16))
    gate = gate_ref[...]
    for k in range(TOP_K):
        pick = sel & (rank == float(k))
        dest_ref[k:k + 1, :] = jnp.sum(jnp.where(pick, dest, 0), axis=0, keepdims=True)
        gk_ref[k:k + 1, :] = jnp.sum(jnp.where(pick, gate, 0.0), axis=0, keepdims=True)


def _slots(gate_t, pos_t, start):
    n_exp, m = pos_t.shape
    tt = LANE
    return pl.pallas_call(
        _slots_kernel,
        out_shape=(jax.ShapeDtypeStruct((TOP_K, m), I32), jax.ShapeDtypeStruct((TOP_K, m), F32)),
        grid=(m // tt,),
        in_specs=[pl.BlockSpec((n_exp, tt), lambda i: (0, i)),
                  pl.BlockSpec((n_exp, tt), lambda i: (0, i)),
                  pl.BlockSpec((n_exp, 1), lambda i: (0, 0))],
        out_specs=(pl.BlockSpec((TOP_K, tt), lambda i: (0, i)),
                   pl.BlockSpec((TOP_K, tt), lambda i: (0, i))),
        compiler_params=_cparams(("arbitrary",)),
        name="moe_slots",
    )(gate_t, pos_t, start.reshape(n_exp, 1))


def _dispatch_kernel(meta_ref, tok_ref, up_hbm, xs_ref, sem, *, rps):
    i = pl.program_id(0)
    used = i * rps < meta_ref[0] * MOE_ROWS

    @pl.when(used)
    def _():
        for a in range(rps // LANE):
            def start(b, c, a=a):
                pltpu.make_async_copy(up_hbm.at[tok_ref[a, b]], xs_ref.at[a * LANE + b], sem).start()
                return c

            lax.fori_loop(0, LANE, start, 0, unroll=8)
        pltpu.make_async_copy(up_hbm.at[pl.ds(0, rps)], xs_ref, sem).wait()

    @pl.when(jnp.logical_not(used))
    def _():
        xs_ref[...] = jnp.zeros_like(xs_ref)


def _dispatch(meta, row_tok3, up3):
    m, s2, _ = up3.shape
    n_steps, sub, _ = row_tok3.shape
    rps = sub * LANE
    assert m >= rps
    return pl.pallas_call(
        functools.partial(_dispatch_kernel, rps=rps),
        out_shape=jax.ShapeDtypeStruct((n_steps * rps, s2, LANE), U32),
        grid_spec=pltpu.PrefetchScalarGridSpec(
            num_scalar_prefetch=1,
            grid=(n_steps,),
            in_specs=[pl.BlockSpec((None, sub, LANE), lambda i, meta: (i, 0, 0), memory_space=pltpu.SMEM),
                      pl.BlockSpec(memory_space=pl.ANY)],
            out_specs=pl.BlockSpec((rps, s2, LANE), lambda i, meta: (i, 0, 0)),
            scratch_shapes=[pltpu.SemaphoreType.DMA(())]),
        compiler_params=_cparams(("arbitrary",)),
        name="moe_dispatch",
    )(meta, row_tok3, up3)


def _unpack_rows(x_ref, rb, s2):
    cols = []
    for s in range(s2):
        w = x_ref[pl.ds(s, rb, stride=s2), :]
        cols.append(pltpu.bitcast(w << 16, F32))
        cols.append(pltpu.bitcast(w & jnp.uint32(0xFFFF0000), F32))
    return jnp.concatenate(cols, axis=1).astype(BF16)


def _blk(i, meta_ref):
    return jnp.minimum(i, meta_ref[0] - 1)


def _expert_up_kernel(meta_ref, be_ref, x_ref, w1_ref, w3_ref, h_ref, wb1_ref, wb3_ref, *, rb, s2):
    i = pl.program_id(1)
    blk = _blk(i, meta_ref)
    changed = (i == 0) | (be_ref[blk] != be_ref[jnp.maximum(blk - 1, 0)])

    @pl.when(changed)
    def _():
        wb1_ref[...] = w1_ref[...].astype(BF16)
        wb3_ref[...] = w3_ref[...].astype(BF16)

    @pl.when(i < meta_ref[0])
    def _():
        x = _unpack_rows(x_ref, rb, s2)
        h_ref[...] = (_silu(_dot(x, wb1_ref[...])) * _dot(x, wb3_ref[...])).astype(BF16)

    @pl.when(i >= meta_ref[0])
    def _():
        h_ref[...] = jnp.zeros_like(h_ref)


def _expert_up(meta, blk_e, xs2, w1, w3, n_blk, rb, s2):
    n_exp, d, hid = w1.shape
    th = min(hid, 256)
    wspec = pl.BlockSpec((None, d, th), lambda hh, i, meta, be: (be[_blk(i, meta)], 0, hh))
    return pl.pallas_call(
        functools.partial(_expert_up_kernel, rb=rb, s2=s2),
        out_shape=jax.ShapeDtypeStruct((n_blk * rb, hid), BF16),
        grid_spec=pltpu.PrefetchScalarGridSpec(
            num_scalar_prefetch=2,
            grid=(hid // th, n_blk),
            in_specs=[pl.BlockSpec((rb * s2, LANE), lambda hh, i, meta, be: (_blk(i, meta), 0)),
                      wspec, wspec],
            out_specs=pl.BlockSpec((rb, th), lambda hh, i, meta, be: (i, hh)),
            scratch_shapes=[pltpu.VMEM((d, th), BF16), pltpu.VMEM((d, th), BF16)]),
        compiler_params=_cparams(("arbitrary", "arbitrary")),
        name="moe_expert_up",
    )(meta, blk_e, xs2, w1, w3)


def _row_pitch(n_sub):
    return n_sub + 8


def _expert_down_kernel(meta_ref, be_ref, h_ref, w2_ref, y_ref, *, rb, tn, pitch):
    i = pl.program_id(0)

    @pl.when(i < meta_ref[0])
    def _():
        h = h_ref[...]
        d = w2_ref.shape[1]
        for c in range(d // tn):
            y = _dot(h, w2_ref[:, c * tn:(c + 1) * tn].astype(BF16))
            for s, w in enumerate(_pack_pairs(y.astype(BF16).astype(F32))):
                y_ref[pl.ds(c * (tn // 256) + s, rb, stride=pitch), :] = w
        for s in range(d // 256, pitch):
            y_ref[pl.ds(s, rb, stride=pitch), :] = jnp.zeros((rb, LANE), U32)

    @pl.when(i >= meta_ref[0])
    def _():
        y_ref[...] = jnp.zeros_like(y_ref)


def _expert_down(meta, blk_e, h, w2, n_blk, rb):
    n_exp, hid, d = w2.shape
    pitch = _row_pitch(d // 256)
    return pl.pallas_call(
        functools.partial(_expert_down_kernel, rb=rb, tn=min(d, 512), pitch=pitch),
        out_shape=jax.ShapeDtypeStruct((n_blk * rb * pitch, LANE), U32),
        grid_spec=pltpu.PrefetchScalarGridSpec(
            num_scalar_prefetch=2,
            grid=(n_blk,),
            in_specs=[pl.BlockSpec((rb, hid), lambda i, meta, be: (_blk(i, meta), 0)),
                      pl.BlockSpec((None, hid, d), lambda i, meta, be: (be[_blk(i, meta)], 0, 0))],
            out_specs=pl.BlockSpec((rb * pitch, LANE), lambda i, meta, be: (i, 0))),
        compiler_params=_cparams(("arbitrary",)),
        name="moe_expert_down",
    )(meta, blk_e, h, w2)


def _final_kernel(dest_ref, destn_ref, gk_ref, y_hbm, x1_ref, ysh_ref, g2_ref, lg_ref, lb_ref, o_ref,
                  buf_ref, r_ref, sem, *, tf, alpha, pitch):
    i = pl.program_id(0)
    s2 = x1_ref.shape[1] // 256
    per_blk = ROW_TILE // tf
    slot = lax.rem(i, 2)

    def issue(d_ref, step, sl):
        off = lax.rem(step, per_blk) * tf

        def body(t, c):
            for k in range(TOP_K):
                pltpu.make_async_copy(y_hbm.at[d_ref[k, off + t], pl.ds(0, s2)], buf_ref.at[sl, k, t],
                                      sem.at[sl]).start()
            return c

        lax.fori_loop(0, tf, body, 0, unroll=2)

    @pl.when(i == 0)
    def _():
        issue(dest_ref, i, slot)

    @pl.when(i + 1 < pl.num_programs(0))
    def _():
        issue(destn_ref, i + 1, 1 - slot)

    for k in range(TOP_K):
        pltpu.make_async_copy(y_hbm.at[pl.ds(0, tf), pl.ds(0, s2)], buf_ref.at[slot, k],
                              sem.at[slot]).wait()

    off = lax.rem(i, per_blk) * tf

    def token(t, c):
        lo = hi = None
        for k in range(TOP_K):
            g = gk_ref[k, off + t]
            w = buf_ref[slot, k, t]
            tlo = g * pltpu.bitcast(w << 16, F32)
            thi = g * pltpu.bitcast(w & jnp.uint32(0xFFFF0000), F32)
            lo, hi = (tlo, thi) if lo is None else (lo + tlo, hi + thi)
        base = pl.multiple_of(t * pitch, 8)
        r_ref[pl.ds(base, s2), :] = lo
        r_ref[pl.ds(base + s2, s2), :] = hi
        return c

    lax.fori_loop(0, tf, token, 0, unroll=2)
    routed = jnp.concatenate(
        [r_ref[pl.ds(half * s2 + s, tf, stride=pitch), :] for s in range(s2) for half in range(2)], axis=1)
    v = alpha * x1_ref[...] + g2_ref[...] * (routed + ysh_ref[...])
    o_ref[...] = _ln_rows(v) * lg_ref[...] + lb_ref[...]


def _final(dest3, gk3, y3, x1, ysh, mod_tiles, mod_index, tf, ln_g, ln_b, alpha):
    m, d = x1.shape
    n_rows = y3.shape[0]
    s2 = d // 256
    pitch = _row_pitch(d // LANE)
    per_blk = ROW_TILE // tf
    n_steps = m // tf
    assert dest3.shape[2] == ROW_TILE and n_rows >= tf
    row = pl.BlockSpec((tf, d), lambda i: (i, 0))
    vec = pl.BlockSpec((1, d), lambda i: (0, 0))
    smem_blk = lambda f: pl.BlockSpec((None, TOP_K, ROW_TILE), lambda i: (f(i) // per_blk, 0, 0),
                                      memory_space=pltpu.SMEM)
    return pl.pallas_call(
        functools.partial(_final_kernel, tf=tf, alpha=alpha, pitch=pitch),
        out_shape=jax.ShapeDtypeStruct((m, d), F32),
        grid=(n_steps,),
        in_specs=[smem_blk(lambda i: i), smem_blk(lambda i: jnp.minimum(i + 1, n_steps - 1)),
                  smem_blk(lambda i: i),
                  pl.BlockSpec(memory_space=pl.ANY),
                  row, row, _mod_spec(5, d, tf, mod_index), vec, vec],
        out_specs=row,
        scratch_shapes=[pltpu.VMEM((2, TOP_K, tf, s2, LANE), U32),
                        pltpu.VMEM((tf * pitch, LANE), F32),
                        pltpu.SemaphoreType.DMA((2,))],
        compiler_params=_cparams(("arbitrary",)),
        name="moe_combine_ln2",
    )(dest3, dest3, gk3, y3, x1, ysh, mod_tiles, ln_g.reshape(1, d), ln_b.reshape(1, d))


def _rope_tables(pos):
    half = HD // 2
    inv = jnp.power(ROPE_THETA, -jnp.arange(half, dtype=F32) / half)
    ang = pos.astype(F32)[:, None] * inv[None, :]
    cos, sin = jnp.cos(ang), jnp.sin(ang)
    return jnp.concatenate([cos, cos], axis=1), jnp.concatenate([-sin, sin], axis=1)


def kernel(x_prompt, x_sample, c_prompt, c_sample, cache_cmp, cache_slc, cache_win, state_gla, page_table, w_ada, b_ada, w_in, w_gla_a2, b_gla_a2, gla_norm_g, cmp_pe_k, cmp_pe_v, cmp_w_k, cmp_w_v, w_br_gla, w_br_nsa, w_out, ln1_g, ln1_b, w_router, b_router, w_exp1, w_exp3, w_exp2, w_sh1, w_sh3, w_sh2, ln2_g, ln2_b):
    depth = w_in.shape[0]
    assert depth == 1
    alpha = (2.0 * depth) ** 0.25
    nb_, s_, d = x_prompt.shape
    db_, t_, _ = x_sample.shape
    mp, ns = nb_ * s_, db_ * t_
    m = mp + ns
    assert ns % ROW_TILE == 0 and s_ % ROW_TILE == 0 and d % 256 == 0
    n_pool, page = cache_cmp.shape[1], cache_cmp.shape[2]
    n_pages = page_table.shape[1]
    past_len = n_pages * page
    kvw2 = 2 * NSA_G * HD
    src, order, dst, n_used, _ = _col_plan(d)
    tn = 512
    small_off = dst['gla_a']
    dst = dict(dst)
    dst['nsa_g'] = small_off + GLA_RANK
    tn_in = 2 * tn
    n_p = -(-(small_off + LANE) // tn_in) * tn_in

    rc = -(-(nb_ + db_) // 8) * 8
    c_all = jnp.pad(jnp.concatenate([c_prompt, c_sample], axis=0), ((0, rc - nb_ - db_), (0, 0)))
    mod = _ada(c_all, w_ada[0], b_ada[0])
    mod_tiles = jnp.concatenate(
        [jnp.broadcast_to(mod[:nb_, None, :], (nb_, ROW_TILE, 6 * d)),
         jnp.repeat(mod[nb_:nb_ + db_], t_, axis=0).reshape(ns // ROW_TILE, ROW_TILE, 6 * d)], axis=0)
    mod_index = _mod_index(ROW_TILE, mp, s_, nb_)

    x_all = jnp.concatenate([x_prompt.reshape(mp, d), x_sample.reshape(ns, d)], axis=0)
    u1 = _ln_mod(x_all, mod_tiles, mod_index)

    pieces = [w_in[0][:, src[n][0]:src[n][0] + src[n][1]] for n in order]
    w_p = jnp.concatenate(pieces + [jnp.zeros((d, n_p - n_used), F32)], axis=1)
    pos_all = jnp.concatenate([jnp.tile(jnp.arange(s_, dtype=I32), nb_),
                               jnp.tile(past_len + jnp.arange(t_, dtype=I32), db_)])
    cos_t, sin_t = _rope_tables(pos_all)
    tm = _divisor_tile(m, 1056, 16)
    rope_cols = ((dst['nsa_q'], dst['nsa_q'] + NSA_HEADS * HD),
                 (dst['nsa_ks'], dst['nsa_ks'] + NSA_G * HD),
                 (dst['nsa_kw'], dst['nsa_kw'] + NSA_G * HD))
    parts = _inproj(u1, w_p, cos_t, sin_t, rope_cols, _divisor_tile(m, 832, 16), tn_in)

    zeros_state = jnp.zeros((nb_, GLA_HEADS, GLA_DK, GLA_DV), F32)
    og_p, gla_p = _gla(parts, nb_, s_, s_, 512, 128, dst, zeros_state, w_gla_a2[0], b_gla_a2[0],
                       gla_norm_g[0])
    t_pad = 16
    gla_cols = dst['nsa_q']
    smp = parts[mp:].reshape(db_, t_, n_p)
    smp_pad = jnp.pad(smp, ((0, 0), (0, t_pad - t_), (0, 0))).reshape(db_ * t_pad, n_p)
    og_s, gla_s = _gla(smp_pad, db_, t_pad, t_, t_pad, t_pad, dst, state_gla[0], w_gla_a2[0],
                       b_gla_a2[0], gla_norm_g[0])
    og_s = og_s.reshape(db_, t_pad, -1)[:, :t_].reshape(ns, -1)
    og = jnp.concatenate([og_p, og_s], axis=0)

    cmp_pos_p = jnp.tile((jnp.arange(s_ // CMP_BLOCK, dtype=I32) + 1) * CMP_BLOCK - 1, nb_)
    cos_c, sin_c = _rope_tables(cmp_pos_p)
    kvb_p = _compress_prompt(parts, mp, dst, cmp_w_k[0], cmp_w_v[0], cmp_pe_k[0], cmp_pe_v[0], cos_c, sin_c)
    on_p = _nsa_prompt(parts, kvb_p, nb_, s_, dst)

    cmp_pos_s = (jnp.arange(past_len // CMP_BLOCK, dtype=I32) + 1) * CMP_BLOCK - 1
    cos_cs, sin_cs = _rope_tables(cmp_pos_s)
    is_key = (jnp.arange(KV_CH) < NSA_G)[None, :, None]
    cos_e = jnp.where(is_key, cos_cs[:, None, :], 1.0).reshape(-1, HD)
    sin_e = jnp.where(is_key, sin_cs[:, None, :], 0.0).reshape(-1, HD)
    kvb_s = _compress_pages(cache_cmp.reshape(-1, HD), page, page_table, cmp_w_k[0], cmp_w_v[0],
                            cmp_pe_k[0], cmp_pe_v[0], cos_e, sin_e)
    tq_pad = 8
    pad_t = lambda a: jnp.pad(a, ((0, 0), (0, tq_pad - t_), (0, 0)))
    q_s = pad_t(smp[:, :, dst['nsa_q']:dst['nsa_q'] + NSA_HEADS * HD])
    gt_s = pad_t(smp[:, :, small_off:small_off + LANE])
    new_s = pad_t(smp[:, :, dst['nsa_ks']:dst['nsa_ks'] + 2 * kvw2])
    on_s = _nsa_sample(q_s, gt_s, kvb_s, cache_slc.reshape(-1, HD), page, cache_win.reshape(-1, HD),
                       cache_win.shape[2], new_s, page_table, t_)
    on = jnp.concatenate([on_p, on_s[:, :t_].reshape(ns, -1).astype(BF16)], axis=0)

    z = _merge(og, on, w_br_gla[0], w_br_nsa[0], parts, dst['mg_gla'], dst['mg_nsa'], tm, tn)
    y = _plain(z, w_out[0], tm, tn, "out_proj")
    x1, u2, up = _mid(x_all, y, mod_tiles, mod_index, ln1_g[0], ln1_b[0], alpha)

    n_exp = w_router.shape[2]
    rb = MOE_ROWS
    s2 = d // 256
    gate_t, pos_t, cnt = _router(u2, w_router[0], b_router[0])
    counts = cnt[:, 0]
    padded = (counts + rb - 1) // rb * rb
    pend = jnp.cumsum(padded)
    start = pend - padded
    n_blk = -(-(m * TOP_K + n_exp * (rb - 1)) // rb)
    rps = 2 * rb
    n_blk = -(-(m * TOP_K + n_exp * (rb - 1)) // rps) * (rps // rb)
    blk_first = jnp.arange(n_blk, dtype=I32) * rb
    blk_e = jnp.minimum(jnp.sum(pend[None, :] <= blk_first[:, None], axis=1), n_exp - 1).astype(I32)
    meta = (pend[-1:] // rb).astype(I32)
    dest, gk = _slots(gate_t, pos_t, start.astype(I32))
    nt = m // ROW_TILE
    as_blocks = lambda a: a.reshape(TOP_K, nt, ROW_TILE).transpose(1, 0, 2)
    dest3, gk3 = as_blocks(dest), as_blocks(gk)
    row_tok = jnp.zeros((n_blk * rb,), I32).at[dest.reshape(-1)].set(
        jnp.tile(jnp.arange(m, dtype=I32), TOP_K))
    xs3 = _dispatch(meta, row_tok.reshape(-1, rps // LANE, LANE), up.reshape(m, s2, LANE))
    h = _expert_up(meta, blk_e, xs3.reshape(-1, LANE), w_exp1[0], w_exp3[0], n_blk, rb, s2)
    y2 = _expert_down(meta, blk_e, h, w_exp2[0], n_blk, rb)
    hs = _swiglu(u2, w_sh1[0], w_sh3[0], tm, tn // 2)
    ysh = _plain(hs, w_sh2[0], tm, tn, "shared_down")
    tf = 64
    out = _final(dest3, gk3, y2.reshape(n_blk * rb, -1, LANE), x1, ysh, mod_tiles,
                 _mod_index(tf, mp, s_, nb_), tf, ln2_g[0], ln2_b[0], alpha)

    def rows(lo, n_rows, name, lead):
        return parts[lo:lo + n_rows, dst[name]:dst[name] + kvw2].reshape(lead + (2, NSA_G, HD))

    win_buf = cache_win.shape[2]
    win_p = rows(0, mp, 'nsa_kw', (nb_, s_))[:, s_ - win_buf:]
    win_s = jnp.concatenate([cache_win[0], rows(mp, ns, 'nsa_kw', (db_, t_))], axis=1)[:, t_:]
    return (out[:mp].reshape(nb_, s_, d), out[mp:].reshape(db_, t_, d),
            rows(0, mp, 'nsa_kc', (nb_, s_))[None], rows(mp, ns, 'nsa_kc', (db_, t_))[None],
            rows(0, mp, 'nsa_ks', (nb_, s_))[None], rows(mp, ns, 'nsa_ks', (db_, t_))[None],
            win_p[None], win_s[None], gla_p[None], gla_s[None])
```

```python
import functools
import math

import jax
import jax.numpy as jnp
import numpy as np
from jax import lax
from jax.experimental import pallas as pl
from jax.experimental.pallas import tpu as pltpu

F32, BF16, I32, U32 = jnp.float32, jnp.bfloat16, jnp.int32, jnp.uint32
HIGHEST = lax.Precision.HIGHEST

GLA_HEADS, GLA_DK, GLA_DV, GLA_RANK, GLA_GATE_NORM = 8, 128, 256, 16, 16.0
NSA_HEADS, NSA_G, NSA_R, HD = 16, 4, 4, 128
KV_CH = 2 * NSA_G
CMP_BLOCK, SEL_BLOCK, N_SEL, WINDOW = 32, 64, 16, 512
ROPE_THETA = 10000.0
TOP_K, N_GROUPS, TOPK_GROUPS, ROUTED_SCALE = 8, 8, 4, 2.5
LN_EPS = 1e-5
NEG = -1e30

LANE = 128
VMEM_LIMIT = 56 * 1024 * 1024
ROW_TILE = 128
MOE_ROWS = 256
GLA_SUB = 32


def _cparams(sem):
    return pltpu.CompilerParams(dimension_semantics=sem, vmem_limit_bytes=VMEM_LIMIT)


def _divisor_tile(n, target, mult):
    best = None
    for d in range(mult, min(n, target) + 1, mult):
        if n % d == 0:
            best = d
    assert best is not None, (n, target, mult)
    return best


def _silu(x):
    return x * jax.nn.sigmoid(x)


def _ln_rows(x):
    mu = jnp.mean(x, axis=-1, keepdims=True)
    xc = x - mu
    var = jnp.mean(xc * xc, axis=-1, keepdims=True)
    return xc * lax.rsqrt(var + LN_EPS)


def _dot(a, b):
    return jnp.dot(a, b, preferred_element_type=F32)


def _dot_nt(a, b, precision=None):
    return lax.dot_general(a, b, (((1,), (1,)), ((), ())), precision=precision,
                           preferred_element_type=F32)


def _dot_tn(a, b, precision=None):
    return lax.dot_general(a, b, (((0,), (0,)), ((), ())), precision=precision,
                           preferred_element_type=F32)


def _col_plan(d_model):
    gqk, gv = GLA_HEADS * GLA_DK, GLA_HEADS * GLA_DV
    nq, nkv = NSA_HEADS * HD, NSA_G * HD
    ref = (('gla_q', gqk), ('gla_k', gqk), ('gla_v', gv), ('gla_r', gv), ('gla_a', GLA_RANK),
           ('nsa_q', nq), ('nsa_kc', nkv), ('nsa_vc', nkv), ('nsa_ks', nkv), ('nsa_vs', nkv),
           ('nsa_kw', nkv), ('nsa_vw', nkv), ('nsa_g', NSA_HEADS * 3), ('mg_gla', d_model),
           ('mg_nsa', d_model))
    src, o = {}, 0
    for n, w in ref:
        src[n] = (o, w)
        o += w
    order = ('gla_q', 'gla_k', 'gla_v', 'gla_r', 'nsa_q', 'nsa_kc', 'nsa_vc', 'nsa_ks', 'nsa_vs',
             'nsa_kw', 'nsa_vw', 'mg_gla', 'mg_nsa', 'gla_a', 'nsa_g')
    dst, o = {}, 0
    for n in order:
        dst[n] = o
        o += src[n][1]
    return src, order, dst, o, o


def _ada_kernel(c_ref, w_ref, b_ref, o_ref):
    a = _silu(c_ref[...]).astype(BF16)
    o_ref[...] = _dot(a, w_ref[...].astype(BF16)) + b_ref[...]


def _ada(c_all, w_ada, b_ada):
    rc, d = c_all.shape
    n = w_ada.shape[1]
    tn = 512
    return pl.pallas_call(
        _ada_kernel,
        out_shape=jax.ShapeDtypeStruct((rc, n), F32),
        grid=(n // tn,),
        in_specs=[pl.BlockSpec((rc, d), lambda j: (0, 0)),
                  pl.BlockSpec((d, tn), lambda j: (0, j)),
                  pl.BlockSpec((1, tn), lambda j: (0, j))],
        out_specs=pl.BlockSpec((rc, tn), lambda j: (0, j)),
        compiler_params=_cparams(("arbitrary",)),
        name="ada_mod",
    )(c_all, w_ada, b_ada.reshape(1, n))


def _mod_index(rows, mp, seq, nseq):
    n_pt, per_seq, per_grp = mp // rows, seq // rows, ROW_TILE // rows

    def index(i):
        j = jnp.maximum(i - n_pt, 0)
        return (jnp.where(i < n_pt, i // per_seq, nseq + j // per_grp),
                jnp.where(i < n_pt, 0, j % per_grp))

    return index


def _mod_spec(k, d, rows, mod_index):
    return pl.BlockSpec((None, rows, d), lambda i: (*mod_index(i), k))


def _ln_mod_kernel(x_ref, sh_ref, sc_ref, u_ref):
    y = _ln_rows(x_ref[...])
    u_ref[...] = (y * (1.0 + sc_ref[...]) + sh_ref[...]).astype(BF16)


def _ln_mod(x_all, mod_tiles, mod_index):
    m, d = x_all.shape
    return pl.pallas_call(
        _ln_mod_kernel,
        out_shape=jax.ShapeDtypeStruct((m, d), BF16),
        grid=(m // ROW_TILE,),
        in_specs=[pl.BlockSpec((ROW_TILE, d), lambda i: (i, 0)),
                  _mod_spec(0, d, ROW_TILE, mod_index), _mod_spec(1, d, ROW_TILE, mod_index)],
        out_specs=pl.BlockSpec((ROW_TILE, d), lambda i: (i, 0)),
        compiler_params=_cparams(("arbitrary",)),
        name="ln_mod1",
    )(x_all, mod_tiles, mod_tiles)


def _pack_pairs(u):
    bits = pltpu.bitcast(u, U32)
    out = []
    for s in range(u.shape[1] // 256):
        lo = bits[:, s * 256:s * 256 + 128]
        hi = bits[:, s * 256 + 128:s * 256 + 256]
        out.append((lo >> 16) | (hi & jnp.uint32(0xFFFF0000)))
    return out


def _mid_kernel(x_ref, y_ref, g1_ref, sh_ref, sc_ref, lg_ref, lb_ref, x1_ref, u_ref, up_ref, *, alpha):
    v = alpha * x_ref[...] + g1_ref[...] * y_ref[...]
    x1 = _ln_rows(v) * lg_ref[...] + lb_ref[...]
    x1_ref[...] = x1
    u = _ln_rows(x1) * (1.0 + sc_ref[...]) + sh_ref[...]
    ub = u.astype(BF16)
    u_ref[...] = ub
    words = _pack_pairs(ub.astype(F32))
    s2 = len(words)
    for s, w in enumerate(words):
        up_ref[pl.ds(s, ROW_TILE, stride=s2), :] = w


def _mid(x_all, y, mod_tiles, mod_index, ln_g, ln_b, alpha):
    m, d = x_all.shape
    s2 = d // 256
    row = pl.BlockSpec((ROW_TILE, d), lambda i: (i, 0))
    vec = pl.BlockSpec((1, d), lambda i: (0, 0))
    return pl.pallas_call(
        functools.partial(_mid_kernel, alpha=alpha),
        out_shape=(jax.ShapeDtypeStruct((m, d), F32), jax.ShapeDtypeStruct((m, d), BF16),
                   jax.ShapeDtypeStruct((m * s2, LANE), U32)),
        grid=(m // ROW_TILE,),
        in_specs=[row, row, _mod_spec(2, d, ROW_TILE, mod_index), _mod_spec(3, d, ROW_TILE, mod_index),
                  _mod_spec(4, d, ROW_TILE, mod_index), vec, vec],
        out_specs=(row, row, pl.BlockSpec((ROW_TILE * s2, LANE), lambda i: (i, 0))),
        compiler_params=_cparams(("arbitrary",)),
        name="ln1_mod2",
    )(x_all, y, mod_tiles, mod_tiles, mod_tiles, ln_g.reshape(1, d), ln_b.reshape(1, d))


def _cast_weights(i, pairs):
    @pl.when(i == 0)
    def _():
        for w_ref, wb_ref in pairs:
            wb_ref[...] = w_ref[...].astype(BF16)


def _inproj_kernel(mb_ref, xb_ref, u_ref, w_ref, x1_ref, x2_ref, cos_ref, sin_ref, o_ref, wb_ref,
                   *, tn, rope_patterns, shift_tiles, small_tile, small_split):
    j, i = pl.program_id(0), pl.program_id(1)
    d = w_ref.shape[0]
    rows = 512

    @pl.when(i == 0)
    def _():
        for shift, (lo, hi) in shift_tiles:
            @pl.when((j >= lo) & (j < hi))
            def _(shift=shift):
                for r0 in range(0, d, rows):
                    if shift == 0:
                        wb_ref[r0:r0 + rows, :] = w_ref[r0:r0 + rows, :].astype(BF16)
                    else:
                        cat = jnp.concatenate([w_ref[r0:r0 + rows, :], x1_ref[r0:r0 + rows, :]], axis=1)
                        wb_ref[r0:r0 + rows, :] = pltpu.roll(cat, tn + LANE - shift, axis=1)[:, :tn].astype(BF16)

        @pl.when(j == small_tile)
        def _():
            wb_ref[...] = jnp.zeros_like(wb_ref)
            lane = lax.broadcasted_iota(I32, (d, LANE), 1)
            wb_ref[:, :LANE] = jnp.where(lane < small_split[0], x1_ref[...],
                                         jnp.where(lane < small_split[1], x2_ref[...], 0.0)).astype(BF16)

    acc = _dot(u_ref[...], wb_ref[...])
    plain = True
    for flags, tiles in rope_patterns:
        here = functools.reduce(jnp.logical_or, [j == t for t in tiles])
        plain = jnp.logical_and(plain, jnp.logical_not(here))

        @pl.when(here)
        def _(flags=flags):
            cos, sin = cos_ref[...], sin_ref[...]
            for h, rotate in enumerate(flags):
                a = acc[:, h * HD:(h + 1) * HD]
                o_ref[:, h * HD:(h + 1) * HD] = (a * cos + pltpu.roll(a, HD // 2, axis=1) * sin) if rotate else a

    @pl.when(plain)
    def _():
        o_ref[...] = acc


def _inproj(u, w_src, cos_t, sin_t, rope_cols, segments, small, n_p, tm, tn):
    m, d = u.shape
    n_tiles = n_p // tn
    by_flags = {}
    for t in range(n_tiles):
        flags = tuple(any(lo <= t * tn + h * HD < hi for lo, hi in rope_cols) for h in range(tn // HD))
        if any(flags):
            by_flags.setdefault(flags, []).append(t)
    rope_patterns = tuple((f, tuple(ts)) for f, ts in by_flags.items())
    small_off, (blk1, end1), (blk2, end2) = small
    assert small_off % tn == 0 and small_off // tn == n_tiles - 1
    assert all(lo % tn == 0 and hi % tn == 0 and 0 <= sh < LANE for lo, hi, sh in segments)
    main_blk = np.arange(n_tiles, dtype=np.int32)
    main_blk[-1] = n_tiles - 2
    last_x = (w_src.shape[1] - 1) // LANE
    x1_blk = np.minimum((np.arange(n_tiles, dtype=np.int32) + 1) * (tn // LANE), last_x)
    x1_blk[-1] = blk1
    shift_tiles = tuple((sh, (lo // tn, hi // tn)) for lo, hi, sh in segments)
    return pl.pallas_call(
        functools.partial(_inproj_kernel, tn=tn, rope_patterns=rope_patterns, shift_tiles=shift_tiles,
                          small_tile=n_tiles - 1, small_split=(end1, end2)),
        out_shape=jax.ShapeDtypeStruct((m, n_p), F32),
        grid_spec=pltpu.PrefetchScalarGridSpec(
            num_scalar_prefetch=2,
            grid=(n_tiles, m // tm),
            in_specs=[pl.BlockSpec((tm, d), lambda j, i, mb, xb: (i, 0)),
                      pl.BlockSpec((d, tn), lambda j, i, mb, xb: (0, mb[j])),
                      pl.BlockSpec((d, LANE), lambda j, i, mb, xb: (0, xb[j])),
                      pl.BlockSpec((d, LANE), lambda j, i, mb, xb: (0, blk2)),
                      pl.BlockSpec((tm, HD), lambda j, i, mb, xb: (i, 0)),
                      pl.BlockSpec((tm, HD), lambda j, i, mb, xb: (i, 0))],
            out_specs=pl.BlockSpec((tm, tn), lambda j, i, mb, xb: (i, j)),
            scratch_shapes=[pltpu.VMEM((d, tn), BF16)]),
        compiler_params=_cparams(("arbitrary", "arbitrary")),
        name="in_proj",
    )(jnp.asarray(main_blk), jnp.asarray(x1_blk), u, w_src, w_src, w_src, cos_t, sin_t)


def _merge_kernel(a1_ref, a2_ref, w1_ref, w2_ref, g1_ref, g2_ref, o_ref, wb1_ref, wb2_ref):
    _cast_weights(pl.program_id(1), ((w1_ref, wb1_ref), (w2_ref, wb2_ref)))
    z = (jax.nn.sigmoid(g1_ref[...]) * _dot(a1_ref[...], wb1_ref[...])
         + jax.nn.sigmoid(g2_ref[...]) * _dot(a2_ref[...], wb2_ref[...]))
    o_ref[...] = z.astype(BF16)


def _merge(og, on, w1, w2, parts, off1, off2, tm, tn):
    m, k = og.shape
    n = w1.shape[1]
    return pl.pallas_call(
        _merge_kernel,
        out_shape=jax.ShapeDtypeStruct((m, n), BF16),
        grid=(n // tn, m // tm),
        in_specs=[pl.BlockSpec((tm, k), lambda j, i: (i, 0)),
                  pl.BlockSpec((tm, k), lambda j, i: (i, 0)),
                  pl.BlockSpec((k, tn), lambda j, i: (0, j)),
                  pl.BlockSpec((k, tn), lambda j, i: (0, j)),
                  pl.BlockSpec((tm, tn), lambda j, i: (i, off1 // tn + j)),
                  pl.BlockSpec((tm, tn), lambda j, i: (i, off2 // tn + j))],
        out_specs=pl.BlockSpec((tm, tn), lambda j, i: (i, j)),
        scratch_shapes=[pltpu.VMEM((k, tn), BF16), pltpu.VMEM((k, tn), BF16)],
        compiler_params=_cparams(("arbitrary", "arbitrary")),
        name="branch_merge",
    )(og, on, w1, w2, parts, parts)


def _plain_kernel(a_ref, w_ref, o_ref, wb_ref):
    _cast_weights(pl.program_id(1), ((w_ref, wb_ref),))
    o_ref[...] = _dot(a_ref[...], wb_ref[...])


def _plain(a, w, tm, tn, name):
    m, k = a.shape
    n = w.shape[1]
    return pl.pallas_call(
        _plain_kernel,
        out_shape=jax.ShapeDtypeStruct((m, n), F32),
        grid=(n // tn, m // tm),
        in_specs=[pl.BlockSpec((tm, k), lambda j, i: (i, 0)),
                  pl.BlockSpec((k, tn), lambda j, i: (0, j))],
        out_specs=pl.BlockSpec((tm, tn), lambda j, i: (i, j)),
        scratch_shapes=[pltpu.VMEM((k, tn), BF16)],
        compiler_params=_cparams(("arbitrary", "arbitrary")),
        name=name,
    )(a, w)


def _swiglu_kernel(a_ref, w1_ref, w3_ref, o_ref, wb1_ref, wb3_ref):
    _cast_weights(pl.program_id(1), ((w1_ref, wb1_ref), (w3_ref, wb3_ref)))
    a = a_ref[...]
    o_ref[...] = (_silu(_dot(a, wb1_ref[...])) * _dot(a, wb3_ref[...])).astype(BF16)


def _swiglu(a, w1, w3, tm, tn):
    m, k = a.shape
    n = w1.shape[1]
    return pl.pallas_call(
        _swiglu_kernel,
        out_shape=jax.ShapeDtypeStruct((m, n), BF16),
        grid=(n // tn, m // tm),
        in_specs=[pl.BlockSpec((tm, k), lambda j, i: (i, 0)),
                  pl.BlockSpec((k, tn), lambda j, i: (0, j)),
                  pl.BlockSpec((k, tn), lambda j, i: (0, j))],
        out_specs=pl.BlockSpec((tm, tn), lambda j, i: (i, j)),
        scratch_shapes=[pltpu.VMEM((k, tn), BF16), pltpu.VMEM((k, tn), BF16)],
        compiler_params=_cparams(("arbitrary", "arbitrary")),
        name="shared_swiglu",
    )(a, w1, w3)


def _gla_kernel(q_ref, k_ref, v_ref, r_ref, a_ref, wa_ref, ba_ref, ng_ref, s0_ref,
                og_ref, sf_ref, st_ref, *, tb, chunk, s_valid):
    i = pl.program_id(2)

    @pl.when(i == 0)
    def _():
        st_ref[...] = s0_ref[...]

    x = jnp.dot(a_ref[:, :GLA_RANK], wa_ref[...], precision=HIGHEST,
                preferred_element_type=F32) + ba_ref[...]
    glog = (jnp.minimum(x, 0.0) - jnp.log1p(jnp.exp(-jnp.abs(x)))) * (1.0 / GLA_GATE_NORM)
    row = i * tb + lax.broadcasted_iota(I32, (tb, 1), 0)
    glog = jnp.where(row < s_valid, glog, 0.0)

    sub = min(GLA_SUB, chunk)
    rr = lax.broadcasted_iota(I32, (chunk, chunk), 0)
    cc = lax.broadcasted_iota(I32, (chunk, chunk), 1)
    tri = (rr >= cc).astype(F32)
    ones = jnp.ones((chunk, GLA_DK), F32)
    ng = ng_ref[...]

    for c in range(tb // chunk):
        lo = c * chunk
        g = glog[lo:lo + chunk]
        bc = jnp.dot(tri, g, precision=HIGHEST, preferred_element_type=F32)
        bl = bc[chunk - 1:chunk]
        q = q_ref[lo:lo + chunk, :] * (GLA_DK ** -0.5)
        k = k_ref[lo:lo + chunk, :]
        vb = v_ref[lo:lo + chunk, :].astype(BF16)
        qg = (q * jnp.exp(bc)).astype(BF16)
        kd = (k * jnp.exp(bl - bc)).astype(BF16)
        state = st_ref[...]
        o_inter = _dot(qg, state.astype(BF16))
        outs = []
        for sb in range(chunk // sub):
            r0, r1 = sb * sub, (sb + 1) * sub
            base = bc[r0 - 1:r0] if sb > 0 else jnp.zeros((1, GLA_DK), F32)
            qs = (q[r0:r1] * jnp.exp(bc[r0:r1] - base)).astype(BF16)
            ks = (k[:r1] * jnp.exp(base - bc[:r1])).astype(BF16)
            att = _dot_nt(qs, ks)
            causal = (r0 + lax.broadcasted_iota(I32, (sub, r1), 0)) >= lax.broadcasted_iota(I32, (sub, r1), 1)
            att = jnp.where(causal, att, 0.0)
            outs.append(_dot(att.astype(BF16), vb[:r1]))
        o = jnp.concatenate(outs, axis=0) + o_inter
        dec = jnp.exp(_dot_tn(g, ones, precision=HIGHEST))
        dec = jnp.concatenate([dec] * (GLA_DV // GLA_DK), axis=1)
        st_ref[...] = dec * state + _dot_tn(kd, vb)
        o = o * lax.rsqrt(jnp.mean(o * o, axis=-1, keepdims=True) + LN_EPS) * ng
        og_ref[lo:lo + chunk, :] = (o * _silu(r_ref[lo:lo + chunk, :])).astype(BF16)

    @pl.when(i == pl.num_programs(2) - 1)
    def _():
        sf_ref[...] = st_ref[...]


def _gla(src, nseq, s_pad, s_valid, tb, chunk, dst, s0, w_a2, b_a2, norm_g):
    nb = s_pad // tb
    rows = nseq * s_pad
    qo, ko = dst['gla_q'] // GLA_DK, dst['gla_k'] // GLA_DK
    vo, ro = dst['gla_v'] // GLA_DV, dst['gla_r'] // GLA_DV
    ao = dst['gla_a'] // LANE
    rowblk = lambda b, h, i: b * nb + i
    st_spec = pl.BlockSpec((None, None, GLA_DK, GLA_DV), lambda b, h, i: (b, h, 0, 0))
    return pl.pallas_call(
        functools.partial(_gla_kernel, tb=tb, chunk=chunk, s_valid=s_valid),
        out_shape=(jax.ShapeDtypeStruct((rows, GLA_HEADS * GLA_DV), BF16),
                   jax.ShapeDtypeStruct((nseq, GLA_HEADS, GLA_DK, GLA_DV), F32)),
        grid=(nseq, GLA_HEADS, nb),
        in_specs=[pl.BlockSpec((tb, GLA_DK), lambda b, h, i: (rowblk(b, h, i), qo + h)),
                  pl.BlockSpec((tb, GLA_DK), lambda b, h, i: (rowblk(b, h, i), ko + h)),
                  pl.BlockSpec((tb, GLA_DV), lambda b, h, i: (rowblk(b, h, i), vo + h)),
                  pl.BlockSpec((tb, GLA_DV), lambda b, h, i: (rowblk(b, h, i), ro + h)),
                  pl.BlockSpec((tb, LANE), lambda b, h, i: (rowblk(b, h, i), ao)),
                  pl.BlockSpec((GLA_RANK, GLA_DK), lambda b, h, i: (0, h)),
                  pl.BlockSpec((1, GLA_DK), lambda b, h, i: (0, h)),
                  pl.BlockSpec((1, GLA_DV), lambda b, h, i: (0, 0)),
                  st_spec],
        out_specs=(pl.BlockSpec((tb, GLA_DV), lambda b, h, i: (rowblk(b, h, i), h)), st_spec),
        scratch_shapes=[pltpu.VMEM((GLA_DK, GLA_DV), F32)],
        compiler_params=_cparams(("arbitrary", "arbitrary", "arbitrary")),
        name="gla_scan",
    )(src, src, src, src, src, w_a2, b_a2.reshape(1, -1), norm_g.reshape(1, -1), s0)


def _rope_heads(x, cos, sin):
    return jnp.concatenate(
        [x[:, h * HD:(h + 1) * HD] * cos + pltpu.roll(x[:, h * HD:(h + 1) * HD], HD // 2, axis=1) * sin
         for h in range(x.shape[1] // HD)], axis=1)


def _compress_rows(x, wk_ref, wv_ref, pek_ref, pev_ref, cos, sin):
    rows, width = x.shape
    nb, half = rows // CMP_BLOCK, width // 2

    def softmax_col(w_ref):
        w = w_ref[...]
        e = jnp.exp(w - jnp.max(w, axis=0, keepdims=True))
        return e / jnp.sum(e, axis=0, keepdims=True)

    wk, wv = softmax_col(wk_ref), softmax_col(wv_ref)
    w2 = jnp.concatenate([jnp.broadcast_to(wk, (CMP_BLOCK, half)),
                          jnp.broadcast_to(wv, (CMP_BLOCK, half))], axis=1)
    y = jnp.sum(x.reshape(nb, CMP_BLOCK, width) * w2[None], axis=1)
    pk = jnp.sum(pek_ref[...] * wk, axis=0, keepdims=True)
    pv = jnp.sum(pev_ref[...] * wv, axis=0, keepdims=True)
    kb = y[:, :half] + jnp.concatenate([pk] * NSA_G, axis=1)
    vb = y[:, half:] + jnp.concatenate([pv] * NSA_G, axis=1)
    return jnp.concatenate([_rope_heads(kb, cos, sin), vb], axis=1)


def _compress_prompt_kernel(x_ref, wk_ref, wv_ref, pek_ref, pev_ref, cos_ref, sin_ref, o_ref):
    o_ref[...] = _compress_rows(x_ref[...], wk_ref, wv_ref, pek_ref, pev_ref, cos_ref[...], sin_ref[...])


def _small_specs(nargs):
    col = pl.BlockSpec((CMP_BLOCK, 1), lambda *a: (0, 0))
    pe = pl.BlockSpec((CMP_BLOCK, HD), lambda *a: (0, 0))
    return [col, col, pe, pe]


def _compress_prompt(parts, rows, dst, w_k, w_v, pe_k, pe_v, cos_c, sin_c):
    width = 2 * NSA_G * HD
    tr = _divisor_tile(rows, 1024, 256)
    nbt = tr // CMP_BLOCK
    return pl.pallas_call(
        _compress_prompt_kernel,
        out_shape=jax.ShapeDtypeStruct((rows // CMP_BLOCK, width), F32),
        grid=(rows // tr,),
        in_specs=[pl.BlockSpec((tr, width), lambda i: (i, dst['nsa_kc'] // width))] + _small_specs(1)
        + [pl.BlockSpec((nbt, HD), lambda i: (i, 0)), pl.BlockSpec((nbt, HD), lambda i: (i, 0))],
        out_specs=pl.BlockSpec((nbt, width), lambda i: (i, 0)),
        compiler_params=_cparams(("arbitrary",)),
        name="nsa_compress_prompt",
    )(parts, w_k.reshape(-1, 1), w_v.reshape(-1, 1), pe_k, pe_v, cos_c, sin_c)


def _compress_pages_kernel(pt_ref, *refs, pg):
    page_refs = refs[:pg]
    wk_ref, wv_ref, pek_ref, pev_ref, cos_ref, sin_ref, o_ref = refs[pg:]

    def softmax_col(w_ref):
        w = w_ref[...]
        e = jnp.exp(w - jnp.max(w, axis=0, keepdims=True))
        return e / jnp.sum(e, axis=0, keepdims=True)

    wk, wv = softmax_col(wk_ref), softmax_col(wv_ref)
    blk_rows = CMP_BLOCK * KV_CH
    onehot = ((lax.broadcasted_iota(I32, (blk_rows, CMP_BLOCK), 0) >> 3)
              == lax.broadcasted_iota(I32, (blk_rows, CMP_BLOCK), 1)).astype(F32)
    spread = lambda w: jnp.dot(onehot, jnp.broadcast_to(w, (CMP_BLOCK, HD)), precision=HIGHEST,
                               preferred_element_type=F32)
    is_key = (lax.broadcasted_iota(I32, (blk_rows, HD), 0) & (KV_CH - 1)) < NSA_G
    wexp = jnp.where(is_key, spread(wk), spread(wv))
    pk = jnp.sum(pek_ref[...] * wk, axis=0, keepdims=True)
    pv = jnp.sum(pev_ref[...] * wv, axis=0, keepdims=True)
    pe8 = jnp.where(lax.broadcasted_iota(I32, (KV_CH, HD), 0) < NSA_G, pk, pv)
    outs = []
    for p in range(pg):
        x = page_refs[p][...]
        for n in range(x.shape[0] // blk_rows):
            xw = x[n * blk_rows:(n + 1) * blk_rows] * wexp
            acc = pe8
            for l in range(CMP_BLOCK):
                acc = acc + xw[l * KV_CH:(l + 1) * KV_CH]
            outs.append(acc)
    y = jnp.concatenate(outs, axis=0)
    o_ref[...] = y * cos_ref[...] + pltpu.roll(y, HD // 2, axis=1) * sin_ref[...]


def _compress_pages(cache2, page, page_table, w_k, w_v, pe_k, pe_v, cos_e, sin_e):
    nseq, n_pages = page_table.shape
    pg = _divisor_tile(n_pages, 8, 1)
    out_rows = page // CMP_BLOCK * KV_CH
    page_specs = [pl.BlockSpec((page * KV_CH, HD),
                               (lambda b, i, pt, p=p: (pt[b * n_pages + i * pg + p], 0)))
                  for p in range(pg)]
    small = [pl.BlockSpec((CMP_BLOCK, 1), lambda b, i, pt: (0, 0))] * 2 \
        + [pl.BlockSpec((CMP_BLOCK, HD), lambda b, i, pt: (0, 0))] * 2
    tab = [pl.BlockSpec((pg * out_rows, HD), lambda b, i, pt: (i, 0))] * 2
    return pl.pallas_call(
        functools.partial(_compress_pages_kernel, pg=pg),
        out_shape=jax.ShapeDtypeStruct((nseq, n_pages * out_rows, HD), F32),
        grid_spec=pltpu.PrefetchScalarGridSpec(
            num_scalar_prefetch=1,
            grid=(nseq, n_pages // pg),
            in_specs=page_specs + small + tab,
            out_specs=pl.BlockSpec((None, pg * out_rows, HD), lambda b, i, pt: (b, i, 0))),
        compiler_params=_cparams(("arbitrary", "arbitrary")),
        name="nsa_compress_pages",
    )(page_table.reshape(-1), *([cache2] * pg), w_k.reshape(-1, 1), w_v.reshape(-1, 1), pe_k, pe_v,
      cos_e, sin_e)


def _select_blocks(psum, cur, extra_forced, cnt_ref=None, cur_max=None):
    rows, nb = psum.shape
    lane = lax.broadcasted_iota(I32, (rows, nb), 1)
    jl = lane >> 1
    pair = jnp.where((lane & 1) == 0, psum + pltpu.roll(psum, nb - 1, axis=1),
                     psum + pltpu.roll(psum, 1, axis=1))
    forced = (jl == 0) | (jl == cur) | (jl == cur - 1)
    imp = jnp.where(forced, jnp.inf, jnp.where(jl <= cur, pair, -jnp.inf))
    base = jnp.where(imp < jnp.inf, float(extra_forced), 0.0)

    def count(cnt, j0, j1):
        for j in range(j0, j1):
            vj = imp[:, 2 * j:2 * j + 1]
            beats = (vj > imp) | ((vj == imp) & (jl > j))
            cnt = cnt + jnp.where(beats, 1.0, 0.0)
        return cnt

    if cnt_ref is None:
        return count(base, 0, nb // 2) < float(N_SEL)
    cnt_ref[...] = base
    group = 8
    for j0 in range(0, nb // 2, group):
        @pl.when(j0 <= cur_max)
        def _():
            cnt_ref[...] = count(cnt_ref[...], j0, j0 + group)
    return cnt_ref[...] < float(N_SEL)


def _flash_update(carry, s, mask, vb):
    m, l, acc = carry
    s = jnp.where(mask, s, NEG)
    m_new = jnp.maximum(m, jnp.max(s, axis=-1, keepdims=True))
    p = jnp.where(mask, jnp.exp(s - m_new), 0.0)
    alpha = jnp.exp(m - m_new)
    l = alpha * l + jnp.sum(p, axis=-1, keepdims=True)
    acc = alpha * acc + _dot(p.astype(BF16), vb)
    return m_new, l, acc


def _flash_step(carry, s, vb):
    m, l, acc = carry
    m_new = jnp.maximum(m, jnp.max(s, axis=-1, keepdims=True))
    p = jnp.exp(s - m_new)
    alpha = jnp.exp(m - m_new)
    l = alpha * l + jnp.sum(p, axis=-1, keepdims=True)
    acc = alpha * acc + _dot(p.astype(BF16), vb)
    return m_new, l, acc


def _flash_init(rows):
    return (jnp.full((rows, 1), NEG, F32), jnp.zeros((rows, 1), F32), jnp.zeros((rows, HD), F32))


def _compressed_branch(qb, kcb, vcb, qpos):
    s = _dot_nt(qb, kcb.astype(BF16))
    n = lax.broadcasted_iota(I32, s.shape, 1)
    vis = ((n + 1) * CMP_BLOCK - 1) <= qpos
    s = jnp.where(vis, s, NEG)
    p = jnp.where(vis, jnp.exp(s - jnp.max(s, axis=-1, keepdims=True)), 0.0)
    den = jnp.sum(p, axis=-1, keepdims=True)
    p = p / jnp.where(den > 0.0, den, 1.0)
    return _dot(p.astype(BF16), vcb.astype(BF16)), p


def _expand_blocks(sel_f, first_block, n_keys):
    nb = sel_f.shape[1]
    n = lax.broadcasted_iota(I32, (nb, n_keys), 0)
    kk = lax.broadcasted_iota(I32, (nb, n_keys), 1)
    e = (n == first_block + (kk >> 5)).astype(BF16)
    return _dot(sel_f.astype(BF16), e) > 0.5


def _nsa_prompt_kernel(q_ref, gt_ref, kcb_ref, vcb_ref, ks_ref, vs_ref, kw_ref, vw_ref, o_ref, cnt_ref,
                       *, tq, tk, gate_col):
    g, qi = pl.program_id(1), pl.program_id(2)
    q4 = q_ref[...]
    qb = (jnp.concatenate([q4[:, r * HD:(r + 1) * HD] for r in range(NSA_R)], axis=0)
          * (HD ** -0.5)).astype(BF16)
    rows = NSA_R * tq
    q0 = qi * tq
    qpos1 = q0 + lax.broadcasted_iota(I32, (tq, 1), 0)
    qpos = jnp.concatenate([qpos1] * NSA_R, axis=0)

    o_c, p_c = _compressed_branch(qb, kcb_ref[...], vcb_ref[...], qpos)
    psum = functools.reduce(jnp.add, [p_c[r * tq:(r + 1) * tq] for r in range(NSA_R)])
    sel = _select_blocks(psum, qpos1 >> 6, 0, cnt_ref, (q0 + tq - 1) >> 6)
    sel_b = jnp.where(sel, 1.0, 0.0).astype(BF16)
    nbl = sel_b.shape[1]
    e_n = lax.broadcasted_iota(I32, (nbl, tk), 0)
    e_k = lax.broadcasted_iota(I32, (nbl, tk), 1) >> 5

    def sel_tile(kt, carry, causal):
        start = pl.multiple_of(kt * tk, tk)
        kb = ks_ref[pl.ds(start, tk), :].astype(BF16)
        vb = vs_ref[pl.ds(start, tk), :].astype(BF16)
        picked = _dot(sel_b, (e_n == kt * (tk // CMP_BLOCK) + e_k).astype(BF16))
        bias = (picked - 1.0) * (-NEG)
        if causal:
            kpos = start + lax.broadcasted_iota(I32, (1, tk), 1)
            bias = jnp.where(kpos <= qpos1, bias, NEG)
        s = _dot_nt(qb, kb) + jnp.concatenate([bias] * NSA_R, axis=0)
        return _flash_step(carry, s, vb)

    n_kt = (q0 + tq + tk - 1) // tk
    carry = lax.fori_loop(0, n_kt - 1, lambda kt, c: sel_tile(kt, c, False), _flash_init(rows))
    m_s, l_s, acc_s = sel_tile(n_kt - 1, carry, True)
    o_s = acc_s / l_s

    carry = _flash_init(rows)
    for wt in reversed(range(WINDOW // tq + 1)):
        start = q0 - WINDOW + wt * tq
        cstart = pl.multiple_of(jnp.maximum(start, 0), tq)
        kb = kw_ref[pl.ds(cstart, tq), :].astype(BF16)
        vb = vw_ref[pl.ds(cstart, tq), :].astype(BF16)
        kpos = start + lax.broadcasted_iota(I32, (1, tq), 1)
        mask = (kpos <= qpos) & (kpos > qpos - WINDOW) & (kpos >= 0)
        carry = _flash_step(carry, jnp.where(mask, _dot_nt(qb, kb), NEG), vb)
    o_w = carry[2] / carry[1]

    gt = jax.nn.sigmoid(gt_ref[...])
    lane = lax.broadcasted_iota(I32, gt.shape, 1)
    outs = []
    for r in range(NSA_R):
        col = gate_col + (g * NSA_R + r) * 3
        gs = [jnp.sum(jnp.where(lane == col + j, gt, 0.0), axis=-1, keepdims=True) for j in range(3)]
        sl = slice(r * tq, (r + 1) * tq)
        outs.append(gs[0] * o_c[sl] + gs[1] * o_s[sl] + gs[2] * o_w[sl])
    o_ref[...] = jnp.concatenate(outs, axis=1).astype(BF16)


def _nsa_prompt(parts, kvb, nseq, s, dst):
    tq, tk = 256, 512
    nq = s // tq
    nb = s // CMP_BLOCK
    assert s % tk == 0 and WINDOW % tq == 0 and nb % LANE == 0
    gw = NSA_R * HD
    kv = lambda name: pl.BlockSpec((s, HD), lambda b, g, i: (b, dst[name] // HD + g))
    return pl.pallas_call(
        functools.partial(_nsa_prompt_kernel, tq=tq, tk=tk, gate_col=GLA_RANK),
        out_shape=jax.ShapeDtypeStruct((nseq * s, NSA_HEADS * HD), BF16),
        grid=(nseq, NSA_G, nq),
        in_specs=[pl.BlockSpec((tq, gw), lambda b, g, i: (b * nq + i, dst['nsa_q'] // gw + g)),
                  pl.BlockSpec((tq, LANE), lambda b, g, i: (b * nq + i, dst['gla_a'] // LANE)),
                  pl.BlockSpec((nb, HD), lambda b, g, i: (b, g)),
                  pl.BlockSpec((nb, HD), lambda b, g, i: (b, NSA_G + g)),
                  kv('nsa_ks'), kv('nsa_vs'), kv('nsa_kw'), kv('nsa_vw')],
        out_specs=pl.BlockSpec((tq, gw), lambda b, g, i: (b * nq + i, g)),
        scratch_shapes=[pltpu.VMEM((tq, nb), F32)],
        compiler_params=_cparams(("arbitrary", "arbitrary", "arbitrary")),
        name="nsa_prompt",
    )(parts, parts, kvb, kvb, parts, parts, parts, parts)


def _nsa_sample_kernel(pt_ref, *refs, pg, n_pages, t_new, t_pad, past_len, win_buf):
    page_refs = refs[:pg]
    q_ref, gt_ref, kvb_ref, win_ref, new_ref, o_ref, sel_ref, m_ref, l_ref, acc_ref, oc_ref = refs[pg:]
    i = pl.program_id(1)
    rows = NSA_R * t_pad
    page = page_refs[0].shape[0] // KV_CH
    kvw = NSA_G * HD
    nbc = kvb_ref.shape[0] // KV_CH
    chan = lambda ref, c, n: ref[pl.ds(c, n, stride=KV_CH), :]
    t1 = lax.broadcasted_iota(I32, (t_pad, 1), 0)
    t_row = jnp.concatenate([t1] * NSA_R, axis=0)
    qpos = past_len + t_row

    def queries(g):
        q = q_ref[...]
        return (jnp.concatenate([q[:, (g * NSA_R + r) * HD:(g * NSA_R + r + 1) * HD]
                                 for r in range(NSA_R)], axis=0) * (HD ** -0.5)).astype(BF16)

    @pl.when(i == 0)
    def _():
        for g in range(NSA_G):
            o_c, p_c = _compressed_branch(queries(g), chan(kvb_ref, g, nbc), chan(kvb_ref, NSA_G + g, nbc),
                                          qpos)
            psum = functools.reduce(jnp.add, [p_c[r * t_pad:(r + 1) * t_pad] for r in range(NSA_R)])
            cur = (past_len + t1) >> 6
            sel_ref[g] = _select_blocks(psum, cur, 1).astype(F32)
            oc_ref[g] = o_c
            m_ref[g] = jnp.full((rows, LANE), NEG, F32)
            l_ref[g] = jnp.zeros((rows, LANE), F32)
            acc_ref[g] = jnp.zeros((rows, HD), F32)

    bpp = page // CMP_BLOCK
    for g in range(NSA_G):
        qb = queries(g)
        carry = (m_ref[g][:, :1], l_ref[g][:, :1], acc_ref[g])
        kb = jnp.concatenate([chan(x, g, page).astype(BF16) for x in page_refs], axis=0)
        vb = jnp.concatenate([chan(x, NSA_G + g, page).astype(BF16) for x in page_refs], axis=0)
        m1 = _expand_blocks(sel_ref[g], i * pg * bpp, pg * page)
        mask = jnp.concatenate([m1] * NSA_R, axis=0)
        carry = _flash_update(carry, _dot_nt(qb, kb), mask, vb)
        m_ref[g] = jnp.broadcast_to(carry[0], (rows, LANE))
        l_ref[g] = jnp.broadcast_to(carry[1], (rows, LANE))
        acc_ref[g] = carry[2]

    @pl.when(i == pl.num_programs(1) - 1)
    def _():
        gt = jax.nn.sigmoid(gt_ref[...])
        new = new_ref[...]
        pad = jnp.zeros((LANE - t_pad, HD), F32)
        kk = lax.broadcasted_iota(I32, (1, LANE), 1)
        new_mask = (kk <= t_row) & (kk < t_new)
        wi = lax.broadcasted_iota(I32, (1, win_buf), 1)
        wpos = past_len - win_buf + wi
        win_mask = (wpos <= qpos) & (wpos > qpos - WINDOW) & (wpos >= 0)
        outs = []
        for g in range(NSA_G):
            qb = queries(g)

            def new_rows(j):
                return jnp.concatenate([new[:, j * kvw + g * HD:j * kvw + (g + 1) * HD], pad],
                                       axis=0).astype(BF16)

            carry = (m_ref[g][:, :1], l_ref[g][:, :1], acc_ref[g])
            carry = _flash_update(carry, _dot_nt(qb, new_rows(0)), new_mask, new_rows(1))
            o_s = carry[2] / carry[1]
            carry = _flash_init(rows)
            carry = _flash_update(carry, _dot_nt(qb, chan(win_ref, g, win_buf).astype(BF16)), win_mask,
                                  chan(win_ref, NSA_G + g, win_buf).astype(BF16))
            carry = _flash_update(carry, _dot_nt(qb, new_rows(2)), new_mask, new_rows(3))
            o_w = carry[2] / carry[1]
            o_c = oc_ref[g]
            for r in range(NSA_R):
                col = GLA_RANK + (g * NSA_R + r) * 3
                sl = slice(r * t_pad, (r + 1) * t_pad)
                outs.append(gt[:, col:col + 1] * o_c[sl] + gt[:, col + 1:col + 2] * o_s[sl]
                            + gt[:, col + 2:col + 3] * o_w[sl])
        o_ref[...] = jnp.concatenate(outs, axis=1)


def _nsa_sample(q_s, gt_s, kvb_s, slc2, page, win2, win_buf, new_s, page_table, t_new):
    nseq, t_pad, _ = q_s.shape
    width = KV_CH * HD
    n_pages = page_table.shape[1]
    past_len = n_pages * page
    nbc = kvb_s.shape[1] // KV_CH
    assert nbc % LANE == 0 and t_new <= SEL_BLOCK and past_len % SEL_BLOCK == 0
    pg = _divisor_tile(n_pages, 16, 1)
    rows = NSA_R * t_pad
    page_specs = [pl.BlockSpec((page * KV_CH, HD),
                               (lambda b, i, pt, p=p: (pt[b * n_pages + i * pg + p], 0)))
                  for p in range(pg)]
    per_seq = lambda shape: pl.BlockSpec((None,) + shape, lambda b, i, pt: (b, 0, 0))
    return pl.pallas_call(
        functools.partial(_nsa_sample_kernel, pg=pg, n_pages=n_pages, t_new=t_new, t_pad=t_pad,
                          past_len=past_len, win_buf=win_buf),
        out_shape=jax.ShapeDtypeStruct((nseq, t_pad, NSA_HEADS * HD), F32),
        grid_spec=pltpu.PrefetchScalarGridSpec(
            num_scalar_prefetch=1,
            grid=(nseq, n_pages // pg),
            in_specs=page_specs + [per_seq((t_pad, NSA_HEADS * HD)), per_seq((t_pad, LANE)),
                                   per_seq((nbc * KV_CH, HD)),
                                   pl.BlockSpec((win_buf * KV_CH, HD), lambda b, i, pt: (b, 0)),
                                   per_seq((t_pad, 2 * width))],
            out_specs=per_seq((t_pad, NSA_HEADS * HD)),
            scratch_shapes=[pltpu.VMEM((NSA_G, t_pad, nbc), F32),
                            pltpu.VMEM((NSA_G, rows, LANE), F32),
                            pltpu.VMEM((NSA_G, rows, LANE), F32),
                            pltpu.VMEM((NSA_G, rows, HD), F32),
                            pltpu.VMEM((NSA_G, rows, HD), F32)]),
        compiler_params=_cparams(("arbitrary", "arbitrary")),
        name="nsa_sample",
    )(page_table.reshape(-1), *([slc2] * pg), q_s, gt_s, kvb_s, win2, new_s)


def _router_kernel(u_ref, w_ref, b_ref, gate_ref, pos_ref, cnt_ref, carry_ref, *, n_exp):
    i = pl.program_id(0)
    tt = u_ref.shape[0]

    @pl.when(i == 0)
    def _():
        carry_ref[...] = jnp.zeros_like(carry_ref)

    logits = _dot(u_ref[...], w_ref[...].astype(BF16))
    sc = jax.nn.sigmoid(logits.T[:n_exp])
    biased = sc + b_ref[...]
    per = n_exp // N_GROUPS
    sub = lax.broadcasted_iota(I32, (per, tt), 0)
    gs_rows = []
    for gq in range(N_GROUPS):
        x8 = biased[gq * per:(gq + 1) * per]
        m1 = jnp.max(x8, axis=0, keepdims=True)
        first = jnp.min(jnp.where(x8 == m1, sub, per), axis=0, keepdims=True)
        m2 = jnp.max(jnp.where(sub == first, -jnp.inf, x8), axis=0, keepdims=True)
        gs_rows.append(m1 + m2)
    gs = jnp.concatenate(gs_rows, axis=0)
    gi = lax.broadcasted_iota(I32, gs.shape, 0)
    gcnt = jnp.zeros(gs.shape, F32)
    for j in range(N_GROUPS):
        vj = gs[j:j + 1]
        gcnt = gcnt + jnp.where((vj > gs) | ((vj == gs) & (gi > j)), 1.0, 0.0)
    gsel = jnp.where(gcnt < float(TOPK_GROUPS), 1.0, 0.0)
    emask = jnp.concatenate([jnp.broadcast_to(gsel[gq:gq + 1], (per, tt)) for gq in range(N_GROUPS)],
                            axis=0) > 0.5
    masked = jnp.where(emask, biased, -jnp.inf)
    ei = lax.broadcasted_iota(I32, masked.shape, 0)
    ecnt = jnp.zeros(masked.shape, F32)
    for j in range(n_exp):
        vj = masked[j:j + 1]
        ecnt = ecnt + jnp.where((vj > masked) | ((vj == masked) & (ei > j)), 1.0, 0.0)
    sel = ecnt < float(TOP_K)
    g = jnp.where(sel, sc, 0.0)
    gate_ref[...] = g / jnp.sum(g, axis=0, keepdims=True) * ROUTED_SCALE
    self_ = sel.astype(BF16)
    tr = lax.broadcasted_iota(I32, (tt, tt), 0)
    tc = lax.broadcasted_iota(I32, (tt, tt), 1)
    prefix = _dot(self_, (tr < tc).astype(BF16))
    carry = carry_ref[...]
    pos_ref[...] = jnp.where(sel, (prefix + carry).astype(I32), -1)
    carry = carry + jnp.sum(sel.astype(F32), axis=1, keepdims=True)
    carry_ref[...] = carry
    cnt_ref[...] = carry.astype(I32)


def _router(u, w_router, b_router):
    m, d = u.shape
    n_exp = w_router.shape[1]
    assert n_exp <= LANE and n_exp % (8 * N_GROUPS) == 0
    tt = LANE
    w_pad = jnp.pad(w_router, ((0, 0), (0, LANE - n_exp)))
    return pl.pallas_call(
        functools.partial(_router_kernel, n_exp=n_exp),
        out_shape=(jax.ShapeDtypeStruct((n_exp, m), F32), jax.ShapeDtypeStruct((n_exp, m), I32),
                   jax.ShapeDtypeStruct((n_exp, LANE), I32)),
        grid=(m // tt,),
        in_specs=[pl.BlockSpec((tt, d), lambda i: (i, 0)),
                  pl.BlockSpec((d, LANE), lambda i: (0, 0)),
                  pl.BlockSpec((n_exp, 1), lambda i: (0, 0))],
        out_specs=(pl.BlockSpec((n_exp, tt), lambda i: (0, i)),
                   pl.BlockSpec((n_exp, tt), lambda i: (0, i)),
                   pl.BlockSpec((n_exp, LANE), lambda i: (0, 0))),
        scratch_shapes=[pltpu.VMEM((n_exp, LANE), F32)],
        compiler_params=_cparams(("arbitrary",)),
        name="moe_router",
    )(u, w_pad, b_router.reshape(n_exp, 1))


def _slots_kernel(gate_ref, pos_ref, start_ref, dest_ref, gk_ref):
    pos = pos_ref[...]
    sel = pos >= 0
    n_exp, tt = pos.shape
    dest = start_ref[...] + pos
    er = lax.broadcasted_iota(I32, (n_exp, n_exp), 0)
    ec = lax.broadcasted_iota(I32, (n_exp, n_exp), 1)
    rank = _dot((ec < er).astype(BF16), sel.astype(BF16))
    gate = gate_ref[...]
    for k in range(TOP_K):
        pick = sel & (rank == float(k))
        dest_ref[k:k + 1, :] = jnp.sum(jnp.where(pick, dest, 0), axis=0, keepdims=True)
        gk_ref[k:k + 1, :] = jnp.sum(jnp.where(pick, gate, 0.0), axis=0, keepdims=True)


def _slots(gate_t, pos_t, start):
    n_exp, m = pos_t.shape
    tt = LANE
    return pl.pallas_call(
        _slots_kernel,
        out_shape=(jax.ShapeDtypeStruct((TOP_K, m), I32), jax.ShapeDtypeStruct((TOP_K, m), F32)),
        grid=(m // tt,),
        in_specs=[pl.BlockSpec((n_exp, tt), lambda i: (0, i)),
                  pl.BlockSpec((n_exp, tt), lambda i: (0, i)),
                  pl.BlockSpec((n_exp, 1), lambda i: (0, 0))],
        out_specs=(pl.BlockSpec((TOP_K, tt), lambda i: (0, i)),
                   pl.BlockSpec((TOP_K, tt), lambda i: (0, i))),
        compiler_params=_cparams(("arbitrary",)),
        name="moe_slots",
    )(gate_t, pos_t, start.reshape(n_exp, 1))


def _dispatch_kernel(meta_ref, tok_ref, tokn_ref, up_hbm, xs_ref, buf_ref, sem, *, rps, s2):
    i = pl.program_id(0)
    slot = lax.rem(i, 2)
    rows_used = meta_ref[0] * MOE_ROWS

    def issue(t_ref, sl):
        for a in range(rps // LANE):
            def start(b, c, a=a):
                src = up_hbm.at[pl.ds(pl.multiple_of(t_ref[a, b] * s2, s2), s2)]
                dst = buf_ref.at[sl, pl.ds(pl.multiple_of((a * LANE + b) * s2, s2), s2)]
                pltpu.make_async_copy(src, dst, sem.at[sl]).start()
                return c

            lax.fori_loop(0, LANE, start, 0, unroll=8)

    @pl.when((i == 0) & (rows_used > 0))
    def _():
        issue(tok_ref, slot)

    @pl.when((i + 1 < pl.num_programs(0)) & ((i + 1) * rps < rows_used))
    def _():
        issue(tokn_ref, 1 - slot)

    @pl.when(i * rps < rows_used)
    def _():
        pltpu.make_async_copy(up_hbm.at[pl.ds(0, rps * s2)], buf_ref.at[slot], sem.at[slot]).wait()
        for s in range(s2):
            w = buf_ref[slot, pl.ds(s, rps, stride=s2), :]
            xs_ref[:, s * 256:s * 256 + LANE] = pltpu.bitcast(w << 16, F32).astype(BF16)
            xs_ref[:, s * 256 + LANE:(s + 1) * 256] = pltpu.bitcast(w & jnp.uint32(0xFFFF0000), F32).astype(BF16)

    @pl.when(i * rps >= rows_used)
    def _():
        xs_ref[...] = jnp.zeros_like(xs_ref)


def _dispatch(meta, row_tok3, up2, s2):
    n_steps, sub, _ = row_tok3.shape
    rps = sub * LANE
    assert up2.shape[0] >= rps * s2
    smem = lambda f: pl.BlockSpec((None, sub, LANE), lambda i, meta: (f(i), 0, 0), memory_space=pltpu.SMEM)
    return pl.pallas_call(
        functools.partial(_dispatch_kernel, rps=rps, s2=s2),
        out_shape=jax.ShapeDtypeStruct((n_steps * rps, 256 * s2), BF16),
        grid_spec=pltpu.PrefetchScalarGridSpec(
            num_scalar_prefetch=1,
            grid=(n_steps,),
            in_specs=[smem(lambda i: i), smem(lambda i: jnp.minimum(i + 1, n_steps - 1)),
                      pl.BlockSpec(memory_space=pl.ANY)],
            out_specs=pl.BlockSpec((rps, 256 * s2), lambda i, meta: (i, 0)),
            scratch_shapes=[pltpu.VMEM((2, rps * s2, LANE), U32), pltpu.SemaphoreType.DMA((2,))]),
        compiler_params=_cparams(("arbitrary",)),
        name="moe_dispatch",
    )(meta, row_tok3, row_tok3, up2)


def _blk(i, meta_ref):
    return jnp.minimum(i, meta_ref[0] - 1)


def _expert_changed(i, meta_ref, be_ref):
    blk = _blk(i, meta_ref)
    return (i == 0) | (be_ref[blk] != be_ref[jnp.maximum(blk - 1, 0)])


def _expert_up_kernel(meta_ref, be_ref, x_ref, w1_ref, w3_ref, h_ref, wb1_ref, wb3_ref):
    i = pl.program_id(1)

    @pl.when(_expert_changed(i, meta_ref, be_ref))
    def _():
        wb1_ref[...] = w1_ref[...].astype(BF16)
        wb3_ref[...] = w3_ref[...].astype(BF16)

    @pl.when(i < meta_ref[0])
    def _():
        x = x_ref[...]
        h_ref[...] = (_silu(_dot(x, wb1_ref[...])) * _dot(x, wb3_ref[...])).astype(BF16)

    @pl.when(i >= meta_ref[0])
    def _():
        h_ref[...] = jnp.zeros_like(h_ref)


def _expert_up(meta, blk_e, xs, w1, w3, n_blk, rb):
    n_exp, d, hid = w1.shape
    th = min(hid, 512)
    wspec = pl.BlockSpec((None, d, th), lambda hh, i, meta, be: (be[_blk(i, meta)], 0, hh))
    return pl.pallas_call(
        _expert_up_kernel,
        out_shape=jax.ShapeDtypeStruct((n_blk * rb, hid), BF16),
        grid_spec=pltpu.PrefetchScalarGridSpec(
            num_scalar_prefetch=2,
            grid=(hid // th, n_blk),
            in_specs=[pl.BlockSpec((rb, d), lambda hh, i, meta, be: (_blk(i, meta), 0)),
                      wspec, wspec],
            out_specs=pl.BlockSpec((rb, th), lambda hh, i, meta, be: (i, hh)),
            scratch_shapes=[pltpu.VMEM((d, th), BF16), pltpu.VMEM((d, th), BF16)]),
        compiler_params=_cparams(("arbitrary", "arbitrary")),
        name="moe_expert_up",
    )(meta, blk_e, xs, w1, w3)


def _row_pitch(n_sub):
    return n_sub + 8


def _expert_down_kernel(meta_ref, be_ref, h_ref, w2_ref, y_ref, wb_ref, *, rb, tn, pitch):
    i = pl.program_id(0)

    @pl.when(_expert_changed(i, meta_ref, be_ref))
    def _():
        wb_ref[...] = w2_ref[...].astype(BF16)

    @pl.when(i < meta_ref[0])
    def _():
        h = h_ref[...]
        d = w2_ref.shape[1]
        for c in range(d // tn):
            y = _dot(h, wb_ref[:, c * tn:(c + 1) * tn])
            for s, w in enumerate(_pack_pairs(y.astype(BF16).astype(F32))):
                y_ref[pl.ds(c * (tn // 256) + s, rb, stride=pitch), :] = w
        for s in range(d // 256, pitch):
            y_ref[pl.ds(s, rb, stride=pitch), :] = jnp.zeros((rb, LANE), U32)

    @pl.when(i >= meta_ref[0])
    def _():
        y_ref[...] = jnp.zeros_like(y_ref)


def _expert_down(meta, blk_e, h, w2, n_blk, rb):
    n_exp, hid, d = w2.shape
    pitch = _row_pitch(d // 256)
    return pl.pallas_call(
        functools.partial(_expert_down_kernel, rb=rb, tn=min(d, 512), pitch=pitch),
        out_shape=jax.ShapeDtypeStruct((n_blk * rb * pitch, LANE), U32),
        grid_spec=pltpu.PrefetchScalarGridSpec(
            num_scalar_prefetch=2,
            grid=(n_blk,),
            in_specs=[pl.BlockSpec((rb, hid), lambda i, meta, be: (_blk(i, meta), 0)),
                      pl.BlockSpec((None, hid, d), lambda i, meta, be: (be[_blk(i, meta)], 0, 0))],
            out_specs=pl.BlockSpec((rb * pitch, LANE), lambda i, meta, be: (i, 0)),
            scratch_shapes=[pltpu.VMEM((hid, d), BF16)]),
        compiler_params=_cparams(("arbitrary",)),
        name="moe_expert_down",
    )(meta, blk_e, h, w2)


def _final_kernel(dest_ref, destn_ref, gk_ref, y_hbm, x1_ref, ysh_ref, g2_ref, lg_ref, lb_ref, o_ref,
                  buf_ref, r_ref, sem, *, tf, alpha, pitch):
    i = pl.program_id(0)
    s2 = x1_ref.shape[1] // 256
    per_blk = ROW_TILE // tf
    slot = lax.rem(i, 2)

    def issue(d_ref, step, sl):
        off = lax.rem(step, per_blk) * tf

        def body(t, c):
            for k in range(TOP_K):
                pltpu.make_async_copy(y_hbm.at[d_ref[k, off + t], pl.ds(0, s2)], buf_ref.at[sl, k, t],
                                      sem.at[sl]).start()
            return c

        lax.fori_loop(0, tf, body, 0, unroll=2)

    @pl.when(i == 0)
    def _():
        issue(dest_ref, i, slot)

    @pl.when(i + 1 < pl.num_programs(0))
    def _():
        issue(destn_ref, i + 1, 1 - slot)

    for k in range(TOP_K):
        pltpu.make_async_copy(y_hbm.at[pl.ds(0, tf), pl.ds(0, s2)], buf_ref.at[slot, k],
                              sem.at[slot]).wait()

    off = lax.rem(i, per_blk) * tf

    def token(t, c):
        lo = hi = None
        for k in range(TOP_K):
            g = gk_ref[k, off + t]
            w = buf_ref[slot, k, t]
            tlo = g * pltpu.bitcast(w << 16, F32)
            thi = g * pltpu.bitcast(w & jnp.uint32(0xFFFF0000), F32)
            lo, hi = (tlo, thi) if lo is None else (lo + tlo, hi + thi)
        base = pl.multiple_of(t * pitch, 8)
        r_ref[pl.ds(base, s2), :] = lo
        r_ref[pl.ds(base + s2, s2), :] = hi
        return c

    lax.fori_loop(0, tf, token, 0, unroll=2)
    routed = jnp.concatenate(
        [r_ref[pl.ds(half * s2 + s, tf, stride=pitch), :] for s in range(s2) for half in range(2)], axis=1)
    v = alpha * x1_ref[...] + g2_ref[...] * (routed + ysh_ref[...])
    o_ref[...] = _ln_rows(v) * lg_ref[...] + lb_ref[...]


def _final(dest3, gk3, y3, x1, ysh, mod_tiles, mod_index, tf, ln_g, ln_b, alpha):
    m, d = x1.shape
    n_rows = y3.shape[0]
    s2 = d // 256
    pitch = _row_pitch(d // LANE)
    per_blk = ROW_TILE // tf
    n_steps = m // tf
    assert dest3.shape[2] == ROW_TILE and n_rows >= tf
    row = pl.BlockSpec((tf, d), lambda i: (i, 0))
    vec = pl.BlockSpec((1, d), lambda i: (0, 0))
    smem_blk = lambda f: pl.BlockSpec((None, TOP_K, ROW_TILE), lambda i: (f(i) // per_blk, 0, 0),
                                      memory_space=pltpu.SMEM)
    return pl.pallas_call(
        functools.partial(_final_kernel, tf=tf, alpha=alpha, pitch=pitch),
        out_shape=jax.ShapeDtypeStruct((m, d), F32),
        grid=(n_steps,),
        in_specs=[smem_blk(lambda i: i), smem_blk(lambda i: jnp.minimum(i + 1, n_steps - 1)),
                  smem_blk(lambda i: i),
                  pl.BlockSpec(memory_space=pl.ANY),
                  row, row, _mod_spec(5, d, tf, mod_index), vec, vec],
        out_specs=row,
        scratch_shapes=[pltpu.VMEM((2, TOP_K, tf, s2, LANE), U32),
                        pltpu.VMEM((tf * pitch, LANE), F32),
                        pltpu.SemaphoreType.DMA((2,))],
        compiler_params=_cparams(("arbitrary",)),
        name="moe_combine_ln2",
    )(dest3, dest3, gk3, y3, x1, ysh, mod_tiles, ln_g.reshape(1, d), ln_b.reshape(1, d))


def _rope_tables(pos):
    half = HD // 2
    inv = jnp.power(ROPE_THETA, -jnp.arange(half, dtype=F32) / half)
    ang = pos.astype(F32)[:, None] * inv[None, :]
    cos, sin = jnp.cos(ang), jnp.sin(ang)
    return jnp.concatenate([cos, cos], axis=1), jnp.concatenate([-sin, sin], axis=1)


def kernel(x_prompt, x_sample, c_prompt, c_sample, cache_cmp, cache_slc, cache_win, state_gla, page_table, w_ada, b_ada, w_in, w_gla_a2, b_gla_a2, gla_norm_g, cmp_pe_k, cmp_pe_v, cmp_w_k, cmp_w_v, w_br_gla, w_br_nsa, w_out, ln1_g, ln1_b, w_router, b_router, w_exp1, w_exp3, w_exp2, w_sh1, w_sh3, w_sh2, ln2_g, ln2_b):
    depth = w_in.shape[0]
    assert depth == 1
    alpha = (2.0 * depth) ** 0.25
    nb_, s_, d = x_prompt.shape
    db_, t_, _ = x_sample.shape
    mp, ns = nb_ * s_, db_ * t_
    m = mp + ns
    assert ns % ROW_TILE == 0 and s_ % ROW_TILE == 0 and d % 256 == 0
    n_pool, page = cache_cmp.shape[1], cache_cmp.shape[2]
    n_pages = page_table.shape[1]
    past_len = n_pages * page
    kvw2 = 2 * NSA_G * HD
    src, order, dst, n_used, _ = _col_plan(d)
    tn = 512
    small_off = dst['gla_a']
    dst = dict(dst)
    dst['nsa_g'] = small_off + GLA_RANK
    n_p = -(-(small_off + LANE) // tn) * tn

    rc = -(-(nb_ + db_) // 8) * 8
    c_all = jnp.pad(jnp.concatenate([c_prompt, c_sample], axis=0), ((0, rc - nb_ - db_), (0, 0)))
    mod = _ada(c_all, w_ada[0], b_ada[0])
    mod_tiles = jnp.concatenate(
        [jnp.broadcast_to(mod[:nb_, None, :], (nb_, ROW_TILE, 6 * d)),
         jnp.repeat(mod[nb_:nb_ + db_], t_, axis=0).reshape(ns // ROW_TILE, ROW_TILE, 6 * d)], axis=0)
    mod_index = _mod_index(ROW_TILE, mp, s_, nb_)

    x_all = jnp.concatenate([x_prompt.reshape(mp, d), x_sample.reshape(ns, d)], axis=0)
    u1 = _ln_mod(x_all, mod_tiles, mod_index)

    segments = []
    for name in order[:-2]:
        lo, shift = dst[name], src[name][0] - dst[name]
        if segments and segments[-1][2] == shift and segments[-1][1] == lo:
            segments[-1] = (segments[-1][0], lo + src[name][1], shift)
        else:
            segments.append((lo, lo + src[name][1], shift))
    a_src, g_src = src['gla_a'][0], src['nsa_g'][0]
    assert a_src % LANE == 0 and g_src % LANE == GLA_RANK
    small = (small_off, (a_src // LANE, GLA_RANK), (g_src // LANE, GLA_RANK + src['nsa_g'][1]))
    pos_all = jnp.concatenate([jnp.tile(jnp.arange(s_, dtype=I32), nb_),
                               jnp.tile(past_len + jnp.arange(t_, dtype=I32), db_)])
    cos_t, sin_t = _rope_tables(pos_all)
    tm = _divisor_tile(m, 1056, 16)
    rope_cols = ((dst['nsa_q'], dst['nsa_q'] + NSA_HEADS * HD),
                 (dst['nsa_ks'], dst['nsa_ks'] + NSA_G * HD),
                 (dst['nsa_kw'], dst['nsa_kw'] + NSA_G * HD))
    parts = _inproj(u1, w_in[0], cos_t, sin_t, rope_cols, tuple(segments), small, n_p, tm, tn)

    zeros_state = jnp.zeros((nb_, GLA_HEADS, GLA_DK, GLA_DV), F32)
    og_p, gla_p = _gla(parts, nb_, s_, s_, 512, 128, dst, zeros_state, w_gla_a2[0], b_gla_a2[0],
                       gla_norm_g[0])
    t_pad = 16
    gla_cols = dst['nsa_q']
    smp = parts[mp:].reshape(db_, t_, n_p)
    smp_pad = jnp.pad(smp, ((0, 0), (0, t_pad - t_), (0, 0))).reshape(db_ * t_pad, n_p)
    og_s, gla_s = _gla(smp_pad, db_, t_pad, t_, t_pad, t_pad, dst, state_gla[0], w_gla_a2[0],
                       b_gla_a2[0], gla_norm_g[0])
    og_s = og_s.reshape(db_, t_pad, -1)[:, :t_].reshape(ns, -1)
    og = jnp.concatenate([og_p, og_s], axis=0)

    cmp_pos_p = jnp.tile((jnp.arange(s_ // CMP_BLOCK, dtype=I32) + 1) * CMP_BLOCK - 1, nb_)
    cos_c, sin_c = _rope_tables(cmp_pos_p)
    kvb_p = _compress_prompt(parts, mp, dst, cmp_w_k[0], cmp_w_v[0], cmp_pe_k[0], cmp_pe_v[0], cos_c, sin_c)
    on_p = _nsa_prompt(parts, kvb_p, nb_, s_, dst)

    cmp_pos_s = (jnp.arange(past_len // CMP_BLOCK, dtype=I32) + 1) * CMP_BLOCK - 1
    cos_cs, sin_cs = _rope_tables(cmp_pos_s)
    is_key = (jnp.arange(KV_CH) < NSA_G)[None, :, None]
    cos_e = jnp.where(is_key, cos_cs[:, None, :], 1.0).reshape(-1, HD)
    sin_e = jnp.where(is_key, sin_cs[:, None, :], 0.0).reshape(-1, HD)
    kvb_s = _compress_pages(cache_cmp.reshape(-1, HD), page, page_table, cmp_w_k[0], cmp_w_v[0],
                            cmp_pe_k[0], cmp_pe_v[0], cos_e, sin_e)
    tq_pad = 8
    pad_t = lambda a: jnp.pad(a, ((0, 0), (0, tq_pad - t_), (0, 0)))
    q_s = pad_t(smp[:, :, dst['nsa_q']:dst['nsa_q'] + NSA_HEADS * HD])
    gt_s = pad_t(smp[:, :, small_off:small_off + LANE])
    new_s = pad_t(smp[:, :, dst['nsa_ks']:dst['nsa_ks'] + 2 * kvw2])
    on_s = _nsa_sample(q_s, gt_s, kvb_s, cache_slc.reshape(-1, HD), page, cache_win.reshape(-1, HD),
                       cache_win.shape[2], new_s, page_table, t_)
    on = jnp.concatenate([on_p, on_s[:, :t_].reshape(ns, -1).astype(BF16)], axis=0)

    z = _merge(og, on, w_br_gla[0], w_br_nsa[0], parts, dst['mg_gla'], dst['mg_nsa'], tm, tn)
    y = _plain(z, w_out[0], tm, tn, "out_proj")
    x1, u2, up = _mid(x_all, y, mod_tiles, mod_index, ln1_g[0], ln1_b[0], alpha)

    n_exp = w_router.shape[2]
    rb = MOE_ROWS
    s2 = d // 256
    gate_t, pos_t, cnt = _router(u2, w_router[0], b_router[0])
    counts = cnt[:, 0]
    padded = (counts + rb - 1) // rb * rb
    pend = jnp.cumsum(padded)
    start = pend - padded
    n_blk = -(-(m * TOP_K + n_exp * (rb - 1)) // rb)
    rps = 2 * rb
    n_blk = -(-(m * TOP_K + n_exp * (rb - 1)) // rps) * (rps // rb)
    blk_first = jnp.arange(n_blk, dtype=I32) * rb
    blk_e = jnp.minimum(jnp.sum(pend[None, :] <= blk_first[:, None], axis=1), n_exp - 1).astype(I32)
    meta = (pend[-1:] // rb).astype(I32)
    dest, gk = _slots(gate_t, pos_t, start.astype(I32))
    nt = m // ROW_TILE
    as_blocks = lambda a: a.reshape(TOP_K, nt, ROW_TILE).transpose(1, 0, 2)
    dest3, gk3 = as_blocks(dest), as_blocks(gk)
    row_tok = jnp.zeros((n_blk * rb,), I32).at[dest.reshape(-1)].set(
        jnp.tile(jnp.arange(m, dtype=I32), TOP_K))
    xs = _dispatch(meta, row_tok.reshape(-1, rps // LANE, LANE), up, s2)
    h = _expert_up(meta, blk_e, xs, w_exp1[0], w_exp3[0], n_blk, rb)
    y2 = _expert_down(meta, blk_e, h, w_exp2[0], n_blk, rb)
    hs = _swiglu(u2, w_sh1[0], w_sh3[0], tm, tn // 2)
    ysh = _plain(hs, w_sh2[0], tm, tn, "shared_down")
    tf = 64
    out = _final(dest3, gk3, y2.reshape(n_blk * rb, -1, LANE), x1, ysh, mod_tiles,
                 _mod_index(tf, mp, s_, nb_), tf, ln2_g[0], ln2_b[0], alpha)

    def rows(lo, n_rows, name, lead):
        return parts[lo:lo + n_rows, dst[name]:dst[name] + kvw2].reshape(lead + (2, NSA_G, HD))

    win_buf = cache_win.shape[2]
    win_p = rows(0, mp, 'nsa_kw', (nb_, s_))[:, s_ - win_buf:]
    win_s = jnp.concatenate([cache_win[0], rows(mp, ns, 'nsa_kw', (db_, t_))], axis=1)[:, t_:]
    return (out[:mp].reshape(nb_, s_, d), out[mp:].reshape(db_, t_, d),
            rows(0, mp, 'nsa_kc', (nb_, s_))[None], rows(mp, ns, 'nsa_kc', (db_, t_))[None],
            rows(0, mp, 'nsa_ks', (nb_, s_))[None], rows(mp, ns, 'nsa_ks', (db_, t_))[None],
            win_p[None], win_s[None], gla_p[None], gla_s[None])
```

```python
import functools
import math

import jax
import jax.numpy as jnp
import numpy as np
from jax import lax
from jax.experimental import pallas as pl
from jax.experimental.pallas import tpu as pltpu

F32, BF16, I32, U32 = jnp.float32, jnp.bfloat16, jnp.int32, jnp.uint32
HIGHEST = lax.Precision.HIGHEST

GLA_HEADS, GLA_DK, GLA_DV, GLA_RANK, GLA_GATE_NORM = 8, 128, 256, 16, 16.0
NSA_HEADS, NSA_G, NSA_R, HD = 16, 4, 4, 128
KV_CH = 2 * NSA_G
CMP_BLOCK, SEL_BLOCK, N_SEL, WINDOW = 32, 64, 16, 512
ROPE_THETA = 10000.0
TOP_K, N_GROUPS, TOPK_GROUPS, ROUTED_SCALE = 8, 8, 4, 2.5
LN_EPS = 1e-5
NEG = -1e30

LANE = 128
VMEM_LIMIT = 56 * 1024 * 1024
ROW_TILE = 128
MOE_ROWS = 256
GLA_SUB = 32


def _cparams(sem):
    return pltpu.CompilerParams(dimension_semantics=sem, vmem_limit_bytes=VMEM_LIMIT)


def _divisor_tile(n, target, mult):
    best = None
    for d in range(mult, min(n, target) + 1, mult):
        if n % d == 0:
            best = d
    assert best is not None, (n, target, mult)
    return best


def _silu(x):
    return x * jax.nn.sigmoid(x)


def _ln_rows(x):
    mu = jnp.mean(x, axis=-1, keepdims=True)
    xc = x - mu
    var = jnp.mean(xc * xc, axis=-1, keepdims=True)
    return xc * lax.rsqrt(var + LN_EPS)


def _dot(a, b):
    return jnp.dot(a, b, preferred_element_type=F32)


def _dot_nt(a, b, precision=None):
    return lax.dot_general(a, b, (((1,), (1,)), ((), ())), precision=precision,
                           preferred_element_type=F32)


def _dot_tn(a, b, precision=None):
    return lax.dot_general(a, b, (((0,), (0,)), ((), ())), precision=precision,
                           preferred_element_type=F32)


def _col_plan(d_model):
    gqk, gv = GLA_HEADS * GLA_DK, GLA_HEADS * GLA_DV
    nq, nkv = NSA_HEADS * HD, NSA_G * HD
    ref = (('gla_q', gqk), ('gla_k', gqk), ('gla_v', gv), ('gla_r', gv), ('gla_a', GLA_RANK),
           ('nsa_q', nq), ('nsa_kc', nkv), ('nsa_vc', nkv), ('nsa_ks', nkv), ('nsa_vs', nkv),
           ('nsa_kw', nkv), ('nsa_vw', nkv), ('nsa_g', NSA_HEADS * 3), ('mg_gla', d_model),
           ('mg_nsa', d_model))
    src, o = {}, 0
    for n, w in ref:
        src[n] = (o, w)
        o += w
    order = ('gla_q', 'gla_k', 'gla_v', 'gla_r', 'nsa_q', 'nsa_kc', 'nsa_vc', 'nsa_ks', 'nsa_vs',
             'nsa_kw', 'nsa_vw', 'mg_gla', 'mg_nsa', 'gla_a', 'nsa_g')
    dst, o = {}, 0
    for n in order:
        dst[n] = o
        o += src[n][1]
    return src, order, dst, o, o


def _ada_kernel(c_ref, w_ref, b_ref, o_ref):
    a = _silu(c_ref[...]).astype(BF16)
    o_ref[...] = _dot(a, w_ref[...].astype(BF16)) + b_ref[...]


def _ada(c_all, w_ada, b_ada):
    rc, d = c_all.shape
    n = w_ada.shape[1]
    tn = 512
    return pl.pallas_call(
        _ada_kernel,
        out_shape=jax.ShapeDtypeStruct((rc, n), F32),
        grid=(n // tn,),
        in_specs=[pl.BlockSpec((rc, d), lambda j: (0, 0)),
                  pl.BlockSpec((d, tn), lambda j: (0, j)),
                  pl.BlockSpec((1, tn), lambda j: (0, j))],
        out_specs=pl.BlockSpec((rc, tn), lambda j: (0, j)),
        compiler_params=_cparams(("arbitrary",)),
        name="ada_mod",
    )(c_all, w_ada, b_ada.reshape(1, n))


def _mod_index(rows, mp, seq, nseq):
    n_pt, per_seq, per_grp = mp // rows, seq // rows, ROW_TILE // rows

    def index(i):
        j = jnp.maximum(i - n_pt, 0)
        return (jnp.where(i < n_pt, i // per_seq, nseq + j // per_grp),
                jnp.where(i < n_pt, 0, j % per_grp))

    return index


def _mod_spec(k, d, rows, mod_index):
    return pl.BlockSpec((None, rows, d), lambda i: (*mod_index(i), k))


def _ln_mod_kernel(x_ref, sh_ref, sc_ref, u_ref):
    y = _ln_rows(x_ref[...])
    u_ref[...] = (y * (1.0 + sc_ref[...]) + sh_ref[...]).astype(BF16)


def _ln_mod(x_all, mod_tiles, mod_index):
    m, d = x_all.shape
    return pl.pallas_call(
        _ln_mod_kernel,
        out_shape=jax.ShapeDtypeStruct((m, d), BF16),
        grid=(m // ROW_TILE,),
        in_specs=[pl.BlockSpec((ROW_TILE, d), lambda i: (i, 0)),
                  _mod_spec(0, d, ROW_TILE, mod_index), _mod_spec(1, d, ROW_TILE, mod_index)],
        out_specs=pl.BlockSpec((ROW_TILE, d), lambda i: (i, 0)),
        compiler_params=_cparams(("arbitrary",)),
        name="ln_mod1",
    )(x_all, mod_tiles, mod_tiles)


def _pack_pairs(u):
    bits = pltpu.bitcast(u, U32)
    out = []
    for s in range(u.shape[1] // 256):
        lo = bits[:, s * 256:s * 256 + 128]
        hi = bits[:, s * 256 + 128:s * 256 + 256]
        out.append((lo >> 16) | (hi & jnp.uint32(0xFFFF0000)))
    return out


def _mid_kernel(x_ref, y_ref, g1_ref, sh_ref, sc_ref, lg_ref, lb_ref, x1_ref, u_ref, up_ref, *, alpha):
    v = alpha * x_ref[...] + g1_ref[...] * y_ref[...]
    x1 = _ln_rows(v) * lg_ref[...] + lb_ref[...]
    x1_ref[...] = x1
    u = _ln_rows(x1) * (1.0 + sc_ref[...]) + sh_ref[...]
    ub = u.astype(BF16)
    u_ref[...] = ub
    words = _pack_pairs(ub.astype(F32))
    s2 = len(words)
    for s, w in enumerate(words):
        up_ref[pl.ds(s, ROW_TILE, stride=s2), :] = w


def _mid(x_all, y, mod_tiles, mod_index, ln_g, ln_b, alpha):
    m, d = x_all.shape
    s2 = d // 256
    row = pl.BlockSpec((ROW_TILE, d), lambda i: (i, 0))
    vec = pl.BlockSpec((1, d), lambda i: (0, 0))
    return pl.pallas_call(
        functools.partial(_mid_kernel, alpha=alpha),
        out_shape=(jax.ShapeDtypeStruct((m, d), F32), jax.ShapeDtypeStruct((m, d), BF16),
                   jax.ShapeDtypeStruct((m * s2, LANE), U32)),
        grid=(m // ROW_TILE,),
        in_specs=[row, row, _mod_spec(2, d, ROW_TILE, mod_index), _mod_spec(3, d, ROW_TILE, mod_index),
                  _mod_spec(4, d, ROW_TILE, mod_index), vec, vec],
        out_specs=(row, row, pl.BlockSpec((ROW_TILE * s2, LANE), lambda i: (i, 0))),
        compiler_params=_cparams(("arbitrary",)),
        name="ln1_mod2",
    )(x_all, y, mod_tiles, mod_tiles, mod_tiles, ln_g.reshape(1, d), ln_b.reshape(1, d))


def _cast_weights(i, pairs):
    @pl.when(i == 0)
    def _():
        for w_ref, wb_ref in pairs:
            wb_ref[...] = w_ref[...].astype(BF16)


def _inproj_kernel(mb_ref, xb_ref, u_ref, w_ref, x1_ref, x2_ref, cos_ref, sin_ref, o_ref, wb_ref,
                   *, tn, rope_patterns, shift_tiles, small_tile, small_split):
    j, i = pl.program_id(0), pl.program_id(1)
    d = w_ref.shape[0]
    rows = 512

    @pl.when(i == 0)
    def _():
        for shift, (lo, hi) in shift_tiles:
            @pl.when((j >= lo) & (j < hi))
            def _(shift=shift):
                for r0 in range(0, d, rows):
                    if shift == 0:
                        wb_ref[r0:r0 + rows, :] = w_ref[r0:r0 + rows, :].astype(BF16)
                    else:
                        cat = jnp.concatenate([w_ref[r0:r0 + rows, :], x1_ref[r0:r0 + rows, :]], axis=1)
                        wb_ref[r0:r0 + rows, :] = pltpu.roll(cat, tn + LANE - shift, axis=1)[:, :tn].astype(BF16)

        @pl.when(j == small_tile)
        def _():
            wb_ref[...] = jnp.zeros_like(wb_ref)
            lane = lax.broadcasted_iota(I32, (d, LANE), 1)
            wb_ref[:, :LANE] = jnp.where(lane < small_split[0], x1_ref[...],
                                         jnp.where(lane < small_split[1], x2_ref[...], 0.0)).astype(BF16)

    acc = _dot(u_ref[...], wb_ref[...])
    plain = True
    for flags, tiles in rope_patterns:
        here = functools.reduce(jnp.logical_or, [j == t for t in tiles])
        plain = jnp.logical_and(plain, jnp.logical_not(here))

        @pl.when(here)
        def _(flags=flags):
            cos, sin = cos_ref[...], sin_ref[...]
            for h, rotate in enumerate(flags):
                a = acc[:, h * HD:(h + 1) * HD]
                o_ref[:, h * HD:(h + 1) * HD] = (a * cos + pltpu.roll(a, HD // 2, axis=1) * sin) if rotate else a

    @pl.when(plain)
    def _():
        o_ref[...] = acc


def _inproj(u, w_src, cos_t, sin_t, rope_cols, segments, small, n_p, tm, tn):
    m, d = u.shape
    n_tiles = n_p // tn
    by_flags = {}
    for t in range(n_tiles):
        flags = tuple(any(lo <= t * tn + h * HD < hi for lo, hi in rope_cols) for h in range(tn // HD))
        if any(flags):
            by_flags.setdefault(flags, []).append(t)
    rope_patterns = tuple((f, tuple(ts)) for f, ts in by_flags.items())
    small_off, (blk1, end1), (blk2, end2) = small
    assert small_off % tn == 0 and small_off // tn == n_tiles - 1
    assert all(lo % tn == 0 and hi % tn == 0 and 0 <= sh < LANE for lo, hi, sh in segments)
    main_blk = np.arange(n_tiles, dtype=np.int32)
    main_blk[-1] = n_tiles - 2
    last_x = (w_src.shape[1] - 1) // LANE
    x1_blk = np.minimum((np.arange(n_tiles, dtype=np.int32) + 1) * (tn // LANE), last_x)
    x1_blk[-1] = blk1
    shift_tiles = tuple((sh, (lo // tn, hi // tn)) for lo, hi, sh in segments)
    return pl.pallas_call(
        functools.partial(_inproj_kernel, tn=tn, rope_patterns=rope_patterns, shift_tiles=shift_tiles,
                          small_tile=n_tiles - 1, small_split=(end1, end2)),
        out_shape=jax.ShapeDtypeStruct((m, n_p), F32),
        grid_spec=pltpu.PrefetchScalarGridSpec(
            num_scalar_prefetch=2,
            grid=(n_tiles, m // tm),
            in_specs=[pl.BlockSpec((tm, d), lambda j, i, mb, xb: (i, 0)),
                      pl.BlockSpec((d, tn), lambda j, i, mb, xb: (0, mb[j])),
                      pl.BlockSpec((d, LANE), lambda j, i, mb, xb: (0, xb[j])),
                      pl.BlockSpec((d, LANE), lambda j, i, mb, xb: (0, blk2)),
                      pl.BlockSpec((tm, HD), lambda j, i, mb, xb: (i, 0)),
                      pl.BlockSpec((tm, HD), lambda j, i, mb, xb: (i, 0))],
            out_specs=pl.BlockSpec((tm, tn), lambda j, i, mb, xb: (i, j)),
            scratch_shapes=[pltpu.VMEM((d, tn), BF16)]),
        compiler_params=_cparams(("arbitrary", "arbitrary")),
        name="in_proj",
    )(jnp.asarray(main_blk), jnp.asarray(x1_blk), u, w_src, w_src, w_src, cos_t, sin_t)


def _merge_kernel(a1_ref, a2_ref, w1_ref, w2_ref, g1_ref, g2_ref, o_ref, wb1_ref, wb2_ref):
    _cast_weights(pl.program_id(1), ((w1_ref, wb1_ref), (w2_ref, wb2_ref)))
    z = (jax.nn.sigmoid(g1_ref[...]) * _dot(a1_ref[...], wb1_ref[...])
         + jax.nn.sigmoid(g2_ref[...]) * _dot(a2_ref[...], wb2_ref[...]))
    o_ref[...] = z.astype(BF16)


def _merge(og, on, w1, w2, parts, off1, off2, tm, tn):
    m, k = og.shape
    n = w1.shape[1]
    return pl.pallas_call(
        _merge_kernel,
        out_shape=jax.ShapeDtypeStruct((m, n), BF16),
        grid=(n // tn, m // tm),
        in_specs=[pl.BlockSpec((tm, k), lambda j, i: (i, 0)),
                  pl.BlockSpec((tm, k), lambda j, i: (i, 0)),
                  pl.BlockSpec((k, tn), lambda j, i: (0, j)),
                  pl.BlockSpec((k, tn), lambda j, i: (0, j)),
                  pl.BlockSpec((tm, tn), lambda j, i: (i, off1 // tn + j)),
                  pl.BlockSpec((tm, tn), lambda j, i: (i, off2 // tn + j))],
        out_specs=pl.BlockSpec((tm, tn), lambda j, i: (i, j)),
        scratch_shapes=[pltpu.VMEM((k, tn), BF16), pltpu.VMEM((k, tn), BF16)],
        compiler_params=_cparams(("arbitrary", "arbitrary")),
        name="branch_merge",
    )(og, on, w1, w2, parts, parts)


def _plain_kernel(a_ref, w_ref, o_ref, wb_ref):
    _cast_weights(pl.program_id(1), ((w_ref, wb_ref),))
    o_ref[...] = _dot(a_ref[...], wb_ref[...])


def _plain(a, w, tm, tn, name):
    m, k = a.shape
    n = w.shape[1]
    return pl.pallas_call(
        _plain_kernel,
        out_shape=jax.ShapeDtypeStruct((m, n), F32),
        grid=(n // tn, m // tm),
        in_specs=[pl.BlockSpec((tm, k), lambda j, i: (i, 0)),
                  pl.BlockSpec((k, tn), lambda j, i: (0, j))],
        out_specs=pl.BlockSpec((tm, tn), lambda j, i: (i, j)),
        scratch_shapes=[pltpu.VMEM((k, tn), BF16)],
        compiler_params=_cparams(("arbitrary", "arbitrary")),
        name=name,
    )(a, w)


def _swiglu_kernel(a_ref, w1_ref, w3_ref, o_ref, wb1_ref, wb3_ref):
    _cast_weights(pl.program_id(1), ((w1_ref, wb1_ref), (w3_ref, wb3_ref)))
    a = a_ref[...]
    o_ref[...] = (_silu(_dot(a, wb1_ref[...])) * _dot(a, wb3_ref[...])).astype(BF16)


def _swiglu(a, w1, w3, tm, tn):
    m, k = a.shape
    n = w1.shape[1]
    return pl.pallas_call(
        _swiglu_kernel,
        out_shape=jax.ShapeDtypeStruct((m, n), BF16),
        grid=(n // tn, m // tm),
        in_specs=[pl.BlockSpec((tm, k), lambda j, i: (i, 0)),
                  pl.BlockSpec((k, tn), lambda j, i: (0, j)),
                  pl.BlockSpec((k, tn), lambda j, i: (0, j))],
        out_specs=pl.BlockSpec((tm, tn), lambda j, i: (i, j)),
        scratch_shapes=[pltpu.VMEM((k, tn), BF16), pltpu.VMEM((k, tn), BF16)],
        compiler_params=_cparams(("arbitrary", "arbitrary")),
        name="shared_swiglu",
    )(a, w1, w3)


def _gla_kernel(q_ref, k_ref, v_ref, r_ref, a_ref, wa_ref, ba_ref, ng_ref, s0_ref,
                og_ref, sf_ref, st_ref, *, tb, chunk, s_valid):
    i = pl.program_id(2)

    @pl.when(i == 0)
    def _():
        st_ref[...] = s0_ref[...]

    x = jnp.dot(a_ref[:, :GLA_RANK], wa_ref[...], precision=HIGHEST,
                preferred_element_type=F32) + ba_ref[...]
    glog = (jnp.minimum(x, 0.0) - jnp.log1p(jnp.exp(-jnp.abs(x)))) * (1.0 / GLA_GATE_NORM)
    row = i * tb + lax.broadcasted_iota(I32, (tb, 1), 0)
    glog = jnp.where(row < s_valid, glog, 0.0)

    sub = min(GLA_SUB, chunk)
    rr = lax.broadcasted_iota(I32, (chunk, chunk), 0)
    cc = lax.broadcasted_iota(I32, (chunk, chunk), 1)
    tri = (rr >= cc).astype(F32)
    ones = jnp.ones((chunk, GLA_DK), F32)
    ng = ng_ref[...]

    for c in range(tb // chunk):
        lo = c * chunk
        g = glog[lo:lo + chunk]
        bc = jnp.dot(tri, g, precision=HIGHEST, preferred_element_type=F32)
        bl = bc[chunk - 1:chunk]
        q = q_ref[lo:lo + chunk, :] * (GLA_DK ** -0.5)
        k = k_ref[lo:lo + chunk, :]
        vb = v_ref[lo:lo + chunk, :].astype(BF16)
        qg = (q * jnp.exp(bc)).astype(BF16)
        kd = (k * jnp.exp(bl - bc)).astype(BF16)
        state = st_ref[...]
        o_inter = _dot(qg, state.astype(BF16))
        outs = []
        for sb in range(chunk // sub):
            r0, r1 = sb * sub, (sb + 1) * sub
            base = bc[r0 - 1:r0] if sb > 0 else jnp.zeros((1, GLA_DK), F32)
            qs = (q[r0:r1] * jnp.exp(bc[r0:r1] - base)).astype(BF16)
            ks = (k[:r1] * jnp.exp(base - bc[:r1])).astype(BF16)
            att = _dot_nt(qs, ks)
            causal = (r0 + lax.broadcasted_iota(I32, (sub, r1), 0)) >= lax.broadcasted_iota(I32, (sub, r1), 1)
            att = jnp.where(causal, att, 0.0)
            outs.append(_dot(att.astype(BF16), vb[:r1]))
        o = jnp.concatenate(outs, axis=0) + o_inter
        dec = jnp.exp(_dot_tn(g, ones, precision=HIGHEST))
        dec = jnp.concatenate([dec] * (GLA_DV // GLA_DK), axis=1)
        st_ref[...] = dec * state + _dot_tn(kd, vb)
        o = o * lax.rsqrt(jnp.mean(o * o, axis=-1, keepdims=True) + LN_EPS) * ng
        og_ref[lo:lo + chunk, :] = (o * _silu(r_ref[lo:lo + chunk, :])).astype(BF16)

    @pl.when(i == pl.num_programs(2) - 1)
    def _():
        sf_ref[...] = st_ref[...]


def _gla(src, nseq, s_pad, s_valid, tb, chunk, dst, s0, w_a2, b_a2, norm_g):
    nb = s_pad // tb
    rows = nseq * s_pad
    qo, ko = dst['gla_q'] // GLA_DK, dst['gla_k'] // GLA_DK
    vo, ro = dst['gla_v'] // GLA_DV, dst['gla_r'] // GLA_DV
    ao = dst['gla_a'] // LANE
    rowblk = lambda b, h, i: b * nb + i
    st_spec = pl.BlockSpec((None, None, GLA_DK, GLA_DV), lambda b, h, i: (b, h, 0, 0))
    return pl.pallas_call(
        functools.partial(_gla_kernel, tb=tb, chunk=chunk, s_valid=s_valid),
        out_shape=(jax.ShapeDtypeStruct((rows, GLA_HEADS * GLA_DV), BF16),
                   jax.ShapeDtypeStruct((nseq, GLA_HEADS, GLA_DK, GLA_DV), F32)),
        grid=(nseq, GLA_HEADS, nb),
        in_specs=[pl.BlockSpec((tb, GLA_DK), lambda b, h, i: (rowblk(b, h, i), qo + h)),
                  pl.BlockSpec((tb, GLA_DK), lambda b, h, i: (rowblk(b, h, i), ko + h)),
                  pl.BlockSpec((tb, GLA_DV), lambda b, h, i: (rowblk(b, h, i), vo + h)),
                  pl.BlockSpec((tb, GLA_DV), lambda b, h, i: (rowblk(b, h, i), ro + h)),
                  pl.BlockSpec((tb, LANE), lambda b, h, i: (rowblk(b, h, i), ao)),
                  pl.BlockSpec((GLA_RANK, GLA_DK), lambda b, h, i: (0, h)),
                  pl.BlockSpec((1, GLA_DK), lambda b, h, i: (0, h)),
                  pl.BlockSpec((1, GLA_DV), lambda b, h, i: (0, 0)),
                  st_spec],
        out_specs=(pl.BlockSpec((tb, GLA_DV), lambda b, h, i: (rowblk(b, h, i), h)), st_spec),
        scratch_shapes=[pltpu.VMEM((GLA_DK, GLA_DV), F32)],
        compiler_params=_cparams(("arbitrary", "arbitrary", "arbitrary")),
        name="gla_scan",
    )(src, src, src, src, src, w_a2, b_a2.reshape(1, -1), norm_g.reshape(1, -1), s0)


def _rope_heads(x, cos, sin):
    return jnp.concatenate(
        [x[:, h * HD:(h + 1) * HD] * cos + pltpu.roll(x[:, h * HD:(h + 1) * HD], HD // 2, axis=1) * sin
         for h in range(x.shape[1] // HD)], axis=1)


def _compress_rows(x, wk_ref, wv_ref, pek_ref, pev_ref, cos, sin):
    rows, width = x.shape
    nb, half = rows // CMP_BLOCK, width // 2

    def softmax_col(w_ref):
        w = w_ref[...]
        e = jnp.exp(w - jnp.max(w, axis=0, keepdims=True))
        return e / jnp.sum(e, axis=0, keepdims=True)

    wk, wv = softmax_col(wk_ref), softmax_col(wv_ref)
    w2 = jnp.concatenate([jnp.broadcast_to(wk, (CMP_BLOCK, half)),
                          jnp.broadcast_to(wv, (CMP_BLOCK, half))], axis=1)
    y = jnp.sum(x.reshape(nb, CMP_BLOCK, width) * w2[None], axis=1)
    pk = jnp.sum(pek_ref[...] * wk, axis=0, keepdims=True)
    pv = jnp.sum(pev_ref[...] * wv, axis=0, keepdims=True)
    kb = y[:, :half] + jnp.concatenate([pk] * NSA_G, axis=1)
    vb = y[:, half:] + jnp.concatenate([pv] * NSA_G, axis=1)
    return jnp.concatenate([_rope_heads(kb, cos, sin), vb], axis=1)


def _compress_prompt_kernel(x_ref, wk_ref, wv_ref, pek_ref, pev_ref, cos_ref, sin_ref, o_ref):
    o_ref[...] = _compress_rows(x_ref[...], wk_ref, wv_ref, pek_ref, pev_ref, cos_ref[...], sin_ref[...])


def _small_specs(nargs):
    col = pl.BlockSpec((CMP_BLOCK, 1), lambda *a: (0, 0))
    pe = pl.BlockSpec((CMP_BLOCK, HD), lambda *a: (0, 0))
    return [col, col, pe, pe]


def _compress_prompt(parts, rows, dst, w_k, w_v, pe_k, pe_v, cos_c, sin_c):
    width = 2 * NSA_G * HD
    tr = _divisor_tile(rows, 1024, 256)
    nbt = tr // CMP_BLOCK
    return pl.pallas_call(
        _compress_prompt_kernel,
        out_shape=jax.ShapeDtypeStruct((rows // CMP_BLOCK, width), F32),
        grid=(rows // tr,),
        in_specs=[pl.BlockSpec((tr, width), lambda i: (i, dst['nsa_kc'] // width))] + _small_specs(1)
        + [pl.BlockSpec((nbt, HD), lambda i: (i, 0)), pl.BlockSpec((nbt, HD), lambda i: (i, 0))],
        out_specs=pl.BlockSpec((nbt, width), lambda i: (i, 0)),
        compiler_params=_cparams(("arbitrary",)),
        name="nsa_compress_prompt",
    )(parts, w_k.reshape(-1, 1), w_v.reshape(-1, 1), pe_k, pe_v, cos_c, sin_c)


def _compress_pages_kernel(pt_ref, *refs, pg):
    page_refs = refs[:pg]
    wk_ref, wv_ref, pek_ref, pev_ref, cos_ref, sin_ref, o_ref = refs[pg:]

    def softmax_col(w_ref):
        w = w_ref[...]
        e = jnp.exp(w - jnp.max(w, axis=0, keepdims=True))
        return e / jnp.sum(e, axis=0, keepdims=True)

    wk, wv = softmax_col(wk_ref), softmax_col(wv_ref)
    blk_rows = CMP_BLOCK * KV_CH
    onehot = ((lax.broadcasted_iota(I32, (blk_rows, CMP_BLOCK), 0) >> 3)
              == lax.broadcasted_iota(I32, (blk_rows, CMP_BLOCK), 1)).astype(F32)
    spread = lambda w: jnp.dot(onehot, jnp.broadcast_to(w, (CMP_BLOCK, HD)), precision=HIGHEST,
                               preferred_element_type=F32)
    is_key = (lax.broadcasted_iota(I32, (blk_rows, HD), 0) & (KV_CH - 1)) < NSA_G
    wexp = jnp.where(is_key, spread(wk), spread(wv))
    pk = jnp.sum(pek_ref[...] * wk, axis=0, keepdims=True)
    pv = jnp.sum(pev_ref[...] * wv, axis=0, keepdims=True)
    pe8 = jnp.where(lax.broadcasted_iota(I32, (KV_CH, HD), 0) < NSA_G, pk, pv)
    outs = []
    for p in range(pg):
        x = page_refs[p][...]
        for n in range(x.shape[0] // blk_rows):
            xw = x[n * blk_rows:(n + 1) * blk_rows] * wexp
            acc = pe8
            for l in range(CMP_BLOCK):
                acc = acc + xw[l * KV_CH:(l + 1) * KV_CH]
            outs.append(acc)
    y = jnp.concatenate(outs, axis=0)
    o_ref[...] = y * cos_ref[...] + pltpu.roll(y, HD // 2, axis=1) * sin_ref[...]


def _compress_pages(cache2, page, page_table, w_k, w_v, pe_k, pe_v, cos_e, sin_e):
    nseq, n_pages = page_table.shape
    pg = _divisor_tile(n_pages, 8, 1)
    out_rows = page // CMP_BLOCK * KV_CH
    page_specs = [pl.BlockSpec((page * KV_CH, HD),
                               (lambda b, i, pt, p=p: (pt[b * n_pages + i * pg + p], 0)))
                  for p in range(pg)]
    small = [pl.BlockSpec((CMP_BLOCK, 1), lambda b, i, pt: (0, 0))] * 2 \
        + [pl.BlockSpec((CMP_BLOCK, HD), lambda b, i, pt: (0, 0))] * 2
    tab = [pl.BlockSpec((pg * out_rows, HD), lambda b, i, pt: (i, 0))] * 2
    return pl.pallas_call(
        functools.partial(_compress_pages_kernel, pg=pg),
        out_shape=jax.ShapeDtypeStruct((nseq, n_pages * out_rows, HD), F32),
        grid_spec=pltpu.PrefetchScalarGridSpec(
            num_scalar_prefetch=1,
            grid=(nseq, n_pages // pg),
            in_specs=page_specs + small + tab,
            out_specs=pl.BlockSpec((None, pg * out_rows, HD), lambda b, i, pt: (b, i, 0))),
        compiler_params=_cparams(("arbitrary", "arbitrary")),
        name="nsa_compress_pages",
    )(page_table.reshape(-1), *([cache2] * pg), w_k.reshape(-1, 1), w_v.reshape(-1, 1), pe_k, pe_v,
      cos_e, sin_e)


def _select_blocks(psum, cur, extra_forced, cnt_ref=None, cur_max=None):
    rows, nb = psum.shape
    lane = lax.broadcasted_iota(I32, (rows, nb), 1)
    jl = lane >> 1
    pair = jnp.where((lane & 1) == 0, psum + pltpu.roll(psum, nb - 1, axis=1),
                     psum + pltpu.roll(psum, 1, axis=1))
    forced = (jl == 0) | (jl == cur) | (jl == cur - 1)
    imp = jnp.where(forced, jnp.inf, jnp.where(jl <= cur, pair, -jnp.inf))
    base = jnp.where(imp < jnp.inf, float(extra_forced), 0.0)

    def count(cnt, j0, j1):
        for j in range(j0, j1):
            vj = imp[:, 2 * j:2 * j + 1]
            beats = (vj > imp) | ((vj == imp) & (jl > j))
            cnt = cnt + jnp.where(beats, 1.0, 0.0)
        return cnt

    if cnt_ref is None:
        return count(base, 0, nb // 2) < float(N_SEL)
    cnt_ref[...] = base
    group = 8
    for j0 in range(0, nb // 2, group):
        @pl.when(j0 <= cur_max)
        def _():
            cnt_ref[...] = count(cnt_ref[...], j0, j0 + group)
    return cnt_ref[...] < float(N_SEL)


def _flash_update(carry, s, mask, vb):
    m, l, acc = carry
    s = jnp.where(mask, s, NEG)
    m_new = jnp.maximum(m, jnp.max(s, axis=-1, keepdims=True))
    p = jnp.where(mask, jnp.exp(s - m_new), 0.0)
    alpha = jnp.exp(m - m_new)
    l = alpha * l + jnp.sum(p, axis=-1, keepdims=True)
    acc = alpha * acc + _dot(p.astype(BF16), vb)
    return m_new, l, acc


def _flash_step(carry, s, vb):
    m, l, acc = carry
    m_new = jnp.maximum(m, jnp.max(s, axis=-1, keepdims=True))
    p = jnp.exp(s - m_new)
    alpha = jnp.exp(m - m_new)
    l = alpha * l + jnp.sum(p, axis=-1, keepdims=True)
    acc = alpha * acc + _dot(p.astype(BF16), vb)
    return m_new, l, acc


def _flash_init(rows):
    return (jnp.full((rows, 1), NEG, F32), jnp.zeros((rows, 1), F32), jnp.zeros((rows, HD), F32))


def _compressed_branch(qb, kcb, vcb, qpos):
    s = _dot_nt(qb, kcb.astype(BF16))
    n = lax.broadcasted_iota(I32, s.shape, 1)
    vis = ((n + 1) * CMP_BLOCK - 1) <= qpos
    s = jnp.where(vis, s, NEG)
    p = jnp.where(vis, jnp.exp(s - jnp.max(s, axis=-1, keepdims=True)), 0.0)
    den = jnp.sum(p, axis=-1, keepdims=True)
    p = p / jnp.where(den > 0.0, den, 1.0)
    return _dot(p.astype(BF16), vcb.astype(BF16)), p


def _expand_blocks(sel_f, first_block, n_keys):
    nb = sel_f.shape[1]
    n = lax.broadcasted_iota(I32, (nb, n_keys), 0)
    kk = lax.broadcasted_iota(I32, (nb, n_keys), 1)
    e = (n == first_block + (kk >> 5)).astype(BF16)
    return _dot(sel_f.astype(BF16), e) > 0.5


def _nsa_prompt_kernel(q_ref, gt_ref, kcb_ref, vcb_ref, ks_ref, vs_ref, kw_ref, vw_ref, o_ref, cnt_ref,
                       *, tq, tk, gate_col):
    g, qi = pl.program_id(1), pl.program_id(2)
    q4 = q_ref[...]
    qb = (jnp.concatenate([q4[:, r * HD:(r + 1) * HD] for r in range(NSA_R)], axis=0)
          * (HD ** -0.5)).astype(BF16)
    rows = NSA_R * tq
    q0 = qi * tq
    qpos1 = q0 + lax.broadcasted_iota(I32, (tq, 1), 0)
    qpos = jnp.concatenate([qpos1] * NSA_R, axis=0)

    o_c, p_c = _compressed_branch(qb, kcb_ref[...], vcb_ref[...], qpos)
    psum = functools.reduce(jnp.add, [p_c[r * tq:(r + 1) * tq] for r in range(NSA_R)])
    sel = _select_blocks(psum, qpos1 >> 6, 0, cnt_ref, (q0 + tq - 1) >> 6)
    sel_b = jnp.where(sel, 1.0, 0.0).astype(BF16)
    nbl = sel_b.shape[1]
    e_n = lax.broadcasted_iota(I32, (nbl, tk), 0)
    e_k = lax.broadcasted_iota(I32, (nbl, tk), 1) >> 5

    def sel_tile(kt, carry, causal):
        start = pl.multiple_of(kt * tk, tk)
        kb = ks_ref[pl.ds(start, tk), :].astype(BF16)
        vb = vs_ref[pl.ds(start, tk), :].astype(BF16)
        picked = _dot(sel_b, (e_n == kt * (tk // CMP_BLOCK) + e_k).astype(BF16))
        bias = (picked - 1.0) * (-NEG)
        if causal:
            kpos = start + lax.broadcasted_iota(I32, (1, tk), 1)
            bias = jnp.where(kpos <= qpos1, bias, NEG)
        s = _dot_nt(qb, kb) + jnp.concatenate([bias] * NSA_R, axis=0)
        return _flash_step(carry, s, vb)

    n_kt = (q0 + tq + tk - 1) // tk
    carry = lax.fori_loop(0, n_kt - 1, lambda kt, c: sel_tile(kt, c, False), _flash_init(rows))
    m_s, l_s, acc_s = sel_tile(n_kt - 1, carry, True)
    o_s = acc_s / l_s

    carry = _flash_init(rows)
    for wt in reversed(range(WINDOW // tq + 1)):
        start = q0 - WINDOW + wt * tq
        cstart = pl.multiple_of(jnp.maximum(start, 0), tq)
        kb = kw_ref[pl.ds(cstart, tq), :].astype(BF16)
        vb = vw_ref[pl.ds(cstart, tq), :].astype(BF16)
        kpos = start + lax.broadcasted_iota(I32, (1, tq), 1)
        mask = (kpos <= qpos) & (kpos > qpos - WINDOW) & (kpos >= 0)
        carry = _flash_step(carry, jnp.where(mask, _dot_nt(qb, kb), NEG), vb)
    o_w = carry[2] / carry[1]

    gt = jax.nn.sigmoid(gt_ref[...])
    lane = lax.broadcasted_iota(I32, gt.shape, 1)
    outs = []
    for r in range(NSA_R):
        col = gate_col + (g * NSA_R + r) * 3
        gs = [jnp.sum(jnp.where(lane == col + j, gt, 0.0), axis=-1, keepdims=True) for j in range(3)]
        sl = slice(r * tq, (r + 1) * tq)
        outs.append(gs[0] * o_c[sl] + gs[1] * o_s[sl] + gs[2] * o_w[sl])
    o_ref[...] = jnp.concatenate(outs, axis=1).astype(BF16)


def _nsa_prompt(parts, kvb, nseq, s, dst):
    tq, tk = 256, 512
    nq = s // tq
    nb = s // CMP_BLOCK
    assert s % tk == 0 and WINDOW % tq == 0 and nb % LANE == 0
    gw = NSA_R * HD
    kv = lambda name: pl.BlockSpec((s, HD), lambda b, g, i: (b, dst[name] // HD + g))
    return pl.pallas_call(
        functools.partial(_nsa_prompt_kernel, tq=tq, tk=tk, gate_col=GLA_RANK),
        out_shape=jax.ShapeDtypeStruct((nseq * s, NSA_HEADS * HD), BF16),
        grid=(nseq, NSA_G, nq),
        in_specs=[pl.BlockSpec((tq, gw), lambda b, g, i: (b * nq + i, dst['nsa_q'] // gw + g)),
                  pl.BlockSpec((tq, LANE), lambda b, g, i: (b * nq + i, dst['gla_a'] // LANE)),
                  pl.BlockSpec((nb, HD), lambda b, g, i: (b, g)),
                  pl.BlockSpec((nb, HD), lambda b, g, i: (b, NSA_G + g)),
                  kv('nsa_ks'), kv('nsa_vs'), kv('nsa_kw'), kv('nsa_vw')],
        out_specs=pl.BlockSpec((tq, gw), lambda b, g, i: (b * nq + i, g)),
        scratch_shapes=[pltpu.VMEM((tq, nb), F32)],
        compiler_params=_cparams(("arbitrary", "arbitrary", "arbitrary")),
        name="nsa_prompt",
    )(parts, parts, kvb, kvb, parts, parts, parts, parts)


def _nsa_sample_kernel(pt_ref, *refs, pg, n_pages, t_new, t_pad, past_len, win_buf):
    page_refs = refs[:pg]
    q_ref, gt_ref, kvb_ref, win_ref, new_ref, o_ref, sel_ref, m_ref, l_ref, acc_ref, oc_ref = refs[pg:]
    i = pl.program_id(1)
    rows = NSA_R * t_pad
    page = page_refs[0].shape[0] // KV_CH
    kvw = NSA_G * HD
    nbc = kvb_ref.shape[0] // KV_CH
    chan = lambda ref, c, n: ref[pl.ds(c, n, stride=KV_CH), :]
    t1 = lax.broadcasted_iota(I32, (t_pad, 1), 0)
    t_row = jnp.concatenate([t1] * NSA_R, axis=0)
    qpos = past_len + t_row

    def queries(g):
        q = q_ref[...]
        return (jnp.concatenate([q[:, (g * NSA_R + r) * HD:(g * NSA_R + r + 1) * HD]
                                 for r in range(NSA_R)], axis=0) * (HD ** -0.5)).astype(BF16)

    @pl.when(i == 0)
    def _():
        for g in range(NSA_G):
            o_c, p_c = _compressed_branch(queries(g), chan(kvb_ref, g, nbc), chan(kvb_ref, NSA_G + g, nbc),
                                          qpos)
            psum = functools.reduce(jnp.add, [p_c[r * t_pad:(r + 1) * t_pad] for r in range(NSA_R)])
            cur = (past_len + t1) >> 6
            sel_ref[g] = _select_blocks(psum, cur, 1).astype(F32)
            oc_ref[g] = o_c
            m_ref[g] = jnp.full((rows, LANE), NEG, F32)
            l_ref[g] = jnp.zeros((rows, LANE), F32)
            acc_ref[g] = jnp.zeros((rows, HD), F32)

    bpp = page // CMP_BLOCK
    for g in range(NSA_G):
        qb = queries(g)
        carry = (m_ref[g][:, :1], l_ref[g][:, :1], acc_ref[g])
        kb = jnp.concatenate([chan(x, g, page).astype(BF16) for x in page_refs], axis=0)
        vb = jnp.concatenate([chan(x, NSA_G + g, page).astype(BF16) for x in page_refs], axis=0)
        m1 = _expand_blocks(sel_ref[g], i * pg * bpp, pg * page)
        mask = jnp.concatenate([m1] * NSA_R, axis=0)
        carry = _flash_update(carry, _dot_nt(qb, kb), mask, vb)
        m_ref[g] = jnp.broadcast_to(carry[0], (rows, LANE))
        l_ref[g] = jnp.broadcast_to(carry[1], (rows, LANE))
        acc_ref[g] = carry[2]

    @pl.when(i == pl.num_programs(1) - 1)
    def _():
        gt = jax.nn.sigmoid(gt_ref[...])
        new = new_ref[...]
        pad = jnp.zeros((LANE - t_pad, HD), F32)
        kk = lax.broadcasted_iota(I32, (1, LANE), 1)
        new_mask = (kk <= t_row) & (kk < t_new)
        wi = lax.broadcasted_iota(I32, (1, win_buf), 1)
        wpos = past_len - win_buf + wi
        win_mask = (wpos <= qpos) & (wpos > qpos - WINDOW) & (wpos >= 0)
        outs = []
        for g in range(NSA_G):
            qb = queries(g)

            def new_rows(j):
                return jnp.concatenate([new[:, j * kvw + g * HD:j * kvw + (g + 1) * HD], pad],
                                       axis=0).astype(BF16)

            carry = (m_ref[g][:, :1], l_ref[g][:, :1], acc_ref[g])
            carry = _flash_update(carry, _dot_nt(qb, new_rows(0)), new_mask, new_rows(1))
            o_s = carry[2] / carry[1]
            carry = _flash_init(rows)
            carry = _flash_update(carry, _dot_nt(qb, chan(win_ref, g, win_buf).astype(BF16)), win_mask,
                                  chan(win_ref, NSA_G + g, win_buf).astype(BF16))
            carry = _flash_update(carry, _dot_nt(qb, new_rows(2)), new_mask, new_rows(3))
            o_w = carry[2] / carry[1]
            o_c = oc_ref[g]
            for r in range(NSA_R):
                col = GLA_RANK + (g * NSA_R + r) * 3
                sl = slice(r * t_pad, (r + 1) * t_pad)
                outs.append(gt[:, col:col + 1] * o_c[sl] + gt[:, col + 1:col + 2] * o_s[sl]
                            + gt[:, col + 2:col + 3] * o_w[sl])
        o_ref[...] = jnp.concatenate(outs, axis=1)


def _nsa_sample(q_s, gt_s, kvb_s, slc2, page, win2, win_buf, new_s, page_table, t_new):
    nseq, t_pad, _ = q_s.shape
    width = KV_CH * HD
    n_pages = page_table.shape[1]
    past_len = n_pages * page
    nbc = kvb_s.shape[1] // KV_CH
    assert nbc % LANE == 0 and t_new <= SEL_BLOCK and past_len % SEL_BLOCK == 0
    pg = _divisor_tile(n_pages, 16, 1)
    rows = NSA_R * t_pad
    page_specs = [pl.BlockSpec((page * KV_CH, HD),
                               (lambda b, i, pt, p=p: (pt[b * n_pages + i * pg + p], 0)))
                  for p in range(pg)]
    per_seq = lambda shape: pl.BlockSpec((None,) + shape, lambda b, i, pt: (b, 0, 0))
    return pl.pallas_call(
        functools.partial(_nsa_sample_kernel, pg=pg, n_pages=n_pages, t_new=t_new, t_pad=t_pad,
                          past_len=past_len, win_buf=win_buf),
        out_shape=jax.ShapeDtypeStruct((nseq, t_pad, NSA_HEADS * HD), F32),
        grid_spec=pltpu.PrefetchScalarGridSpec(
            num_scalar_prefetch=1,
            grid=(nseq, n_pages // pg),
            in_specs=page_specs + [per_seq((t_pad, NSA_HEADS * HD)), per_seq((t_pad, LANE)),
                                   per_seq((nbc * KV_CH, HD)),
                                   pl.BlockSpec((win_buf * KV_CH, HD), lambda b, i, pt: (b, 0)),
                                   per_seq((t_pad, 2 * width))],
            out_specs=per_seq((t_pad, NSA_HEADS * HD)),
            scratch_shapes=[pltpu.VMEM((NSA_G, t_pad, nbc), F32),
                            pltpu.VMEM((NSA_G, rows, LANE), F32),
                            pltpu.VMEM((NSA_G, rows, LANE), F32),
                            pltpu.VMEM((NSA_G, rows, HD), F32),
                            pltpu.VMEM((NSA_G, rows, HD), F32)]),
        compiler_params=_cparams(("arbitrary", "arbitrary")),
        name="nsa_sample",
    )(page_table.reshape(-1), *([slc2] * pg), q_s, gt_s, kvb_s, win2, new_s)


def _router_kernel(u_ref, w_ref, b_ref, gate_ref, pos_ref, cnt_ref, carry_ref, *, n_exp):
    i = pl.program_id(0)
    tt = u_ref.shape[0]

    @pl.when(i == 0)
    def _():
        carry_ref[...] = jnp.zeros_like(carry_ref)

    logits = _dot(u_ref[...], w_ref[...].astype(BF16))
    sc = jax.nn.sigmoid(logits.T[:n_exp])
    biased = sc + b_ref[...]
    per = n_exp // N_GROUPS
    sub = lax.broadcasted_iota(I32, (per, tt), 0)
    gs_rows = []
    for gq in range(N_GROUPS):
        x8 = biased[gq * per:(gq + 1) * per]
        m1 = jnp.max(x8, axis=0, keepdims=True)
        first = jnp.min(jnp.where(x8 == m1, sub, per), axis=0, keepdims=True)
        m2 = jnp.max(jnp.where(sub == first, -jnp.inf, x8), axis=0, keepdims=True)
        gs_rows.append(m1 + m2)
    gs = jnp.concatenate(gs_rows, axis=0)
    gi = lax.broadcasted_iota(I32, gs.shape, 0)
    gcnt = jnp.zeros(gs.shape, F32)
    for j in range(N_GROUPS):
        vj = gs[j:j + 1]
        gcnt = gcnt + jnp.where((vj > gs) | ((vj == gs) & (gi > j)), 1.0, 0.0)
    gsel = jnp.where(gcnt < float(TOPK_GROUPS), 1.0, 0.0)
    emask = jnp.concatenate([jnp.broadcast_to(gsel[gq:gq + 1], (per, tt)) for gq in range(N_GROUPS)],
                            axis=0) > 0.5
    masked = jnp.where(emask, biased, -jnp.inf)
    ei = lax.broadcasted_iota(I32, masked.shape, 0)
    ecnt = jnp.zeros(masked.shape, F32)
    for j in range(n_exp):
        vj = masked[j:j + 1]
        ecnt = ecnt + jnp.where((vj > masked) | ((vj == masked) & (ei > j)), 1.0, 0.0)
    sel = ecnt < float(TOP_K)
    g = jnp.where(sel, sc, 0.0)
    gate_ref[...] = g / jnp.sum(g, axis=0, keepdims=True) * ROUTED_SCALE
    self_ = sel.astype(BF16)
    tr = lax.broadcasted_iota(I32, (tt, tt), 0)
    tc = lax.broadcasted_iota(I32, (tt, tt), 1)
    prefix = _dot(self_, (tr < tc).astype(BF16))
    carry = carry_ref[...]
    pos_ref[...] = jnp.where(sel, (prefix + carry).astype(I32), -1)
    carry = carry + jnp.sum(sel.astype(F32), axis=1, keepdims=True)
    carry_ref[...] = carry
    cnt_ref[...] = carry.astype(I32)


def _router(u, w_router, b_router):
    m, d = u.shape
    n_exp = w_router.shape[1]
    assert n_exp <= LANE and n_exp % (8 * N_GROUPS) == 0
    tt = LANE
    w_pad = jnp.pad(w_router, ((0, 0), (0, LANE - n_exp)))
    return pl.pallas_call(
        functools.partial(_router_kernel, n_exp=n_exp),
        out_shape=(jax.ShapeDtypeStruct((n_exp, m), F32), jax.ShapeDtypeStruct((n_exp, m), I32),
                   jax.ShapeDtypeStruct((n_exp, LANE), I32)),
        grid=(m // tt,),
        in_specs=[pl.BlockSpec((tt, d), lambda i: (i, 0)),
                  pl.BlockSpec((d, LANE), lambda i: (0, 0)),
                  pl.BlockSpec((n_exp, 1), lambda i: (0, 0))],
        out_specs=(pl.BlockSpec((n_exp, tt), lambda i: (0, i)),
                   pl.BlockSpec((n_exp, tt), lambda i: (0, i)),
                   pl.BlockSpec((n_exp, LANE), lambda i: (0, 0))),
        scratch_shapes=[pltpu.VMEM((n_exp, LANE), F32)],
        compiler_params=_cparams(("arbitrary",)),
        name="moe_router",
    )(u, w_pad, b_router.reshape(n_exp, 1))


def _slots_kernel(gate_ref, pos_ref, start_ref, dest_ref, gk_ref):
    pos = pos_ref[...]
    sel = pos >= 0
    n_exp, tt = pos.shape
    dest = start_ref[...] + pos
    er = lax.broadcasted_iota(I32, (n_exp, n_exp), 0)
    ec = lax.broadcasted_iota(I32, (n_exp, n_exp), 1)
    rank = _dot((ec < er).astype(BF16), sel.astype(BF16))
    gate = gate_ref[...]
    for k in range(TOP_K):
        pick = sel & (rank == float(k))
        dest_ref[k:k + 1, :] = jnp.sum(jnp.where(pick, dest, 0), axis=0, keepdims=True)
        gk_ref[k:k + 1, :] = jnp.sum(jnp.where(pick, gate, 0.0), axis=0, keepdims=True)


def _slots(gate_t, pos_t, start):
    n_exp, m = pos_t.shape
    tt = LANE
    return pl.pallas_call(
        _slots_kernel,
        out_shape=(jax.ShapeDtypeStruct((TOP_K, m), I32), jax.ShapeDtypeStruct((TOP_K, m), F32)),
        grid=(m // tt,),
        in_specs=[pl.BlockSpec((n_exp, tt), lambda i: (0, i)),
                  pl.BlockSpec((n_exp, tt), lambda i: (0, i)),
                  pl.BlockSpec((n_exp, 1), lambda i: (0, 0))],
        out_specs=(pl.BlockSpec((TOP_K, tt), lambda i: (0, i)),
                   pl.BlockSpec((TOP_K, tt), lambda i: (0, i))),
        compiler_params=_cparams(("arbitrary",)),
        name="moe_slots",
    )(gate_t, pos_t, start.reshape(n_exp, 1))


def _dispatch_kernel(meta_ref, tok_ref, tokn_ref, up_hbm, xs_ref, buf_ref, sem, *, rps, s2):
    i = pl.program_id(0)
    slot = lax.rem(i, 2)
    rows_used = meta_ref[0] * MOE_ROWS

    def issue(t_ref, sl):
        for a in range(rps // LANE):
            def start(b2, c, a=a):
                for q in range(2):
                    b = 2 * b2 + q
                    src = up_hbm.at[pl.ds(pl.multiple_of(t_ref[a, b] * s2, s2), s2)]
                    dst = buf_ref.at[sl, pl.ds(pl.multiple_of((a * LANE + b) * s2, s2), s2)]
                    pltpu.make_async_copy(src, dst, sem.at[sl]).start(priority=q)
                return c

            lax.fori_loop(0, LANE // 2, start, 0, unroll=4)

    @pl.when((i == 0) & (rows_used > 0))
    def _():
        issue(tok_ref, slot)

    @pl.when((i + 1 < pl.num_programs(0)) & ((i + 1) * rps < rows_used))
    def _():
        issue(tokn_ref, 1 - slot)

    @pl.when(i * rps < rows_used)
    def _():
        pltpu.make_async_copy(up_hbm.at[pl.ds(0, rps * s2)], buf_ref.at[slot], sem.at[slot]).wait()
        for s in range(s2):
            w = buf_ref[slot, pl.ds(s, rps, stride=s2), :]
            xs_ref[:, s * 256:s * 256 + LANE] = pltpu.bitcast(w << 16, F32).astype(BF16)
            xs_ref[:, s * 256 + LANE:(s + 1) * 256] = pltpu.bitcast(w & jnp.uint32(0xFFFF0000), F32).astype(BF16)

    @pl.when(i * rps >= rows_used)
    def _():
        xs_ref[...] = jnp.zeros_like(xs_ref)


def _dispatch(meta, row_tok3, up2, s2):
    n_steps, sub, _ = row_tok3.shape
    rps = sub * LANE
    assert up2.shape[0] >= rps * s2
    smem = lambda f: pl.BlockSpec((None, sub, LANE), lambda i, meta: (f(i), 0, 0), memory_space=pltpu.SMEM)
    return pl.pallas_call(
        functools.partial(_dispatch_kernel, rps=rps, s2=s2),
        out_shape=jax.ShapeDtypeStruct((n_steps * rps, 256 * s2), BF16),
        grid_spec=pltpu.PrefetchScalarGridSpec(
            num_scalar_prefetch=1,
            grid=(n_steps,),
            in_specs=[smem(lambda i: i), smem(lambda i: jnp.minimum(i + 1, n_steps - 1)),
                      pl.BlockSpec(memory_space=pl.ANY)],
            out_specs=pl.BlockSpec((rps, 256 * s2), lambda i, meta: (i, 0)),
            scratch_shapes=[pltpu.VMEM((2, rps * s2, LANE), U32), pltpu.SemaphoreType.DMA((2,))]),
        compiler_params=_cparams(("arbitrary",)),
        name="moe_dispatch",
    )(meta, row_tok3, row_tok3, up2)


def _blk(i, meta_ref):
    return jnp.minimum(i, meta_ref[0] - 1)


def _group_tables(blk_e, n_used):
    n_blk = blk_e.shape[0]
    idx = jnp.arange(n_blk, dtype=I32)
    first = (idx < n_used) & ((idx == 0) | (blk_e != jnp.roll(blk_e, 1)))
    grp = jnp.cumsum(first.astype(I32)) - 1
    starts = jnp.where(first, idx, n_blk)
    next_start = jnp.concatenate([lax.cummin(starts, reverse=True)[1:], jnp.full((1,), n_blk, I32)])
    nxt = jnp.where(next_start < n_blk, blk_e[jnp.minimum(next_start, n_blk - 1)], -1)
    return first.astype(I32), grp.astype(I32), nxt.astype(I32), jnp.sum(first.astype(I32))


def _weight_stream(meta_ref, be_ref, first_ref, grp_ref, nxt_ref, pass_id, n_pass, i, copies, sem, on_ready):
    blk = _blk(i, meta_ref)
    slot = lax.rem(pass_id * meta_ref[1] + grp_ref[blk], 2)

    @pl.when((first_ref[blk] == 1) & (i < meta_ref[0]))
    def _():
        @pl.when((i == 0) & (pass_id == 0))
        def _():
            for c in copies(be_ref[0], 0, slot):
                c.start()

        for c in copies(be_ref[blk], pass_id, slot):
            c.wait()
        nxt = nxt_ref[blk]

        @pl.when(nxt >= 0)
        def _():
            for c in copies(nxt, pass_id, 1 - slot):
                c.start()

        @pl.when((nxt < 0) & (pass_id + 1 < n_pass))
        def _():
            for c in copies(be_ref[0], pass_id + 1, 1 - slot):
                c.start()

        on_ready(slot)


def _expert_up_kernel(meta_ref, be_ref, first_ref, grp_ref, nxt_ref, x_ref, w1_hbm, w3_hbm, h_ref,
                      wf1_ref, wf3_ref, wb1_ref, wb3_ref, sem, *, th):
    hh, i = pl.program_id(0), pl.program_id(1)

    def copies(expert, pass_id, slot):
        cols = pl.ds(pl.multiple_of(pass_id * th, th), th)
        return [pltpu.make_async_copy(w_hbm.at[expert, :, cols], wf.at[slot], sem.at[slot])
                for w_hbm, wf in ((w1_hbm, wf1_ref), (w3_hbm, wf3_ref))]

    def on_ready(slot):
        wb1_ref[...] = wf1_ref[slot].astype(BF16)
        wb3_ref[...] = wf3_ref[slot].astype(BF16)

    _weight_stream(meta_ref, be_ref, first_ref, grp_ref, nxt_ref, hh, pl.num_programs(0), i, copies, sem,
                   on_ready)

    @pl.when(i < meta_ref[0])
    def _():
        x = x_ref[...]
        h_ref[...] = (_silu(_dot(x, wb1_ref[...])) * _dot(x, wb3_ref[...])).astype(BF16)

    @pl.when(i >= meta_ref[0])
    def _():
        h_ref[...] = jnp.zeros_like(h_ref)


def _expert_up(tables, xs, w1, w3, n_blk, rb):
    n_exp, d, hid = w1.shape
    th = min(hid, 512)
    return pl.pallas_call(
        functools.partial(_expert_up_kernel, th=th),
        out_shape=jax.ShapeDtypeStruct((n_blk * rb, hid), BF16),
        grid_spec=pltpu.PrefetchScalarGridSpec(
            num_scalar_prefetch=5,
            grid=(hid // th, n_blk),
            in_specs=[pl.BlockSpec((rb, d), lambda hh, i, meta, *_: (_blk(i, meta), 0)),
                      pl.BlockSpec(memory_space=pl.ANY), pl.BlockSpec(memory_space=pl.ANY)],
            out_specs=pl.BlockSpec((rb, th), lambda hh, i, *_: (i, hh)),
            scratch_shapes=[pltpu.VMEM((2, d, th), F32), pltpu.VMEM((2, d, th), F32),
                            pltpu.VMEM((d, th), BF16), pltpu.VMEM((d, th), BF16),
                            pltpu.SemaphoreType.DMA((2,))]),
        compiler_params=_cparams(("arbitrary", "arbitrary")),
        name="moe_expert_up",
    )(*tables, xs, w1, w3)


def _row_pitch(n_sub):
    return n_sub + 8


def _expert_down_kernel(meta_ref, be_ref, first_ref, grp_ref, nxt_ref, h_ref, w2_hbm, y_ref, wf_ref, wb_ref,
                        sem, *, rb, tn, pitch):
    i = pl.program_id(0)

    def copies(expert, pass_id, slot):
        return [pltpu.make_async_copy(w2_hbm.at[expert], wf_ref.at[slot], sem.at[slot])]

    def on_ready(slot):
        wb_ref[...] = wf_ref[slot].astype(BF16)

    _weight_stream(meta_ref, be_ref, first_ref, grp_ref, nxt_ref, 0, 1, i, copies, sem, on_ready)

    @pl.when(i < meta_ref[0])
    def _():
        h = h_ref[...]
        d = wb_ref.shape[1]
        for c in range(d // tn):
            y = _dot(h, wb_ref[:, c * tn:(c + 1) * tn])
            for s, w in enumerate(_pack_pairs(y.astype(BF16).astype(F32))):
                y_ref[pl.ds(c * (tn // 256) + s, rb, stride=pitch), :] = w
        for s in range(d // 256, pitch):
            y_ref[pl.ds(s, rb, stride=pitch), :] = jnp.zeros((rb, LANE), U32)

    @pl.when(i >= meta_ref[0])
    def _():
        y_ref[...] = jnp.zeros_like(y_ref)


def _expert_down(tables, h, w2, n_blk, rb):
    n_exp, hid, d = w2.shape
    pitch = _row_pitch(d // 256)
    return pl.pallas_call(
        functools.partial(_expert_down_kernel, rb=rb, tn=min(d, 512), pitch=pitch),
        out_shape=jax.ShapeDtypeStruct((n_blk * rb * pitch, LANE), U32),
        grid_spec=pltpu.PrefetchScalarGridSpec(
            num_scalar_prefetch=5,
            grid=(n_blk,),
            in_specs=[pl.BlockSpec((rb, hid), lambda i, meta, *_: (_blk(i, meta), 0)),
                      pl.BlockSpec(memory_space=pl.ANY)],
            out_specs=pl.BlockSpec((rb * pitch, LANE), lambda i, *_: (i, 0)),
            scratch_shapes=[pltpu.VMEM((2, hid, d), F32), pltpu.VMEM((hid, d), BF16),
                            pltpu.SemaphoreType.DMA((2,))]),
        compiler_params=_cparams(("arbitrary",)),
        name="moe_expert_down",
    )(*tables, h, w2)


def _final_kernel(dest_ref, destn_ref, gk_ref, y_hbm, x1_ref, ysh_ref, g2_ref, lg_ref, lb_ref, o_ref,
                  buf_ref, r_ref, sem, *, tf, alpha, pitch):
    i = pl.program_id(0)
    s2 = x1_ref.shape[1] // 256
    per_blk = ROW_TILE // tf
    slot = lax.rem(i, 2)

    def issue(d_ref, step, sl):
        off = lax.rem(step, per_blk) * tf

        def body(t, c):
            for k in range(TOP_K):
                pltpu.make_async_copy(y_hbm.at[d_ref[k, off + t], pl.ds(0, s2)], buf_ref.at[sl, k, t],
                                      sem.at[sl]).start(priority=k % 2)
            return c

        lax.fori_loop(0, tf, body, 0, unroll=2)

    @pl.when(i == 0)
    def _():
        issue(dest_ref, i, slot)

    @pl.when(i + 1 < pl.num_programs(0))
    def _():
        issue(destn_ref, i + 1, 1 - slot)

    for k in range(TOP_K):
        pltpu.make_async_copy(y_hbm.at[pl.ds(0, tf), pl.ds(0, s2)], buf_ref.at[slot, k],
                              sem.at[slot]).wait()

    off = lax.rem(i, per_blk) * tf

    def token(t, c):
        lo = hi = None
        for k in range(TOP_K):
            g = gk_ref[k, off + t]
            w = buf_ref[slot, k, t]
            tlo = g * pltpu.bitcast(w << 16, F32)
            thi = g * pltpu.bitcast(w & jnp.uint32(0xFFFF0000), F32)
            lo, hi = (tlo, thi) if lo is None else (lo + tlo, hi + thi)
        base = pl.multiple_of(t * pitch, 8)
        r_ref[pl.ds(base, s2), :] = lo
        r_ref[pl.ds(base + s2, s2), :] = hi
        return c

    lax.fori_loop(0, tf, token, 0, unroll=2)
    routed = jnp.concatenate(
        [r_ref[pl.ds(half * s2 + s, tf, stride=pitch), :] for s in range(s2) for half in range(2)], axis=1)
    v = alpha * x1_ref[...] + g2_ref[...] * (routed + ysh_ref[...])
    o_ref[...] = _ln_rows(v) * lg_ref[...] + lb_ref[...]


def _final(dest3, gk3, y3, x1, ysh, mod_tiles, mod_index, tf, ln_g, ln_b, alpha):
    m, d = x1.shape
    n_rows = y3.shape[0]
    s2 = d // 256
    pitch = _row_pitch(d // LANE)
    per_blk = ROW_TILE // tf
    n_steps = m // tf
    assert dest3.shape[2] == ROW_TILE and n_rows >= tf
    row = pl.BlockSpec((tf, d), lambda i: (i, 0))
    vec = pl.BlockSpec((1, d), lambda i: (0, 0))
    smem_blk = lambda f: pl.BlockSpec((None, TOP_K, ROW_TILE), lambda i: (f(i) // per_blk, 0, 0),
                                      memory_space=pltpu.SMEM)
    return pl.pallas_call(
        functools.partial(_final_kernel, tf=tf, alpha=alpha, pitch=pitch),
        out_shape=jax.ShapeDtypeStruct((m, d), F32),
        grid=(n_steps,),
        in_specs=[smem_blk(lambda i: i), smem_blk(lambda i: jnp.minimum(i + 1, n_steps - 1)),
                  smem_blk(lambda i: i),
                  pl.BlockSpec(memory_space=pl.ANY),
                  row, row, _mod_spec(5, d, tf, mod_index), vec, vec],
        out_specs=row,
        scratch_shapes=[pltpu.VMEM((2, TOP_K, tf, s2, LANE), U32),
                        pltpu.VMEM((tf * pitch, LANE), F32),
                        pltpu.SemaphoreType.DMA((2,))],
        compiler_params=_cparams(("arbitrary",)),
        name="moe_combine_ln2",
    )(dest3, dest3, gk3, y3, x1, ysh, mod_tiles, ln_g.reshape(1, d), ln_b.reshape(1, d))


def _rope_tables(pos):
    half = HD // 2
    inv = jnp.power(ROPE_THETA, -jnp.arange(half, dtype=F32) / half)
    ang = pos.astype(F32)[:, None] * inv[None, :]
    cos, sin = jnp.cos(ang), jnp.sin(ang)
    return jnp.concatenate([cos, cos], axis=1), jnp.concatenate([-sin, sin], axis=1)


def kernel(x_prompt, x_sample, c_prompt, c_sample, cache_cmp, cache_slc, cache_win, state_gla, page_table, w_ada, b_ada, w_in, w_gla_a2, b_gla_a2, gla_norm_g, cmp_pe_k, cmp_pe_v, cmp_w_k, cmp_w_v, w_br_gla, w_br_nsa, w_out, ln1_g, ln1_b, w_router, b_router, w_exp1, w_exp3, w_exp2, w_sh1, w_sh3, w_sh2, ln2_g, ln2_b):
    depth = w_in.shape[0]
    assert depth == 1
    alpha = (2.0 * depth) ** 0.25
    nb_, s_, d = x_prompt.shape
    db_, t_, _ = x_sample.shape
    mp, ns = nb_ * s_, db_ * t_
    m = mp + ns
    assert ns % ROW_TILE == 0 and s_ % ROW_TILE == 0 and d % 256 == 0
    n_pool, page = cache_cmp.shape[1], cache_cmp.shape[2]
    n_pages = page_table.shape[1]
    past_len = n_pages * page
    kvw2 = 2 * NSA_G * HD
    src, order, dst, n_used, _ = _col_plan(d)
    tn = 512
    small_off = dst['gla_a']
    dst = dict(dst)
    dst['nsa_g'] = small_off + GLA_RANK
    n_p = -(-(small_off + LANE) // tn) * tn

    rc = -(-(nb_ + db_) // 8) * 8
    c_all = jnp.pad(jnp.concatenate([c_prompt, c_sample], axis=0), ((0, rc - nb_ - db_), (0, 0)))
    mod = _ada(c_all, w_ada[0], b_ada[0])
    mod_tiles = jnp.concatenate(
        [jnp.broadcast_to(mod[:nb_, None, :], (nb_, ROW_TILE, 6 * d)),
         jnp.repeat(mod[nb_:nb_ + db_], t_, axis=0).reshape(ns // ROW_TILE, ROW_TILE, 6 * d)], axis=0)
    mod_index = _mod_index(ROW_TILE, mp, s_, nb_)

    x_all = jnp.concatenate([x_prompt.reshape(mp, d), x_sample.reshape(ns, d)], axis=0)
    u1 = _ln_mod(x_all, mod_tiles, mod_index)

    segments = []
    for name in order[:-2]:
        lo, shift = dst[name], src[name][0] - dst[name]
        if segments and segments[-1][2] == shift and segments[-1][1] == lo:
            segments[-1] = (segments[-1][0], lo + src[name][1], shift)
        else:
            segments.append((lo, lo + src[name][1], shift))
    a_src, g_src = src['gla_a'][0], src['nsa_g'][0]
    assert a_src % LANE == 0 and g_src % LANE == GLA_RANK
    small = (small_off, (a_src // LANE, GLA_RANK), (g_src // LANE, GLA_RANK + src['nsa_g'][1]))
    pos_all = jnp.concatenate([jnp.tile(jnp.arange(s_, dtype=I32), nb_),
                               jnp.tile(past_len + jnp.arange(t_, dtype=I32), db_)])
    cos_t, sin_t = _rope_tables(pos_all)
    tm = _divisor_tile(m, 1056, 16)
    rope_cols = ((dst['nsa_q'], dst['nsa_q'] + NSA_HEADS * HD),
                 (dst['nsa_ks'], dst['nsa_ks'] + NSA_G * HD),
                 (dst['nsa_kw'], dst['nsa_kw'] + NSA_G * HD))
    parts = _inproj(u1, w_in[0], cos_t, sin_t, rope_cols, tuple(segments), small, n_p, tm, tn)

    zeros_state = jnp.zeros((nb_, GLA_HEADS, GLA_DK, GLA_DV), F32)
    og_p, gla_p = _gla(parts, nb_, s_, s_, 512, 128, dst, zeros_state, w_gla_a2[0], b_gla_a2[0],
                       gla_norm_g[0])
    t_pad = 16
    gla_cols = dst['nsa_q']
    smp = parts[mp:].reshape(db_, t_, n_p)
    smp_pad = jnp.pad(smp, ((0, 0), (0, t_pad - t_), (0, 0))).reshape(db_ * t_pad, n_p)
    og_s, gla_s = _gla(smp_pad, db_, t_pad, t_, t_pad, t_pad, dst, state_gla[0], w_gla_a2[0],
                       b_gla_a2[0], gla_norm_g[0])
    og_s = og_s.reshape(db_, t_pad, -1)[:, :t_].reshape(ns, -1)
    og = jnp.concatenate([og_p, og_s], axis=0)

    cmp_pos_p = jnp.tile((jnp.arange(s_ // CMP_BLOCK, dtype=I32) + 1) * CMP_BLOCK - 1, nb_)
    cos_c, sin_c = _rope_tables(cmp_pos_p)
    kvb_p = _compress_prompt(parts, mp, dst, cmp_w_k[0], cmp_w_v[0], cmp_pe_k[0], cmp_pe_v[0], cos_c, sin_c)
    on_p = _nsa_prompt(parts, kvb_p, nb_, s_, dst)

    cmp_pos_s = (jnp.arange(past_len // CMP_BLOCK, dtype=I32) + 1) * CMP_BLOCK - 1
    cos_cs, sin_cs = _rope_tables(cmp_pos_s)
    is_key = (jnp.arange(KV_CH) < NSA_G)[None, :, None]
    cos_e = jnp.where(is_key, cos_cs[:, None, :], 1.0).reshape(-1, HD)
    sin_e = jnp.where(is_key, sin_cs[:, None, :], 0.0).reshape(-1, HD)
    kvb_s = _compress_pages(cache_cmp.reshape(-1, HD), page, page_table, cmp_w_k[0], cmp_w_v[0],
                            cmp_pe_k[0], cmp_pe_v[0], cos_e, sin_e)
    tq_pad = 8
    pad_t = lambda a: jnp.pad(a, ((0, 0), (0, tq_pad - t_), (0, 0)))
    q_s = pad_t(smp[:, :, dst['nsa_q']:dst['nsa_q'] + NSA_HEADS * HD])
    gt_s = pad_t(smp[:, :, small_off:small_off + LANE])
    new_s = pad_t(smp[:, :, dst['nsa_ks']:dst['nsa_ks'] + 2 * kvw2])
    on_s = _nsa_sample(q_s, gt_s, kvb_s, cache_slc.reshape(-1, HD), page, cache_win.reshape(-1, HD),
                       cache_win.shape[2], new_s, page_table, t_)
    on = jnp.concatenate([on_p, on_s[:, :t_].reshape(ns, -1).astype(BF16)], axis=0)

    z = _merge(og, on, w_br_gla[0], w_br_nsa[0], parts, dst['mg_gla'], dst['mg_nsa'], tm, tn)
    y = _plain(z, w_out[0], tm, tn, "out_proj")
    x1, u2, up = _mid(x_all, y, mod_tiles, mod_index, ln1_g[0], ln1_b[0], alpha)

    n_exp = w_router.shape[2]
    rb = MOE_ROWS
    s2 = d // 256
    gate_t, pos_t, cnt = _router(u2, w_router[0], b_router[0])
    counts = cnt[:, 0]
    padded = (counts + rb - 1) // rb * rb
    pend = jnp.cumsum(padded)
    start = pend - padded
    n_blk = -(-(m * TOP_K + n_exp * (rb - 1)) // rb)
    rps = 2 * rb
    n_blk = -(-(m * TOP_K + n_exp * (rb - 1)) // rps) * (rps // rb)
    blk_first = jnp.arange(n_blk, dtype=I32) * rb
    blk_e = jnp.minimum(jnp.sum(pend[None, :] <= blk_first[:, None], axis=1), n_exp - 1).astype(I32)
    meta = (pend[-1:] // rb).astype(I32)
    dest, gk = _slots(gate_t, pos_t, start.astype(I32))
    nt = m // ROW_TILE
    as_blocks = lambda a: a.reshape(TOP_K, nt, ROW_TILE).transpose(1, 0, 2)
    dest3, gk3 = as_blocks(dest), as_blocks(gk)
    row_tok = jnp.zeros((n_blk * rb,), I32).at[dest.reshape(-1)].set(
        jnp.tile(jnp.arange(m, dtype=I32), TOP_K))
    xs = _dispatch(meta, row_tok.reshape(-1, rps // LANE, LANE), up, s2)
    first, grp, nxt, n_groups = _group_tables(blk_e, meta[0])
    tables = (jnp.concatenate([meta, n_groups[None]]), blk_e, first, grp, nxt)
    h = _expert_up(tables, xs, w_exp1[0], w_exp3[0], n_blk, rb)
    y2 = _expert_down(tables, h, w_exp2[0], n_blk, rb)
    hs = _swiglu(u2, w_sh1[0], w_sh3[0], tm, tn // 2)
    ysh = _plain(hs, w_sh2[0], tm, tn, "shared_down")
    tf = 64
    out = _final(dest3, gk3, y2.reshape(n_blk * rb, -1, LANE), x1, ysh, mod_tiles,
                 _mod_index(tf, mp, s_, nb_), tf, ln2_g[0], ln2_b[0], alpha)

    def rows(lo, n_rows, name, lead):
        return parts[lo:lo + n_rows, dst[name]:dst[name] + kvw2].reshape(lead + (2, NSA_G, HD))

    win_buf = cache_win.shape[2]
    win_p = rows(0, mp, 'nsa_kw', (nb_, s_))[:, s_ - win_buf:]
    win_s = jnp.concatenate([cache_win[0], rows(mp, ns, 'nsa_kw', (db_, t_))], axis=1)[:, t_:]
    return (out[:mp].reshape(nb_, s_, d), out[mp:].reshape(db_, t_, d),
            rows(0, mp, 'nsa_kc', (nb_, s_))[None], rows(mp, ns, 'nsa_kc', (db_, t_))[None],
            rows(0, mp, 'nsa_ks', (nb_, s_))[None], rows(mp, ns, 'nsa_ks', (db_, t_))[None],
            win_p[None], win_s[None], gla_p[None], gla_s[None])
```

```python
import functools
import math

import jax
import jax.numpy as jnp
import numpy as np
from jax import lax
from jax.experimental import pallas as pl
from jax.experimental.pallas import tpu as pltpu

F32, BF16, I32, U32 = jnp.float32, jnp.bfloat16, jnp.int32, jnp.uint32
HIGHEST = lax.Precision.HIGHEST

GLA_HEADS, GLA_DK, GLA_DV, GLA_RANK, GLA_GATE_NORM = 8, 128, 256, 16, 16.0
NSA_HEADS, NSA_G, NSA_R, HD = 16, 4, 4, 128
KV_CH = 2 * NSA_G
CMP_BLOCK, SEL_BLOCK, N_SEL, WINDOW = 32, 64, 16, 512
ROPE_THETA = 10000.0
TOP_K, N_GROUPS, TOPK_GROUPS, ROUTED_SCALE = 8, 8, 4, 2.5
LN_EPS = 1e-5
NEG = -1e30

LANE = 128
VMEM_LIMIT = 56 * 1024 * 1024
ROW_TILE = 128
MOE_ROWS = 256
GLA_SUB = 32


def _cparams(sem):
    return pltpu.CompilerParams(dimension_semantics=sem, vmem_limit_bytes=VMEM_LIMIT)


def _divisor_tile(n, target, mult):
    best = None
    for d in range(mult, min(n, target) + 1, mult):
        if n % d == 0:
            best = d
    assert best is not None, (n, target, mult)
    return best


def _silu(x):
    return x * jax.nn.sigmoid(x)


def _ln_rows(x):
    mu = jnp.mean(x, axis=-1, keepdims=True)
    xc = x - mu
    var = jnp.mean(xc * xc, axis=-1, keepdims=True)
    return xc * lax.rsqrt(var + LN_EPS)


def _dot(a, b):
    return jnp.dot(a, b, preferred_element_type=F32)


def _dot_nt(a, b, precision=None):
    return lax.dot_general(a, b, (((1,), (1,)), ((), ())), precision=precision,
                           preferred_element_type=F32)


def _dot_tn(a, b, precision=None):
    return lax.dot_general(a, b, (((0,), (0,)), ((), ())), precision=precision,
                           preferred_element_type=F32)


def _col_plan(d_model):
    gqk, gv = GLA_HEADS * GLA_DK, GLA_HEADS * GLA_DV
    nq, nkv = NSA_HEADS * HD, NSA_G * HD
    ref = (('gla_q', gqk), ('gla_k', gqk), ('gla_v', gv), ('gla_r', gv), ('gla_a', GLA_RANK),
           ('nsa_q', nq), ('nsa_kc', nkv), ('nsa_vc', nkv), ('nsa_ks', nkv), ('nsa_vs', nkv),
           ('nsa_kw', nkv), ('nsa_vw', nkv), ('nsa_g', NSA_HEADS * 3), ('mg_gla', d_model),
           ('mg_nsa', d_model))
    src, o = {}, 0
    for n, w in ref:
        src[n] = (o, w)
        o += w
    order = ('gla_q', 'gla_k', 'gla_v', 'gla_r', 'nsa_q', 'nsa_kc', 'nsa_vc', 'nsa_ks', 'nsa_vs',
             'nsa_kw', 'nsa_vw', 'mg_gla', 'mg_nsa', 'gla_a', 'nsa_g')
    dst, o = {}, 0
    for n in order:
        dst[n] = o
        o += src[n][1]
    return src, order, dst, o, o


def _ada_kernel(c_ref, w_ref, b_ref, o_ref):
    a = _silu(c_ref[...]).astype(BF16)
    o_ref[...] = _dot(a, w_ref[...].astype(BF16)) + b_ref[...]


def _ada(c_all, w_ada, b_ada):
    rc, d = c_all.shape
    n = w_ada.shape[1]
    tn = 512
    return pl.pallas_call(
        _ada_kernel,
        out_shape=jax.ShapeDtypeStruct((rc, n), F32),
        grid=(n // tn,),
        in_specs=[pl.BlockSpec((rc, d), lambda j: (0, 0)),
                  pl.BlockSpec((d, tn), lambda j: (0, j)),
                  pl.BlockSpec((1, tn), lambda j: (0, j))],
        out_specs=pl.BlockSpec((rc, tn), lambda j: (0, j)),
        compiler_params=_cparams(("arbitrary",)),
        name="ada_mod",
    )(c_all, w_ada, b_ada.reshape(1, n))


def _mod_index(rows, mp, seq, nseq):
    n_pt, per_seq, per_grp = mp // rows, seq // rows, ROW_TILE // rows

    def index(i):
        j = jnp.maximum(i - n_pt, 0)
        return (jnp.where(i < n_pt, i // per_seq, nseq + j // per_grp),
                jnp.where(i < n_pt, 0, j % per_grp))

    return index


def _mod_spec(k, d, rows, mod_index):
    return pl.BlockSpec((None, rows, d), lambda i: (*mod_index(i), k))


def _ln_mod_kernel(x_ref, sh_ref, sc_ref, u_ref):
    y = _ln_rows(x_ref[...])
    u_ref[...] = (y * (1.0 + sc_ref[...]) + sh_ref[...]).astype(BF16)


def _ln_mod(x_all, mod_tiles, mod_index):
    m, d = x_all.shape
    return pl.pallas_call(
        _ln_mod_kernel,
        out_shape=jax.ShapeDtypeStruct((m, d), BF16),
        grid=(m // ROW_TILE,),
        in_specs=[pl.BlockSpec((ROW_TILE, d), lambda i: (i, 0)),
                  _mod_spec(0, d, ROW_TILE, mod_index), _mod_spec(1, d, ROW_TILE, mod_index)],
        out_specs=pl.BlockSpec((ROW_TILE, d), lambda i: (i, 0)),
        compiler_params=_cparams(("arbitrary",)),
        name="ln_mod1",
    )(x_all, mod_tiles, mod_tiles)


def _pack_pairs(u):
    bits = pltpu.bitcast(u, U32)
    out = []
    for s in range(u.shape[1] // 256):
        lo = bits[:, s * 256:s * 256 + 128]
        hi = bits[:, s * 256 + 128:s * 256 + 256]
        out.append((lo >> 16) | (hi & jnp.uint32(0xFFFF0000)))
    return out


def _mid_kernel(x_ref, y_ref, g1_ref, sh_ref, sc_ref, lg_ref, lb_ref, x1_ref, u_ref, up_ref, *, alpha):
    v = alpha * x_ref[...] + g1_ref[...] * y_ref[...]
    x1 = _ln_rows(v) * lg_ref[...] + lb_ref[...]
    x1_ref[...] = x1
    u = _ln_rows(x1) * (1.0 + sc_ref[...]) + sh_ref[...]
    ub = u.astype(BF16)
    u_ref[...] = ub
    words = _pack_pairs(ub.astype(F32))
    s2 = len(words)
    for s, w in enumerate(words):
        up_ref[pl.ds(s, ROW_TILE, stride=s2), :] = w


def _mid(x_all, y, mod_tiles, mod_index, ln_g, ln_b, alpha):
    m, d = x_all.shape
    s2 = d // 256
    row = pl.BlockSpec((ROW_TILE, d), lambda i: (i, 0))
    vec = pl.BlockSpec((1, d), lambda i: (0, 0))
    return pl.pallas_call(
        functools.partial(_mid_kernel, alpha=alpha),
        out_shape=(jax.ShapeDtypeStruct((m, d), F32), jax.ShapeDtypeStruct((m, d), BF16),
                   jax.ShapeDtypeStruct((m * s2, LANE), U32)),
        grid=(m // ROW_TILE,),
        in_specs=[row, row, _mod_spec(2, d, ROW_TILE, mod_index), _mod_spec(3, d, ROW_TILE, mod_index),
                  _mod_spec(4, d, ROW_TILE, mod_index), vec, vec],
        out_specs=(row, row, pl.BlockSpec((ROW_TILE * s2, LANE), lambda i: (i, 0))),
        compiler_params=_cparams(("arbitrary",)),
        name="ln1_mod2",
    )(x_all, y, mod_tiles, mod_tiles, mod_tiles, ln_g.reshape(1, d), ln_b.reshape(1, d))


def _cast_weights(i, pairs):
    @pl.when(i == 0)
    def _():
        for w_ref, wb_ref in pairs:
            wb_ref[...] = w_ref[...].astype(BF16)


def _inproj_kernel(mb_ref, xb_ref, u_ref, w_ref, x1_ref, x2_ref, cos_ref, sin_ref, o_ref, wb_ref,
                   *, tn, rope_patterns, shift_tiles, small_tile, small_split):
    j, i = pl.program_id(0), pl.program_id(1)
    d = w_ref.shape[0]
    rows = 512

    @pl.when(i == 0)
    def _():
        for shift, (lo, hi) in shift_tiles:
            @pl.when((j >= lo) & (j < hi))
            def _(shift=shift):
                for r0 in range(0, d, rows):
                    if shift == 0:
                        wb_ref[r0:r0 + rows, :] = w_ref[r0:r0 + rows, :].astype(BF16)
                    else:
                        cat = jnp.concatenate([w_ref[r0:r0 + rows, :], x1_ref[r0:r0 + rows, :]], axis=1)
                        wb_ref[r0:r0 + rows, :] = pltpu.roll(cat, tn + LANE - shift, axis=1)[:, :tn].astype(BF16)

        @pl.when(j == small_tile)
        def _():
            wb_ref[...] = jnp.zeros_like(wb_ref)
            lane = lax.broadcasted_iota(I32, (d, LANE), 1)
            wb_ref[:, :LANE] = jnp.where(lane < small_split[0], x1_ref[...],
                                         jnp.where(lane < small_split[1], x2_ref[...], 0.0)).astype(BF16)

    acc = _dot(u_ref[...], wb_ref[...])
    plain = True
    for flags, tiles in rope_patterns:
        here = functools.reduce(jnp.logical_or, [j == t for t in tiles])
        plain = jnp.logical_and(plain, jnp.logical_not(here))

        @pl.when(here)
        def _(flags=flags):
            cos, sin = cos_ref[...], sin_ref[...]
            for h, rotate in enumerate(flags):
                a = acc[:, h * HD:(h + 1) * HD]
                o_ref[:, h * HD:(h + 1) * HD] = (a * cos + pltpu.roll(a, HD // 2, axis=1) * sin) if rotate else a

    @pl.when(plain)
    def _():
        o_ref[...] = acc


def _inproj(u, w_src, cos_t, sin_t, rope_cols, segments, small, n_p, tm, tn):
    m, d = u.shape
    n_tiles = n_p // tn
    by_flags = {}
    for t in range(n_tiles):
        flags = tuple(any(lo <= t * tn + h * HD < hi for lo, hi in rope_cols) for h in range(tn // HD))
        if any(flags):
            by_flags.setdefault(flags, []).append(t)
    rope_patterns = tuple((f, tuple(ts)) for f, ts in by_flags.items())
    small_off, (blk1, end1), (blk2, end2) = small
    assert small_off % tn == 0 and small_off // tn == n_tiles - 1
    assert all(lo % tn == 0 and hi % tn == 0 and 0 <= sh < LANE for lo, hi, sh in segments)
    main_blk = np.arange(n_tiles, dtype=np.int32)
    main_blk[-1] = n_tiles - 2
    last_x = (w_src.shape[1] - 1) // LANE
    x1_blk = np.minimum((np.arange(n_tiles, dtype=np.int32) + 1) * (tn // LANE), last_x)
    x1_blk[-1] = blk1
    shift_tiles = tuple((sh, (lo // tn, hi // tn)) for lo, hi, sh in segments)
    return pl.pallas_call(
        functools.partial(_inproj_kernel, tn=tn, rope_patterns=rope_patterns, shift_tiles=shift_tiles,
                          small_tile=n_tiles - 1, small_split=(end1, end2)),
        out_shape=jax.ShapeDtypeStruct((m, n_p), F32),
        grid_spec=pltpu.PrefetchScalarGridSpec(
            num_scalar_prefetch=2,
            grid=(n_tiles, m // tm),
            in_specs=[pl.BlockSpec((tm, d), lambda j, i, mb, xb: (i, 0)),
                      pl.BlockSpec((d, tn), lambda j, i, mb, xb: (0, mb[j])),
                      pl.BlockSpec((d, LANE), lambda j, i, mb, xb: (0, xb[j])),
                      pl.BlockSpec((d, LANE), lambda j, i, mb, xb: (0, blk2)),
                      pl.BlockSpec((tm, HD), lambda j, i, mb, xb: (i, 0)),
                      pl.BlockSpec((tm, HD), lambda j, i, mb, xb: (i, 0))],
            out_specs=pl.BlockSpec((tm, tn), lambda j, i, mb, xb: (i, j)),
            scratch_shapes=[pltpu.VMEM((d, tn), BF16)]),
        compiler_params=_cparams(("arbitrary", "arbitrary")),
        name="in_proj",
    )(jnp.asarray(main_blk), jnp.asarray(x1_blk), u, w_src, w_src, w_src, cos_t, sin_t)


def _merge_kernel(a1_ref, a2_ref, w1_ref, w2_ref, g1_ref, g2_ref, o_ref, wb1_ref, wb2_ref):
    _cast_weights(pl.program_id(1), ((w1_ref, wb1_ref), (w2_ref, wb2_ref)))
    z = (jax.nn.sigmoid(g1_ref[...]) * _dot(a1_ref[...], wb1_ref[...])
         + jax.nn.sigmoid(g2_ref[...]) * _dot(a2_ref[...], wb2_ref[...]))
    o_ref[...] = z.astype(BF16)


def _merge(og, on, w1, w2, parts, off1, off2, tm, tn):
    m, k = og.shape
    n = w1.shape[1]
    return pl.pallas_call(
        _merge_kernel,
        out_shape=jax.ShapeDtypeStruct((m, n), BF16),
        grid=(n // tn, m // tm),
        in_specs=[pl.BlockSpec((tm, k), lambda j, i: (i, 0)),
                  pl.BlockSpec((tm, k), lambda j, i: (i, 0)),
                  pl.BlockSpec((k, tn), lambda j, i: (0, j)),
                  pl.BlockSpec((k, tn), lambda j, i: (0, j)),
                  pl.BlockSpec((tm, tn), lambda j, i: (i, off1 // tn + j)),
                  pl.BlockSpec((tm, tn), lambda j, i: (i, off2 // tn + j))],
        out_specs=pl.BlockSpec((tm, tn), lambda j, i: (i, j)),
        scratch_shapes=[pltpu.VMEM((k, tn), BF16), pltpu.VMEM((k, tn), BF16)],
        compiler_params=_cparams(("arbitrary", "arbitrary")),
        name="branch_merge",
    )(og, on, w1, w2, parts, parts)


def _plain_kernel(a_ref, w_ref, o_ref, wb_ref):
    _cast_weights(pl.program_id(1), ((w_ref, wb_ref),))
    o_ref[...] = _dot(a_ref[...], wb_ref[...])


def _plain(a, w, tm, tn, name):
    m, k = a.shape
    n = w.shape[1]
    return pl.pallas_call(
        _plain_kernel,
        out_shape=jax.ShapeDtypeStruct((m, n), F32),
        grid=(n // tn, m // tm),
        in_specs=[pl.BlockSpec((tm, k), lambda j, i: (i, 0)),
                  pl.BlockSpec((k, tn), lambda j, i: (0, j))],
        out_specs=pl.BlockSpec((tm, tn), lambda j, i: (i, j)),
        scratch_shapes=[pltpu.VMEM((k, tn), BF16)],
        compiler_params=_cparams(("arbitrary", "arbitrary")),
        name=name,
    )(a, w)


def _swiglu_kernel(a_ref, w1_ref, w3_ref, o_ref, wb1_ref, wb3_ref):
    _cast_weights(pl.program_id(1), ((w1_ref, wb1_ref), (w3_ref, wb3_ref)))
    a = a_ref[...]
    o_ref[...] = (_silu(_dot(a, wb1_ref[...])) * _dot(a, wb3_ref[...])).astype(BF16)


def _swiglu(a, w1, w3, tm, tn):
    m, k = a.shape
    n = w1.shape[1]
    return pl.pallas_call(
        _swiglu_kernel,
        out_shape=jax.ShapeDtypeStruct((m, n), BF16),
        grid=(n // tn, m // tm),
        in_specs=[pl.BlockSpec((tm, k), lambda j, i: (i, 0)),
                  pl.BlockSpec((k, tn), lambda j, i: (0, j)),
                  pl.BlockSpec((k, tn), lambda j, i: (0, j))],
        out_specs=pl.BlockSpec((tm, tn), lambda j, i: (i, j)),
        scratch_shapes=[pltpu.VMEM((k, tn), BF16), pltpu.VMEM((k, tn), BF16)],
        compiler_params=_cparams(("arbitrary", "arbitrary")),
        name="shared_swiglu",
    )(a, w1, w3)


def _gla_kernel(q_ref, k_ref, v_ref, r_ref, a_ref, wa_ref, ba_ref, ng_ref, s0_ref,
                og_ref, sf_ref, st_ref, *, tb, chunk, s_valid):
    i = pl.program_id(2)

    @pl.when(i == 0)
    def _():
        st_ref[...] = s0_ref[...]

    x = jnp.dot(a_ref[:, :GLA_RANK], wa_ref[...], precision=HIGHEST,
                preferred_element_type=F32) + ba_ref[...]
    glog = (jnp.minimum(x, 0.0) - jnp.log1p(jnp.exp(-jnp.abs(x)))) * (1.0 / GLA_GATE_NORM)
    row = i * tb + lax.broadcasted_iota(I32, (tb, 1), 0)
    glog = jnp.where(row < s_valid, glog, 0.0)

    sub = min(GLA_SUB, chunk)
    rr = lax.broadcasted_iota(I32, (chunk, chunk), 0)
    cc = lax.broadcasted_iota(I32, (chunk, chunk), 1)
    tri = (rr >= cc).astype(F32)
    ones = jnp.ones((chunk, GLA_DK), F32)
    ng = ng_ref[...]

    for c in range(tb // chunk):
        lo = c * chunk
        g = glog[lo:lo + chunk]
        bc = jnp.dot(tri, g, precision=HIGHEST, preferred_element_type=F32)
        bl = bc[chunk - 1:chunk]
        q = q_ref[lo:lo + chunk, :] * (GLA_DK ** -0.5)
        k = k_ref[lo:lo + chunk, :]
        vb = v_ref[lo:lo + chunk, :].astype(BF16)
        qg = (q * jnp.exp(bc)).astype(BF16)
        kd = (k * jnp.exp(bl - bc)).astype(BF16)
        state = st_ref[...]
        o_inter = _dot(qg, state.astype(BF16))
        outs = []
        for sb in range(chunk // sub):
            r0, r1 = sb * sub, (sb + 1) * sub
            base = bc[r0 - 1:r0] if sb > 0 else jnp.zeros((1, GLA_DK), F32)
            qs = (q[r0:r1] * jnp.exp(bc[r0:r1] - base)).astype(BF16)
            ks = (k[:r1] * jnp.exp(base - bc[:r1])).astype(BF16)
            att = _dot_nt(qs, ks)
            causal = (r0 + lax.broadcasted_iota(I32, (sub, r1), 0)) >= lax.broadcasted_iota(I32, (sub, r1), 1)
            att = jnp.where(causal, att, 0.0)
            outs.append(_dot(att.astype(BF16), vb[:r1]))
        o = jnp.concatenate(outs, axis=0) + o_inter
        dec = jnp.exp(_dot_tn(g, ones, precision=HIGHEST))
        dec = jnp.concatenate([dec] * (GLA_DV // GLA_DK), axis=1)
        st_ref[...] = dec * state + _dot_tn(kd, vb)
        o = o * lax.rsqrt(jnp.mean(o * o, axis=-1, keepdims=True) + LN_EPS) * ng
        og_ref[lo:lo + chunk, :] = (o * _silu(r_ref[lo:lo + chunk, :])).astype(BF16)

    @pl.when(i == pl.num_programs(2) - 1)
    def _():
        sf_ref[...] = st_ref[...]


def _gla(src, nseq, s_pad, s_valid, tb, chunk, dst, s0, w_a2, b_a2, norm_g):
    nb = s_pad // tb
    rows = nseq * s_pad
    qo, ko = dst['gla_q'] // GLA_DK, dst['gla_k'] // GLA_DK
    vo, ro = dst['gla_v'] // GLA_DV, dst['gla_r'] // GLA_DV
    ao = dst['gla_a'] // LANE
    rowblk = lambda b, h, i: b * nb + i
    st_spec = pl.BlockSpec((None, None, GLA_DK, GLA_DV), lambda b, h, i: (b, h, 0, 0))
    return pl.pallas_call(
        functools.partial(_gla_kernel, tb=tb, chunk=chunk, s_valid=s_valid),
        out_shape=(jax.ShapeDtypeStruct((rows, GLA_HEADS * GLA_DV), BF16),
                   jax.ShapeDtypeStruct((nseq, GLA_HEADS, GLA_DK, GLA_DV), F32)),
        grid=(nseq, GLA_HEADS, nb),
        in_specs=[pl.BlockSpec((tb, GLA_DK), lambda b, h, i: (rowblk(b, h, i), qo + h)),
                  pl.BlockSpec((tb, GLA_DK), lambda b, h, i: (rowblk(b, h, i), ko + h)),
                  pl.BlockSpec((tb, GLA_DV), lambda b, h, i: (rowblk(b, h, i), vo + h)),
                  pl.BlockSpec((tb, GLA_DV), lambda b, h, i: (rowblk(b, h, i), ro + h)),
                  pl.BlockSpec((tb, LANE), lambda b, h, i: (rowblk(b, h, i), ao)),
                  pl.BlockSpec((GLA_RANK, GLA_DK), lambda b, h, i: (0, h)),
                  pl.BlockSpec((1, GLA_DK), lambda b, h, i: (0, h)),
                  pl.BlockSpec((1, GLA_DV), lambda b, h, i: (0, 0)),
                  st_spec],
        out_specs=(pl.BlockSpec((tb, GLA_DV), lambda b, h, i: (rowblk(b, h, i), h)), st_spec),
        scratch_shapes=[pltpu.VMEM((GLA_DK, GLA_DV), F32)],
        compiler_params=_cparams(("arbitrary", "arbitrary", "arbitrary")),
        name="gla_scan",
    )(src, src, src, src, src, w_a2, b_a2.reshape(1, -1), norm_g.reshape(1, -1), s0)


def _rope_heads(x, cos, sin):
    return jnp.concatenate(
        [x[:, h * HD:(h + 1) * HD] * cos + pltpu.roll(x[:, h * HD:(h + 1) * HD], HD // 2, axis=1) * sin
         for h in range(x.shape[1] // HD)], axis=1)


def _compress_rows(x, wk_ref, wv_ref, pek_ref, pev_ref, cos, sin):
    rows, width = x.shape
    nb, half = rows // CMP_BLOCK, width // 2

    def softmax_col(w_ref):
        w = w_ref[...]
        e = jnp.exp(w - jnp.max(w, axis=0, keepdims=True))
        return e / jnp.sum(e, axis=0, keepdims=True)

    wk, wv = softmax_col(wk_ref), softmax_col(wv_ref)
    w2 = jnp.concatenate([jnp.broadcast_to(wk, (CMP_BLOCK, half)),
                          jnp.broadcast_to(wv, (CMP_BLOCK, half))], axis=1)
    y = jnp.sum(x.reshape(nb, CMP_BLOCK, width) * w2[None], axis=1)
    pk = jnp.sum(pek_ref[...] * wk, axis=0, keepdims=True)
    pv = jnp.sum(pev_ref[...] * wv, axis=0, keepdims=True)
    kb = y[:, :half] + jnp.concatenate([pk] * NSA_G, axis=1)
    vb = y[:, half:] + jnp.concatenate([pv] * NSA_G, axis=1)
    return jnp.concatenate([_rope_heads(kb, cos, sin), vb], axis=1)


def _compress_prompt_kernel(x_ref, wk_ref, wv_ref, pek_ref, pev_ref, cos_ref, sin_ref, o_ref):
    o_ref[...] = _compress_rows(x_ref[...], wk_ref, wv_ref, pek_ref, pev_ref, cos_ref[...], sin_ref[...])


def _small_specs(nargs):
    col = pl.BlockSpec((CMP_BLOCK, 1), lambda *a: (0, 0))
    pe = pl.BlockSpec((CMP_BLOCK, HD), lambda *a: (0, 0))
    return [col, col, pe, pe]


def _compress_prompt(parts, rows, dst, w_k, w_v, pe_k, pe_v, cos_c, sin_c):
    width = 2 * NSA_G * HD
    tr = _divisor_tile(rows, 1024, 256)
    nbt = tr // CMP_BLOCK
    return pl.pallas_call(
        _compress_prompt_kernel,
        out_shape=jax.ShapeDtypeStruct((rows // CMP_BLOCK, width), F32),
        grid=(rows // tr,),
        in_specs=[pl.BlockSpec((tr, width), lambda i: (i, dst['nsa_kc'] // width))] + _small_specs(1)
        + [pl.BlockSpec((nbt, HD), lambda i: (i, 0)), pl.BlockSpec((nbt, HD), lambda i: (i, 0))],
        out_specs=pl.BlockSpec((nbt, width), lambda i: (i, 0)),
        compiler_params=_cparams(("arbitrary",)),
        name="nsa_compress_prompt",
    )(parts, w_k.reshape(-1, 1), w_v.reshape(-1, 1), pe_k, pe_v, cos_c, sin_c)


def _compress_pages_kernel(pt_ref, *refs, pg):
    page_refs = refs[:pg]
    wk_ref, wv_ref, pek_ref, pev_ref, cos_ref, sin_ref, o_ref = refs[pg:]

    def softmax_col(w_ref):
        w = w_ref[...]
        e = jnp.exp(w - jnp.max(w, axis=0, keepdims=True))
        return e / jnp.sum(e, axis=0, keepdims=True)

    wk, wv = softmax_col(wk_ref), softmax_col(wv_ref)
    blk_rows = CMP_BLOCK * KV_CH
    onehot = ((lax.broadcasted_iota(I32, (blk_rows, CMP_BLOCK), 0) >> 3)
              == lax.broadcasted_iota(I32, (blk_rows, CMP_BLOCK), 1)).astype(F32)
    spread = lambda w: jnp.dot(onehot, jnp.broadcast_to(w, (CMP_BLOCK, HD)), precision=HIGHEST,
                               preferred_element_type=F32)
    is_key = (lax.broadcasted_iota(I32, (blk_rows, HD), 0) & (KV_CH - 1)) < NSA_G
    wexp = jnp.where(is_key, spread(wk), spread(wv))
    pk = jnp.sum(pek_ref[...] * wk, axis=0, keepdims=True)
    pv = jnp.sum(pev_ref[...] * wv, axis=0, keepdims=True)
    pe8 = jnp.where(lax.broadcasted_iota(I32, (KV_CH, HD), 0) < NSA_G, pk, pv)
    outs = []
    for p in range(pg):
        x = page_refs[p][...]
        for n in range(x.shape[0] // blk_rows):
            xw = x[n * blk_rows:(n + 1) * blk_rows] * wexp
            acc = pe8
            for l in range(CMP_BLOCK):
                acc = acc + xw[l * KV_CH:(l + 1) * KV_CH]
            outs.append(acc)
    y = jnp.concatenate(outs, axis=0)
    o_ref[...] = y * cos_ref[...] + pltpu.roll(y, HD // 2, axis=1) * sin_ref[...]


def _compress_pages(cache2, page, page_table, w_k, w_v, pe_k, pe_v, cos_e, sin_e):
    nseq, n_pages = page_table.shape
    pg = _divisor_tile(n_pages, 8, 1)
    out_rows = page // CMP_BLOCK * KV_CH
    page_specs = [pl.BlockSpec((page * KV_CH, HD),
                               (lambda b, i, pt, p=p: (pt[b * n_pages + i * pg + p], 0)))
                  for p in range(pg)]
    small = [pl.BlockSpec((CMP_BLOCK, 1), lambda b, i, pt: (0, 0))] * 2 \
        + [pl.BlockSpec((CMP_BLOCK, HD), lambda b, i, pt: (0, 0))] * 2
    tab = [pl.BlockSpec((pg * out_rows, HD), lambda b, i, pt: (i, 0))] * 2
    return pl.pallas_call(
        functools.partial(_compress_pages_kernel, pg=pg),
        out_shape=jax.ShapeDtypeStruct((nseq, n_pages * out_rows, HD), F32),
        grid_spec=pltpu.PrefetchScalarGridSpec(
            num_scalar_prefetch=1,
            grid=(nseq, n_pages // pg),
            in_specs=page_specs + small + tab,
            out_specs=pl.BlockSpec((None, pg * out_rows, HD), lambda b, i, pt: (b, i, 0))),
        compiler_params=_cparams(("arbitrary", "arbitrary")),
        name="nsa_compress_pages",
    )(page_table.reshape(-1), *([cache2] * pg), w_k.reshape(-1, 1), w_v.reshape(-1, 1), pe_k, pe_v,
      cos_e, sin_e)


def _select_blocks(psum, cur, extra_forced, cnt_ref=None, cur_max=None):
    rows, nb = psum.shape
    lane = lax.broadcasted_iota(I32, (rows, nb), 1)
    jl = lane >> 1
    pair = jnp.where((lane & 1) == 0, psum + pltpu.roll(psum, nb - 1, axis=1),
                     psum + pltpu.roll(psum, 1, axis=1))
    forced = (jl == 0) | (jl == cur) | (jl == cur - 1)
    imp = jnp.where(forced, jnp.inf, jnp.where(jl <= cur, pair, -jnp.inf))
    base = jnp.where(imp < jnp.inf, float(extra_forced), 0.0)

    def count(cnt, j0, j1):
        for j in range(j0, j1):
            vj = imp[:, 2 * j:2 * j + 1]
            beats = (vj > imp) | ((vj == imp) & (jl > j))
            cnt = cnt + jnp.where(beats, 1.0, 0.0)
        return cnt

    if cnt_ref is None:
        return count(base, 0, nb // 2) < float(N_SEL)
    cnt_ref[...] = base
    group = 8
    for j0 in range(0, nb // 2, group):
        @pl.when(j0 <= cur_max)
        def _():
            cnt_ref[...] = count(cnt_ref[...], j0, j0 + group)
    return cnt_ref[...] < float(N_SEL)


def _flash_update(carry, s, mask, vb):
    m, l, acc = carry
    s = jnp.where(mask, s, NEG)
    m_new = jnp.maximum(m, jnp.max(s, axis=-1, keepdims=True))
    p = jnp.where(mask, jnp.exp(s - m_new), 0.0)
    alpha = jnp.exp(m - m_new)
    l = alpha * l + jnp.sum(p, axis=-1, keepdims=True)
    acc = alpha * acc + _dot(p.astype(BF16), vb)
    return m_new, l, acc


def _flash_step(carry, s, vb):
    m, l, acc = carry
    m_new = jnp.maximum(m, jnp.max(s, axis=-1, keepdims=True))
    p = jnp.exp(s - m_new)
    alpha = jnp.exp(m - m_new)
    l = alpha * l + jnp.sum(p, axis=-1, keepdims=True)
    acc = alpha * acc + _dot(p.astype(BF16), vb)
    return m_new, l, acc


def _flash_init(rows):
    return (jnp.full((rows, 1), NEG, F32), jnp.zeros((rows, 1), F32), jnp.zeros((rows, HD), F32))


def _compressed_branch(qb, kcb, vcb, qpos):
    s = _dot_nt(qb, kcb.astype(BF16))
    n = lax.broadcasted_iota(I32, s.shape, 1)
    vis = ((n + 1) * CMP_BLOCK - 1) <= qpos
    s = jnp.where(vis, s, NEG)
    p = jnp.where(vis, jnp.exp(s - jnp.max(s, axis=-1, keepdims=True)), 0.0)
    den = jnp.sum(p, axis=-1, keepdims=True)
    p = p / jnp.where(den > 0.0, den, 1.0)
    return _dot(p.astype(BF16), vcb.astype(BF16)), p


def _expand_blocks(sel_f, first_block, n_keys):
    nb = sel_f.shape[1]
    n = lax.broadcasted_iota(I32, (nb, n_keys), 0)
    kk = lax.broadcasted_iota(I32, (nb, n_keys), 1)
    e = (n == first_block + (kk >> 5)).astype(BF16)
    return _dot(sel_f.astype(BF16), e) > 0.5


def _nsa_prompt_kernel(q_ref, gt_ref, kcb_ref, vcb_ref, ks_ref, vs_ref, kw_ref, vw_ref, o_ref, cnt_ref,
                       *, tq, tk, gate_col):
    g, qi = pl.program_id(1), pl.program_id(2)
    q4 = q_ref[...]
    qb = (jnp.concatenate([q4[:, r * HD:(r + 1) * HD] for r in range(NSA_R)], axis=0)
          * (HD ** -0.5)).astype(BF16)
    rows = NSA_R * tq
    q0 = qi * tq
    qpos1 = q0 + lax.broadcasted_iota(I32, (tq, 1), 0)
    qpos = jnp.concatenate([qpos1] * NSA_R, axis=0)

    o_c, p_c = _compressed_branch(qb, kcb_ref[...], vcb_ref[...], qpos)
    psum = functools.reduce(jnp.add, [p_c[r * tq:(r + 1) * tq] for r in range(NSA_R)])
    sel = _select_blocks(psum, qpos1 >> 6, 0, cnt_ref, (q0 + tq - 1) >> 6)
    sel_b = jnp.where(sel, 1.0, 0.0).astype(BF16)
    nbl = sel_b.shape[1]
    e_n = lax.broadcasted_iota(I32, (nbl, tk), 0)
    e_k = lax.broadcasted_iota(I32, (nbl, tk), 1) >> 5

    def sel_tile(kt, carry, causal):
        start = pl.multiple_of(kt * tk, tk)
        kb = ks_ref[pl.ds(start, tk), :].astype(BF16)
        vb = vs_ref[pl.ds(start, tk), :].astype(BF16)
        picked = _dot(sel_b, (e_n == kt * (tk // CMP_BLOCK) + e_k).astype(BF16))
        bias = (picked - 1.0) * (-NEG)
        if causal:
            kpos = start + lax.broadcasted_iota(I32, (1, tk), 1)
            bias = jnp.where(kpos <= qpos1, bias, NEG)
        s = _dot_nt(qb, kb) + jnp.concatenate([bias] * NSA_R, axis=0)
        return _flash_step(carry, s, vb)

    n_kt = (q0 + tq + tk - 1) // tk
    carry = lax.fori_loop(0, n_kt - 1, lambda kt, c: sel_tile(kt, c, False), _flash_init(rows))
    m_s, l_s, acc_s = sel_tile(n_kt - 1, carry, True)
    o_s = acc_s / l_s

    carry = _flash_init(rows)
    for wt in reversed(range(WINDOW // tq + 1)):
        start = q0 - WINDOW + wt * tq
        cstart = pl.multiple_of(jnp.maximum(start, 0), tq)
        kb = kw_ref[pl.ds(cstart, tq), :].astype(BF16)
        vb = vw_ref[pl.ds(cstart, tq), :].astype(BF16)
        kpos = start + lax.broadcasted_iota(I32, (1, tq), 1)
        mask = (kpos <= qpos) & (kpos > qpos - WINDOW) & (kpos >= 0)
        carry = _flash_step(carry, jnp.where(mask, _dot_nt(qb, kb), NEG), vb)
    o_w = carry[2] / carry[1]

    gt = jax.nn.sigmoid(gt_ref[...])
    lane = lax.broadcasted_iota(I32, gt.shape, 1)
    outs = []
    for r in range(NSA_R):
        col = gate_col + (g * NSA_R + r) * 3
        gs = [jnp.sum(jnp.where(lane == col + j, gt, 0.0), axis=-1, keepdims=True) for j in range(3)]
        sl = slice(r * tq, (r + 1) * tq)
        outs.append(gs[0] * o_c[sl] + gs[1] * o_s[sl] + gs[2] * o_w[sl])
    o_ref[...] = jnp.concatenate(outs, axis=1).astype(BF16)


def _nsa_prompt(parts, kvb, nseq, s, dst):
    tq, tk = 256, 512
    nq = s // tq
    nb = s // CMP_BLOCK
    assert s % tk == 0 and WINDOW % tq == 0 and nb % LANE == 0
    gw = NSA_R * HD
    kv = lambda name: pl.BlockSpec((s, HD), lambda b, g, i: (b, dst[name] // HD + g))
    return pl.pallas_call(
        functools.partial(_nsa_prompt_kernel, tq=tq, tk=tk, gate_col=GLA_RANK),
        out_shape=jax.ShapeDtypeStruct((nseq * s, NSA_HEADS * HD), BF16),
        grid=(nseq, NSA_G, nq),
        in_specs=[pl.BlockSpec((tq, gw), lambda b, g, i: (b * nq + i, dst['nsa_q'] // gw + g)),
                  pl.BlockSpec((tq, LANE), lambda b, g, i: (b * nq + i, dst['gla_a'] // LANE)),
                  pl.BlockSpec((nb, HD), lambda b, g, i: (b, g)),
                  pl.BlockSpec((nb, HD), lambda b, g, i: (b, NSA_G + g)),
                  kv('nsa_ks'), kv('nsa_vs'), kv('nsa_kw'), kv('nsa_vw')],
        out_specs=pl.BlockSpec((tq, gw), lambda b, g, i: (b * nq + i, g)),
        scratch_shapes=[pltpu.VMEM((tq, nb), F32)],
        compiler_params=_cparams(("arbitrary", "arbitrary", "arbitrary")),
        name="nsa_prompt",
    )(parts, parts, kvb, kvb, parts, parts, parts, parts)


def _nsa_sample_kernel(pt_ref, *refs, pg, n_pages, t_new, t_pad, past_len, win_buf):
    page_refs = refs[:pg]
    q_ref, gt_ref, kvb_ref, win_ref, new_ref, o_ref, sel_ref, m_ref, l_ref, acc_ref, oc_ref = refs[pg:]
    i = pl.program_id(1)
    rows = NSA_R * t_pad
    page = page_refs[0].shape[0] // KV_CH
    kvw = NSA_G * HD
    nbc = kvb_ref.shape[0] // KV_CH
    chan = lambda ref, c, n: ref[pl.ds(c, n, stride=KV_CH), :]
    t1 = lax.broadcasted_iota(I32, (t_pad, 1), 0)
    t_row = jnp.concatenate([t1] * NSA_R, axis=0)
    qpos = past_len + t_row

    def queries(g):
        q = q_ref[...]
        return (jnp.concatenate([q[:, (g * NSA_R + r) * HD:(g * NSA_R + r + 1) * HD]
                                 for r in range(NSA_R)], axis=0) * (HD ** -0.5)).astype(BF16)

    @pl.when(i == 0)
    def _():
        for g in range(NSA_G):
            o_c, p_c = _compressed_branch(queries(g), chan(kvb_ref, g, nbc), chan(kvb_ref, NSA_G + g, nbc),
                                          qpos)
            psum = functools.reduce(jnp.add, [p_c[r * t_pad:(r + 1) * t_pad] for r in range(NSA_R)])
            cur = (past_len + t1) >> 6
            sel_ref[g] = _select_blocks(psum, cur, 1).astype(F32)
            oc_ref[g] = o_c
            m_ref[g] = jnp.full((rows, LANE), NEG, F32)
            l_ref[g] = jnp.zeros((rows, LANE), F32)
            acc_ref[g] = jnp.zeros((rows, HD), F32)

    bpp = page // CMP_BLOCK
    for g in range(NSA_G):
        qb = queries(g)
        carry = (m_ref[g][:, :1], l_ref[g][:, :1], acc_ref[g])
        kb = jnp.concatenate([chan(x, g, page).astype(BF16) for x in page_refs], axis=0)
        vb = jnp.concatenate([chan(x, NSA_G + g, page).astype(BF16) for x in page_refs], axis=0)
        m1 = _expand_blocks(sel_ref[g], i * pg * bpp, pg * page)
        mask = jnp.concatenate([m1] * NSA_R, axis=0)
        carry = _flash_update(carry, _dot_nt(qb, kb), mask, vb)
        m_ref[g] = jnp.broadcast_to(carry[0], (rows, LANE))
        l_ref[g] = jnp.broadcast_to(carry[1], (rows, LANE))
        acc_ref[g] = carry[2]

    @pl.when(i == pl.num_programs(1) - 1)
    def _():
        gt = jax.nn.sigmoid(gt_ref[...])
        new = new_ref[...]
        pad = jnp.zeros((LANE - t_pad, HD), F32)
        kk = lax.broadcasted_iota(I32, (1, LANE), 1)
        new_mask = (kk <= t_row) & (kk < t_new)
        wi = lax.broadcasted_iota(I32, (1, win_buf), 1)
        wpos = past_len - win_buf + wi
        win_mask = (wpos <= qpos) & (wpos > qpos - WINDOW) & (wpos >= 0)
        outs = []
        for g in range(NSA_G):
            qb = queries(g)

            def new_rows(j):
                return jnp.concatenate([new[:, j * kvw + g * HD:j * kvw + (g + 1) * HD], pad],
                                       axis=0).astype(BF16)

            carry = (m_ref[g][:, :1], l_ref[g][:, :1], acc_ref[g])
            carry = _flash_update(carry, _dot_nt(qb, new_rows(0)), new_mask, new_rows(1))
            o_s = carry[2] / carry[1]
            carry = _flash_init(rows)
            carry = _flash_update(carry, _dot_nt(qb, chan(win_ref, g, win_buf).astype(BF16)), win_mask,
                                  chan(win_ref, NSA_G + g, win_buf).astype(BF16))
            carry = _flash_update(carry, _dot_nt(qb, new_rows(2)), new_mask, new_rows(3))
            o_w = carry[2] / carry[1]
            o_c = oc_ref[g]
            for r in range(NSA_R):
                col = GLA_RANK + (g * NSA_R + r) * 3
                sl = slice(r * t_pad, (r + 1) * t_pad)
                outs.append(gt[:, col:col + 1] * o_c[sl] + gt[:, col + 1:col + 2] * o_s[sl]
                            + gt[:, col + 2:col + 3] * o_w[sl])
        o_ref[...] = jnp.concatenate(outs, axis=1)


def _nsa_sample(q_s, gt_s, kvb_s, slc2, page, win2, win_buf, new_s, page_table, t_new):
    nseq, t_pad, _ = q_s.shape
    width = KV_CH * HD
    n_pages = page_table.shape[1]
    past_len = n_pages * page
    nbc = kvb_s.shape[1] // KV_CH
    assert nbc % LANE == 0 and t_new <= SEL_BLOCK and past_len % SEL_BLOCK == 0
    pg = _divisor_tile(n_pages, 32, 1)
    rows = NSA_R * t_pad
    page_specs = [pl.BlockSpec((page * KV_CH, HD),
                               (lambda b, i, pt, p=p: (pt[b * n_pages + i * pg + p], 0)))
                  for p in range(pg)]
    per_seq = lambda shape: pl.BlockSpec((None,) + shape, lambda b, i, pt: (b, 0, 0))
    return pl.pallas_call(
        functools.partial(_nsa_sample_kernel, pg=pg, n_pages=n_pages, t_new=t_new, t_pad=t_pad,
                          past_len=past_len, win_buf=win_buf),
        out_shape=jax.ShapeDtypeStruct((nseq, t_pad, NSA_HEADS * HD), F32),
        grid_spec=pltpu.PrefetchScalarGridSpec(
            num_scalar_prefetch=1,
            grid=(nseq, n_pages // pg),
            in_specs=page_specs + [per_seq((t_pad, NSA_HEADS * HD)), per_seq((t_pad, LANE)),
                                   per_seq((nbc * KV_CH, HD)),
                                   pl.BlockSpec((win_buf * KV_CH, HD), lambda b, i, pt: (b, 0)),
                                   per_seq((t_pad, 2 * width))],
            out_specs=per_seq((t_pad, NSA_HEADS * HD)),
            scratch_shapes=[pltpu.VMEM((NSA_G, t_pad, nbc), F32),
                            pltpu.VMEM((NSA_G, rows, LANE), F32),
                            pltpu.VMEM((NSA_G, rows, LANE), F32),
                            pltpu.VMEM((NSA_G, rows, HD), F32),
                            pltpu.VMEM((NSA_G, rows, HD), F32)]),
        compiler_params=_cparams(("arbitrary", "arbitrary")),
        name="nsa_sample",
    )(page_table.reshape(-1), *([slc2] * pg), q_s, gt_s, kvb_s, win2, new_s)


def _router_kernel(u_ref, w_ref, b_ref, gate_ref, pos_ref, cnt_ref, carry_ref, *, n_exp):
    i = pl.program_id(0)
    tt = u_ref.shape[0]

    @pl.when(i == 0)
    def _():
        carry_ref[...] = jnp.zeros_like(carry_ref)

    logits = _dot(u_ref[...], w_ref[...].astype(BF16))
    sc = jax.nn.sigmoid(logits.T[:n_exp])
    biased = sc + b_ref[...]
    per = n_exp // N_GROUPS
    sub = lax.broadcasted_iota(I32, (per, tt), 0)
    gs_rows = []
    for gq in range(N_GROUPS):
        x8 = biased[gq * per:(gq + 1) * per]
        m1 = jnp.max(x8, axis=0, keepdims=True)
        first = jnp.min(jnp.where(x8 == m1, sub, per), axis=0, keepdims=True)
        m2 = jnp.max(jnp.where(sub == first, -jnp.inf, x8), axis=0, keepdims=True)
        gs_rows.append(m1 + m2)
    gs = jnp.concatenate(gs_rows, axis=0)
    gi = lax.broadcasted_iota(I32, gs.shape, 0)
    gcnt = jnp.zeros(gs.shape, F32)
    for j in range(N_GROUPS):
        vj = gs[j:j + 1]
        gcnt = gcnt + jnp.where((vj > gs) | ((vj == gs) & (gi > j)), 1.0, 0.0)
    gsel = jnp.where(gcnt < float(TOPK_GROUPS), 1.0, 0.0)
    emask = jnp.concatenate([jnp.broadcast_to(gsel[gq:gq + 1], (per, tt)) for gq in range(N_GROUPS)],
                            axis=0) > 0.5
    masked = jnp.where(emask, biased, -jnp.inf)
    ei = lax.broadcasted_iota(I32, masked.shape, 0)
    ecnt = jnp.zeros(masked.shape, F32)
    for j in range(n_exp):
        vj = masked[j:j + 1]
        ecnt = ecnt + jnp.where((vj > masked) | ((vj == masked) & (ei > j)), 1.0, 0.0)
    sel = ecnt < float(TOP_K)
    g = jnp.where(sel, sc, 0.0)
    gate_ref[...] = g / jnp.sum(g, axis=0, keepdims=True) * ROUTED_SCALE
    self_ = sel.astype(BF16)
    tr = lax.broadcasted_iota(I32, (tt, tt), 0)
    tc = lax.broadcasted_iota(I32, (tt, tt), 1)
    prefix = _dot(self_, (tr < tc).astype(BF16))
    carry = carry_ref[...]
    pos_ref[...] = jnp.where(sel, (prefix + carry).astype(I32), -1)
    carry = carry + jnp.sum(sel.astype(F32), axis=1, keepdims=True)
    carry_ref[...] = carry
    cnt_ref[...] = carry.astype(I32)


def _router(u, w_router, b_router):
    m, d = u.shape
    n_exp = w_router.shape[1]
    assert n_exp <= LANE and n_exp % (8 * N_GROUPS) == 0
    tt = LANE
    w_pad = jnp.pad(w_router, ((0, 0), (0, LANE - n_exp)))
    return pl.pallas_call(
        functools.partial(_router_kernel, n_exp=n_exp),
        out_shape=(jax.ShapeDtypeStruct((n_exp, m), F32), jax.ShapeDtypeStruct((n_exp, m), I32),
                   jax.ShapeDtypeStruct((n_exp, LANE), I32)),
        grid=(m // tt,),
        in_specs=[pl.BlockSpec((tt, d), lambda i: (i, 0)),
                  pl.BlockSpec((d, LANE), lambda i: (0, 0)),
                  pl.BlockSpec((n_exp, 1), lambda i: (0, 0))],
        out_specs=(pl.BlockSpec((n_exp, tt), lambda i: (0, i)),
                   pl.BlockSpec((n_exp, tt), lambda i: (0, i)),
                   pl.BlockSpec((n_exp, LANE), lambda i: (0, 0))),
        scratch_shapes=[pltpu.VMEM((n_exp, LANE), F32)],
        compiler_params=_cparams(("arbitrary",)),
        name="moe_router",
    )(u, w_pad, b_router.reshape(n_exp, 1))


def _slots_kernel(gate_ref, pos_ref, start_ref, dest_ref, gk_ref):
    pos = pos_ref[...]
    sel = pos >= 0
    n_exp, tt = pos.shape
    dest = start_ref[...] + pos
    er = lax.broadcasted_iota(I32, (n_exp, n_exp), 0)
    ec = lax.broadcasted_iota(I32, (n_exp, n_exp), 1)
    rank = _dot((ec < er).astype(BF16), sel.astype(BF16))
    gate = gate_ref[...]
    for k in range(TOP_K):
        pick = sel & (rank == float(k))
        dest_ref[k:k + 1, :] = jnp.sum(jnp.where(pick, dest, 0), axis=0, keepdims=True)
        gk_ref[k:k + 1, :] = jnp.sum(jnp.where(pick, gate, 0.0), axis=0, keepdims=True)


def _slots(gate_t, pos_t, start):
    n_exp, m = pos_t.shape
    tt = LANE
    return pl.pallas_call(
        _slots_kernel,
        out_shape=(jax.ShapeDtypeStruct((TOP_K, m), I32), jax.ShapeDtypeStruct((TOP_K, m), F32)),
        grid=(m // tt,),
        in_specs=[pl.BlockSpec((n_exp, tt), lambda i: (0, i)),
                  pl.BlockSpec((n_exp, tt), lambda i: (0, i)),
                  pl.BlockSpec((n_exp, 1), lambda i: (0, 0))],
        out_specs=(pl.BlockSpec((TOP_K, tt), lambda i: (0, i)),
                   pl.BlockSpec((TOP_K, tt), lambda i: (0, i))),
        compiler_params=_cparams(("arbitrary",)),
        name="moe_slots",
    )(gate_t, pos_t, start.reshape(n_exp, 1))


def _dispatch_kernel(meta_ref, tok_ref, tokn_ref, up_hbm, xs_ref, buf_ref, sem, *, rps, s2):
    i = pl.program_id(0)
    slot = lax.rem(i, 2)
    rows_used = meta_ref[0] * MOE_ROWS

    def issue(t_ref, sl):
        for a in range(rps // LANE):
            def start(b2, c, a=a):
                for q in range(2):
                    b = 2 * b2 + q
                    src = up_hbm.at[pl.ds(pl.multiple_of(t_ref[a, b] * s2, s2), s2)]
                    dst = buf_ref.at[sl, pl.ds(pl.multiple_of((a * LANE + b) * s2, s2), s2)]
                    pltpu.make_async_copy(src, dst, sem.at[sl]).start(priority=q)
                return c

            lax.fori_loop(0, LANE // 2, start, 0, unroll=4)

    @pl.when((i == 0) & (rows_used > 0))
    def _():
        issue(tok_ref, slot)

    @pl.when((i + 1 < pl.num_programs(0)) & ((i + 1) * rps < rows_used))
    def _():
        issue(tokn_ref, 1 - slot)

    @pl.when(i * rps < rows_used)
    def _():
        pltpu.make_async_copy(up_hbm.at[pl.ds(0, rps * s2)], buf_ref.at[slot], sem.at[slot]).wait()
        for s in range(s2):
            w = buf_ref[slot, pl.ds(s, rps, stride=s2), :]
            xs_ref[:, s * 256:s * 256 + LANE] = pltpu.bitcast(w << 16, F32).astype(BF16)
            xs_ref[:, s * 256 + LANE:(s + 1) * 256] = pltpu.bitcast(w & jnp.uint32(0xFFFF0000), F32).astype(BF16)

    @pl.when(i * rps >= rows_used)
    def _():
        xs_ref[...] = jnp.zeros_like(xs_ref)


def _dispatch(meta, row_tok3, up2, s2):
    n_steps, sub, _ = row_tok3.shape
    rps = sub * LANE
    assert up2.shape[0] >= rps * s2
    smem = lambda f: pl.BlockSpec((None, sub, LANE), lambda i, meta: (f(i), 0, 0), memory_space=pltpu.SMEM)
    return pl.pallas_call(
        functools.partial(_dispatch_kernel, rps=rps, s2=s2),
        out_shape=jax.ShapeDtypeStruct((n_steps * rps, 256 * s2), BF16),
        grid_spec=pltpu.PrefetchScalarGridSpec(
            num_scalar_prefetch=1,
            grid=(n_steps,),
            in_specs=[smem(lambda i: i), smem(lambda i: jnp.minimum(i + 1, n_steps - 1)),
                      pl.BlockSpec(memory_space=pl.ANY)],
            out_specs=pl.BlockSpec((rps, 256 * s2), lambda i, meta: (i, 0)),
            scratch_shapes=[pltpu.VMEM((2, rps * s2, LANE), U32), pltpu.SemaphoreType.DMA((2,))]),
        compiler_params=_cparams(("arbitrary",)),
        name="moe_dispatch",
    )(meta, row_tok3, row_tok3, up2)


def _blk(i, meta_ref):
    return jnp.minimum(i, meta_ref[0] - 1)


def _group_tables(blk_e, n_used):
    n_blk = blk_e.shape[0]
    idx = jnp.arange(n_blk, dtype=I32)
    first = (idx < n_used) & ((idx == 0) | (blk_e != jnp.roll(blk_e, 1)))
    grp = jnp.cumsum(first.astype(I32)) - 1
    starts = jnp.where(first, idx, n_blk)
    next_start = jnp.concatenate([lax.cummin(starts, reverse=True)[1:], jnp.full((1,), n_blk, I32)])
    nxt = jnp.where(next_start < n_blk, blk_e[jnp.minimum(next_start, n_blk - 1)], -1)
    return first.astype(I32), grp.astype(I32), nxt.astype(I32), jnp.sum(first.astype(I32))


def _weight_stream(meta_ref, be_ref, first_ref, grp_ref, nxt_ref, pass_id, n_pass, i, copies, sem, on_ready):
    blk = _blk(i, meta_ref)
    slot = lax.rem(pass_id * meta_ref[1] + grp_ref[blk], 2)

    @pl.when((first_ref[blk] == 1) & (i < meta_ref[0]))
    def _():
        @pl.when((i == 0) & (pass_id == 0))
        def _():
            for c in copies(be_ref[0], 0, slot):
                c.start()

        for c in copies(be_ref[blk], pass_id, slot):
            c.wait()
        nxt = nxt_ref[blk]

        @pl.when(nxt >= 0)
        def _():
            for c in copies(nxt, pass_id, 1 - slot):
                c.start()

        @pl.when((nxt < 0) & (pass_id + 1 < n_pass))
        def _():
            for c in copies(be_ref[0], pass_id + 1, 1 - slot):
                c.start()

        on_ready(slot)


def _expert_up_kernel(meta_ref, be_ref, first_ref, grp_ref, nxt_ref, x_ref, w1_hbm, w3_hbm, h_ref,
                      wf1_ref, wf3_ref, wb1_ref, wb3_ref, sem, *, th):
    hh, i = pl.program_id(0), pl.program_id(1)

    def copies(expert, pass_id, slot):
        cols = pl.ds(pl.multiple_of(pass_id * th, th), th)
        return [pltpu.make_async_copy(w_hbm.at[expert, :, cols], wf.at[slot], sem.at[slot])
                for w_hbm, wf in ((w1_hbm, wf1_ref), (w3_hbm, wf3_ref))]

    def on_ready(slot):
        wb1_ref[...] = wf1_ref[slot].astype(BF16)
        wb3_ref[...] = wf3_ref[slot].astype(BF16)

    _weight_stream(meta_ref, be_ref, first_ref, grp_ref, nxt_ref, hh, pl.num_programs(0), i, copies, sem,
                   on_ready)

    @pl.when(i < meta_ref[0])
    def _():
        x = x_ref[...]
        h_ref[...] = (_silu(_dot(x, wb1_ref[...])) * _dot(x, wb3_ref[...])).astype(BF16)

    @pl.when(i >= meta_ref[0])
    def _():
        h_ref[...] = jnp.zeros_like(h_ref)


def _expert_up(tables, xs, w1, w3, n_blk, rb):
    n_exp, d, hid = w1.shape
    th = min(hid, 512)
    return pl.pallas_call(
        functools.partial(_expert_up_kernel, th=th),
        out_shape=jax.ShapeDtypeStruct((n_blk * rb, hid), BF16),
        grid_spec=pltpu.PrefetchScalarGridSpec(
            num_scalar_prefetch=5,
            grid=(hid // th, n_blk),
            in_specs=[pl.BlockSpec((rb, d), lambda hh, i, meta, *_: (_blk(i, meta), 0)),
                      pl.BlockSpec(memory_space=pl.ANY), pl.BlockSpec(memory_space=pl.ANY)],
            out_specs=pl.BlockSpec((rb, th), lambda hh, i, *_: (i, hh)),
            scratch_shapes=[pltpu.VMEM((2, d, th), F32), pltpu.VMEM((2, d, th), F32),
                            pltpu.VMEM((d, th), BF16), pltpu.VMEM((d, th), BF16),
                            pltpu.SemaphoreType.DMA((2,))]),
        compiler_params=_cparams(("arbitrary", "arbitrary")),
        name="moe_expert_up",
    )(*tables, xs, w1, w3)


def _row_pitch(n_sub):
    return n_sub + 8


def _expert_down_kernel(meta_ref, be_ref, first_ref, grp_ref, nxt_ref, h_ref, w2_hbm, y_ref, wf_ref, wb_ref,
                        sem, *, rb, tn, pitch):
    i = pl.program_id(0)

    def copies(expert, pass_id, slot):
        return [pltpu.make_async_copy(w2_hbm.at[expert], wf_ref.at[slot], sem.at[slot])]

    def on_ready(slot):
        wb_ref[...] = wf_ref[slot].astype(BF16)

    _weight_stream(meta_ref, be_ref, first_ref, grp_ref, nxt_ref, 0, 1, i, copies, sem, on_ready)

    @pl.when(i < meta_ref[0])
    def _():
        h = h_ref[...]
        d = wb_ref.shape[1]
        for c in range(d // tn):
            y = _dot(h, wb_ref[:, c * tn:(c + 1) * tn])
            for s, w in enumerate(_pack_pairs(y.astype(BF16).astype(F32))):
                y_ref[pl.ds(c * (tn // 256) + s, rb, stride=pitch), :] = w
        for s in range(d // 256, pitch):
            y_ref[pl.ds(s, rb, stride=pitch), :] = jnp.zeros((rb, LANE), U32)

    @pl.when(i >= meta_ref[0])
    def _():
        y_ref[...] = jnp.zeros_like(y_ref)


def _expert_down(tables, h, w2, n_blk, rb):
    n_exp, hid, d = w2.shape
    pitch = _row_pitch(d // 256)
    return pl.pallas_call(
        functools.partial(_expert_down_kernel, rb=rb, tn=min(d, 512), pitch=pitch),
        out_shape=jax.ShapeDtypeStruct((n_blk * rb * pitch, LANE), U32),
        grid_spec=pltpu.PrefetchScalarGridSpec(
            num_scalar_prefetch=5,
            grid=(n_blk,),
            in_specs=[pl.BlockSpec((rb, hid), lambda i, meta, *_: (_blk(i, meta), 0)),
                      pl.BlockSpec(memory_space=pl.ANY)],
            out_specs=pl.BlockSpec((rb * pitch, LANE), lambda i, *_: (i, 0)),
            scratch_shapes=[pltpu.VMEM((2, hid, d), F32), pltpu.VMEM((hid, d), BF16),
                            pltpu.SemaphoreType.DMA((2,))]),
        compiler_params=_cparams(("arbitrary",)),
        name="moe_expert_down",
    )(*tables, h, w2)


def _final_kernel(dest_ref, destn_ref, gk_ref, y_hbm, x1_ref, ysh_ref, g2_ref, lg_ref, lb_ref, o_ref,
                  buf_ref, r_ref, sem, *, tf, alpha, pitch):
    i = pl.program_id(0)
    s2 = x1_ref.shape[1] // 256
    per_blk = ROW_TILE // tf
    slot = lax.rem(i, 2)

    def issue(d_ref, step, sl):
        off = lax.rem(step, per_blk) * tf

        def body(t, c):
            for k in range(TOP_K):
                pltpu.make_async_copy(y_hbm.at[d_ref[k, off + t], pl.ds(0, s2)], buf_ref.at[sl, k, t],
                                      sem.at[sl]).start(priority=k % 2)
            return c

        lax.fori_loop(0, tf, body, 0, unroll=2)

    @pl.when(i == 0)
    def _():
        issue(dest_ref, i, slot)

    @pl.when(i + 1 < pl.num_programs(0))
    def _():
        issue(destn_ref, i + 1, 1 - slot)

    for k in range(TOP_K):
        pltpu.make_async_copy(y_hbm.at[pl.ds(0, tf), pl.ds(0, s2)], buf_ref.at[slot, k],
                              sem.at[slot]).wait()

    off = lax.rem(i, per_blk) * tf

    def token(t, c):
        lo = hi = None
        for k in range(TOP_K):
            g = gk_ref[k, off + t]
            w = buf_ref[slot, k, t]
            tlo = g * pltpu.bitcast(w << 16, F32)
            thi = g * pltpu.bitcast(w & jnp.uint32(0xFFFF0000), F32)
            lo, hi = (tlo, thi) if lo is None else (lo + tlo, hi + thi)
        base = pl.multiple_of(t * pitch, 8)
        r_ref[pl.ds(base, s2), :] = lo
        r_ref[pl.ds(base + s2, s2), :] = hi
        return c

    lax.fori_loop(0, tf, token, 0, unroll=2)
    routed = jnp.concatenate(
        [r_ref[pl.ds(half * s2 + s, tf, stride=pitch), :] for s in range(s2) for half in range(2)], axis=1)
    v = alpha * x1_ref[...] + g2_ref[...] * (routed + ysh_ref[...])
    o_ref[...] = _ln_rows(v) * lg_ref[...] + lb_ref[...]


def _final(dest3, gk3, y3, x1, ysh, mod_tiles, mod_index, tf, ln_g, ln_b, alpha):
    m, d = x1.shape
    n_rows = y3.shape[0]
    s2 = d // 256
    pitch = _row_pitch(d // LANE)
    per_blk = ROW_TILE // tf
    n_steps = m // tf
    assert dest3.shape[2] == ROW_TILE and n_rows >= tf
    row = pl.BlockSpec((tf, d), lambda i: (i, 0))
    vec = pl.BlockSpec((1, d), lambda i: (0, 0))
    smem_blk = lambda f: pl.BlockSpec((None, TOP_K, ROW_TILE), lambda i: (f(i) // per_blk, 0, 0),
                                      memory_space=pltpu.SMEM)
    return pl.pallas_call(
        functools.partial(_final_kernel, tf=tf, alpha=alpha, pitch=pitch),
        out_shape=jax.ShapeDtypeStruct((m, d), F32),
        grid=(n_steps,),
        in_specs=[smem_blk(lambda i: i), smem_blk(lambda i: jnp.minimum(i + 1, n_steps - 1)),
                  smem_blk(lambda i: i),
                  pl.BlockSpec(memory_space=pl.ANY),
                  row, row, _mod_spec(5, d, tf, mod_index), vec, vec],
        out_specs=row,
        scratch_shapes=[pltpu.VMEM((2, TOP_K, tf, s2, LANE), U32),
                        pltpu.VMEM((tf * pitch, LANE), F32),
                        pltpu.SemaphoreType.DMA((2,))],
        compiler_params=_cparams(("arbitrary",)),
        name="moe_combine_ln2",
    )(dest3, dest3, gk3, y3, x1, ysh, mod_tiles, ln_g.reshape(1, d), ln_b.reshape(1, d))


def _rope_tables(pos):
    half = HD // 2
    inv = jnp.power(ROPE_THETA, -jnp.arange(half, dtype=F32) / half)
    ang = pos.astype(F32)[:, None] * inv[None, :]
    cos, sin = jnp.cos(ang), jnp.sin(ang)
    return jnp.concatenate([cos, cos], axis=1), jnp.concatenate([-sin, sin], axis=1)


def kernel(x_prompt, x_sample, c_prompt, c_sample, cache_cmp, cache_slc, cache_win, state_gla, page_table, w_ada, b_ada, w_in, w_gla_a2, b_gla_a2, gla_norm_g, cmp_pe_k, cmp_pe_v, cmp_w_k, cmp_w_v, w_br_gla, w_br_nsa, w_out, ln1_g, ln1_b, w_router, b_router, w_exp1, w_exp3, w_exp2, w_sh1, w_sh3, w_sh2, ln2_g, ln2_b):
    depth = w_in.shape[0]
    assert depth == 1
    alpha = (2.0 * depth) ** 0.25
    nb_, s_, d = x_prompt.shape
    db_, t_, _ = x_sample.shape
    mp, ns = nb_ * s_, db_ * t_
    m = mp + ns
    assert ns % ROW_TILE == 0 and s_ % ROW_TILE == 0 and d % 256 == 0
    n_pool, page = cache_cmp.shape[1], cache_cmp.shape[2]
    n_pages = page_table.shape[1]
    past_len = n_pages * page
    kvw2 = 2 * NSA_G * HD
    src, order, dst, n_used, _ = _col_plan(d)
    tn = 512
    small_off = dst['gla_a']
    dst = dict(dst)
    dst['nsa_g'] = small_off + GLA_RANK
    n_p = -(-(small_off + LANE) // tn) * tn

    rc = -(-(nb_ + db_) // 8) * 8
    c_all = jnp.pad(jnp.concatenate([c_prompt, c_sample], axis=0), ((0, rc - nb_ - db_), (0, 0)))
    mod = _ada(c_all, w_ada[0], b_ada[0])
    mod_tiles = jnp.concatenate(
        [jnp.broadcast_to(mod[:nb_, None, :], (nb_, ROW_TILE, 6 * d)),
         jnp.repeat(mod[nb_:nb_ + db_], t_, axis=0).reshape(ns // ROW_TILE, ROW_TILE, 6 * d)], axis=0)
    mod_index = _mod_index(ROW_TILE, mp, s_, nb_)

    x_all = jnp.concatenate([x_prompt.reshape(mp, d), x_sample.reshape(ns, d)], axis=0)
    u1 = _ln_mod(x_all, mod_tiles, mod_index)

    segments = []
    for name in order[:-2]:
        lo, shift = dst[name], src[name][0] - dst[name]
        if segments and segments[-1][2] == shift and segments[-1][1] == lo:
            segments[-1] = (segments[-1][0], lo + src[name][1], shift)
        else:
            segments.append((lo, lo + src[name][1], shift))
    a_src, g_src = src['gla_a'][0], src['nsa_g'][0]
    assert a_src % LANE == 0 and g_src % LANE == GLA_RANK
    small = (small_off, (a_src // LANE, GLA_RANK), (g_src // LANE, GLA_RANK + src['nsa_g'][1]))
    pos_all = jnp.concatenate([jnp.tile(jnp.arange(s_, dtype=I32), nb_),
                               jnp.tile(past_len + jnp.arange(t_, dtype=I32), db_)])
    cos_t, sin_t = _rope_tables(pos_all)
    tm = _divisor_tile(m, 1056, 16)
    rope_cols = ((dst['nsa_q'], dst['nsa_q'] + NSA_HEADS * HD),
                 (dst['nsa_ks'], dst['nsa_ks'] + NSA_G * HD),
                 (dst['nsa_kw'], dst['nsa_kw'] + NSA_G * HD))
    parts = _inproj(u1, w_in[0], cos_t, sin_t, rope_cols, tuple(segments), small, n_p, tm, tn)

    zeros_state = jnp.zeros((nb_, GLA_HEADS, GLA_DK, GLA_DV), F32)
    og_p, gla_p = _gla(parts, nb_, s_, s_, 512, 128, dst, zeros_state, w_gla_a2[0], b_gla_a2[0],
                       gla_norm_g[0])
    t_pad = 16
    gla_cols = dst['nsa_q']
    smp = parts[mp:].reshape(db_, t_, n_p)
    smp_pad = jnp.pad(smp, ((0, 0), (0, t_pad - t_), (0, 0))).reshape(db_ * t_pad, n_p)
    og_s, gla_s = _gla(smp_pad, db_, t_pad, t_, t_pad, t_pad, dst, state_gla[0], w_gla_a2[0],
                       b_gla_a2[0], gla_norm_g[0])
    og_s = og_s.reshape(db_, t_pad, -1)[:, :t_].reshape(ns, -1)
    og = jnp.concatenate([og_p, og_s], axis=0)

    cmp_pos_p = jnp.tile((jnp.arange(s_ // CMP_BLOCK, dtype=I32) + 1) * CMP_BLOCK - 1, nb_)
    cos_c, sin_c = _rope_tables(cmp_pos_p)
    kvb_p = _compress_prompt(parts, mp, dst, cmp_w_k[0], cmp_w_v[0], cmp_pe_k[0], cmp_pe_v[0], cos_c, sin_c)
    on_p = _nsa_prompt(parts, kvb_p, nb_, s_, dst)

    cmp_pos_s = (jnp.arange(past_len // CMP_BLOCK, dtype=I32) + 1) * CMP_BLOCK - 1
    cos_cs, sin_cs = _rope_tables(cmp_pos_s)
    is_key = (jnp.arange(KV_CH) < NSA_G)[None, :, None]
    cos_e = jnp.where(is_key, cos_cs[:, None, :], 1.0).reshape(-1, HD)
    sin_e = jnp.where(is_key, sin_cs[:, None, :], 0.0).reshape(-1, HD)
    kvb_s = _compress_pages(cache_cmp.reshape(-1, HD), page, page_table, cmp_w_k[0], cmp_w_v[0],
                            cmp_pe_k[0], cmp_pe_v[0], cos_e, sin_e)
    tq_pad = 8
    pad_t = lambda a: jnp.pad(a, ((0, 0), (0, tq_pad - t_), (0, 0)))
    q_s = pad_t(smp[:, :, dst['nsa_q']:dst['nsa_q'] + NSA_HEADS * HD])
    gt_s = pad_t(smp[:, :, small_off:small_off + LANE])
    new_s = pad_t(smp[:, :, dst['nsa_ks']:dst['nsa_ks'] + 2 * kvw2])
    on_s = _nsa_sample(q_s, gt_s, kvb_s, cache_slc.reshape(-1, HD), page, cache_win.reshape(-1, HD),
                       cache_win.shape[2], new_s, page_table, t_)
    on = jnp.concatenate([on_p, on_s[:, :t_].reshape(ns, -1).astype(BF16)], axis=0)

    z = _merge(og, on, w_br_gla[0], w_br_nsa[0], parts, dst['mg_gla'], dst['mg_nsa'], tm, tn)
    y = _plain(z, w_out[0], tm, tn, "out_proj")
    x1, u2, up = _mid(x_all, y, mod_tiles, mod_index, ln1_g[0], ln1_b[0], alpha)

    n_exp = w_router.shape[2]
    rb = MOE_ROWS
    s2 = d // 256
    gate_t, pos_t, cnt = _router(u2, w_router[0], b_router[0])
    counts = cnt[:, 0]
    padded = (counts + rb - 1) // rb * rb
    pend = jnp.cumsum(padded)
    start = pend - padded
    n_blk = -(-(m * TOP_K + n_exp * (rb - 1)) // rb)
    rps = 2 * rb
    n_blk = -(-(m * TOP_K + n_exp * (rb - 1)) // rps) * (rps // rb)
    blk_first = jnp.arange(n_blk, dtype=I32) * rb
    blk_e = jnp.minimum(jnp.sum(pend[None, :] <= blk_first[:, None], axis=1), n_exp - 1).astype(I32)
    meta = (pend[-1:] // rb).astype(I32)
    dest, gk = _slots(gate_t, pos_t, start.astype(I32))
    nt = m // ROW_TILE
    as_blocks = lambda a: a.reshape(TOP_K, nt, ROW_TILE).transpose(1, 0, 2)
    dest3, gk3 = as_blocks(dest), as_blocks(gk)
    row_tok = jnp.zeros((n_blk * rb,), I32).at[dest.reshape(-1)].set(
        jnp.tile(jnp.arange(m, dtype=I32), TOP_K), unique_indices=True)
    xs = _dispatch(meta, row_tok.reshape(-1, rps // LANE, LANE), up, s2)
    first, grp, nxt, n_groups = _group_tables(blk_e, meta[0])
    tables = (jnp.concatenate([meta, n_groups[None]]), blk_e, first, grp, nxt)
    h = _expert_up(tables, xs, w_exp1[0], w_exp3[0], n_blk, rb)
    y2 = _expert_down(tables, h, w_exp2[0], n_blk, rb)
    hs = _swiglu(u2, w_sh1[0], w_sh3[0], tm, tn // 2)
    ysh = _plain(hs, w_sh2[0], tm, tn, "shared_down")
    tf = ROW_TILE
    out = _final(dest3, gk3, y2.reshape(n_blk * rb, -1, LANE), x1, ysh, mod_tiles,
                 _mod_index(tf, mp, s_, nb_), tf, ln2_g[0], ln2_b[0], alpha)

    def rows(lo, n_rows, name, lead):
        return parts[lo:lo + n_rows, dst[name]:dst[name] + kvw2].reshape(lead + (2, NSA_G, HD))

    win_buf = cache_win.shape[2]
    win_p = rows(0, mp, 'nsa_kw', (nb_, s_))[:, s_ - win_buf:]
    win_s = jnp.concatenate([cache_win[0], rows(mp, ns, 'nsa_kw', (db_, t_))], axis=1)[:, t_:]
    return (out[:mp].reshape(nb_, s_, d), out[mp:].reshape(db_, t_, d),
            rows(0, mp, 'nsa_kc', (nb_, s_))[None], rows(mp, ns, 'nsa_kc', (db_, t_))[None],
            rows(0, mp, 'nsa_ks', (nb_, s_))[None], rows(mp, ns, 'nsa_ks', (db_, t_))[None],
            win_p[None], win_s[None], gla_p[None], gla_s[None])
```
